```python
import jax, jax.numpy as jnp
from jax import lax
import numpy as np

D_MODEL = 2048
BATCH = 2
SEQ = 16384
DEPTH = 4

N_EVEN = (DEPTH + 1) // 2
N_ODD = DEPTH // 2
N_VRES = max(N_EVEN - 1, 0)
NORM_EPS = 1e-6
FFN_DIM = 5632
MIX_WIDTH = D_MODEL
A_WIDTH = MIX_WIDTH // 2
A_HEAD_DIM = 64
A_HEADS = A_WIDTH // A_HEAD_DIM
A_DECAY_RANK = 64
A_ICLR_RANK = 64
A_GATE_RANK = 160
A_VRES_RANK = 32
A_GN_EPS = 64e-5
A_COLS = 3 * A_WIDTH + A_DECAY_RANK + A_ICLR_RANK + A_GATE_RANK
B_HEAD_DIM = 64
B_HEADS = (MIX_WIDTH - A_WIDTH) // B_HEAD_DIM
B_KV_HEADS = 4
B_GROUP = B_HEADS // B_KV_HEADS
B_COLS = (B_HEADS + 2 * B_KV_HEADS) * B_HEAD_DIM
WINDOW = 128
BLOCK = 128
E_IN_COLS = A_COLS + B_COLS
C_WIDTH = D_MODEL
C_GROUPS = 16
C_CHUNK = 128

kernel_name = 'hybrid_rwkv7_swa_gmlp_macaron'


def rms_norm(x, g):
    xf = x.astype(jnp.float32)
    y = xf * lax.rsqrt(jnp.mean(xf * xf, axis=-1, keepdims=True) + NORM_EPS)
    return (y * g.astype(jnp.float32)).astype(x.dtype)


def swiglu(h, wg, wu, wd):
    return (jax.nn.silu(h @ wg) * (h @ wu)) @ wd


def token_shift(z):
    return jnp.pad(z, ((0, 0), (1, 0), (0, 0)))[:, :-1]


def alibi_slopes(n):
    return 2.0 ** (-8.0 * jnp.arange(1, n + 1, dtype=jnp.float32) / n)


def rwkv7_scan(r, w, k, v, a_vec, b_vec):
    B, T, H, N = r.shape

    def step(S, inp):
        r_t, w_t, k_t, v_t, a_t, b_t = inp
        sa = jnp.einsum('bhvk,bhk->bhv', S, a_t)
        S = S * w_t[:, :, None, :] + sa[..., None] * b_t[:, :, None, :] + v_t[..., None] * k_t[:, :, None, :]
        return S, jnp.einsum('bhvk,bhk->bhv', S, r_t)

    xs = tuple(jnp.moveaxis(t, 1, 0) for t in (r, w, k, v, a_vec, b_vec))
    _, y = lax.scan(step, jnp.zeros((B, H, N, N), jnp.float32), xs)
    return jnp.moveaxis(y, 0, 1)


def rwkv7_mixer(z, mu, w0, w2, a0, a2, g2, k_k, k_a, r_k, gn_g, gn_b, v_first, vres):
    B, T, _ = z.shape
    z = z + (token_shift(z) - z) * mu
    idx = [A_WIDTH, 2 * A_WIDTH, 3 * A_WIDTH, 3 * A_WIDTH + A_DECAY_RANK,
           3 * A_WIDTH + A_DECAY_RANK + A_ICLR_RANK]
    r, k, v, xw, xa, xg = jnp.split(z, idx, axis=-1)
    if vres is not None:
        v0, v1, v2 = vres
        v = v + (v_first - v) * jax.nn.sigmoid(v0 + (v @ v1) @ v2)
    w_log = -jax.nn.softplus(-(w0 + jnp.tanh(xw) @ w2).astype(jnp.float32)) - 0.5
    decay = jnp.exp(-jnp.exp(w_log))
    a = jax.nn.sigmoid(a0 + xa @ a2)
    g = jax.nn.sigmoid(xg) @ g2
    hs = (A_HEADS, A_HEAD_DIM)
    heads = lambda t: t.reshape(B, T, A_HEADS, A_HEAD_DIM).astype(jnp.float32)
    r_h, k_h, v_h, a_h, w_h = heads(r), heads(k), heads(v), heads(a), heads(decay)
    kk = k_h * k_k.reshape(hs).astype(jnp.float32)
    kk = kk / jnp.maximum(jnp.linalg.norm(kk, axis=-1, keepdims=True), 1e-12)
    k_h = k_h * (1.0 + (a_h - 1.0) * k_a.reshape(hs).astype(jnp.float32))
    y = rwkv7_scan(r_h, w_h, k_h, v_h, -kk, kk * a_h)
    mean = jnp.mean(y, axis=-1, keepdims=True)
    var = jnp.mean(jnp.square(y - mean), axis=-1, keepdims=True)
    y = (y - mean) * lax.rsqrt(var + A_GN_EPS) * gn_g.reshape(hs).astype(jnp.float32) + gn_b.reshape(hs).astype(jnp.float32)
    y = y + jnp.sum(r_h * k_h * r_k.astype(jnp.float32), axis=-1, keepdims=True) * v_h
    out = y.reshape(B, T, A_WIDTH).astype(z.dtype) * g
    return out, v


def sliding_window_attention(q, k, v, sinks):
    B, T = q.shape[0], q.shape[1]
    nb = T // BLOCK
    qb = q.reshape(B, nb, BLOCK, B_KV_HEADS, B_GROUP, B_HEAD_DIM)

    def band(t):
        tb = t.reshape(B, nb, BLOCK, B_KV_HEADS, B_HEAD_DIM)
        prev = jnp.pad(tb, ((0, 0), (1, 0), (0, 0), (0, 0), (0, 0)))[:, :-1]
        return jnp.concatenate([prev, tb], axis=2)

    kb, vb = band(k), band(v)
    s = jnp.einsum('bnqhgd,bnkhd->bnhgqk', qb, kb).astype(jnp.float32) * (B_HEAD_DIM ** -0.5)
    dist = jnp.arange(BLOCK)[:, None] + BLOCK - jnp.arange(2 * BLOCK)[None, :]
    slopes = alibi_slopes(B_HEADS).reshape(B_KV_HEADS, B_GROUP)
    s = s - slopes[:, :, None, None] * dist.astype(jnp.float32)
    key_pos = jnp.arange(nb)[:, None] * BLOCK - BLOCK + jnp.arange(2 * BLOCK)[None, :]
    valid = ((dist >= 0) & (dist < WINDOW))[None] & (key_pos >= 0)[:, None, :]
    s = jnp.where(valid[None, :, None, None], s, -jnp.inf)
    sink = sinks.reshape(B_KV_HEADS, B_GROUP)[:, :, None, None].astype(jnp.float32)
    m = jnp.maximum(jnp.max(s, axis=-1, keepdims=True), sink)
    p = jnp.exp(s - m)
    p = p / (jnp.sum(p, axis=-1, keepdims=True) + jnp.exp(sink - m))
    o = jnp.einsum('bnhgqk,bnkhd->bnqhgd', p.astype(vb.dtype), vb)
    return o.reshape(B, T, B_HEADS * B_HEAD_DIM)


def even_mixer(h, w_in, b_qkv, mu, w0, w2, a0, a2, g2, k_k, k_a, r_k, gn_g, gn_b, sinks, w_out, v_first, vres):
    B, T, _ = h.shape
    z = h @ w_in
    z_a, z_b = z[..., :A_COLS], z[..., A_COLS:] + b_qkv
    o_a, v_a = rwkv7_mixer(z_a, mu, w0, w2, a0, a2, g2, k_k, k_a, r_k, gn_g, gn_b, v_first, vres)
    nq, nkv = B_HEADS * B_HEAD_DIM, B_KV_HEADS * B_HEAD_DIM
    q = z_b[..., :nq].reshape(B, T, B_HEADS, B_HEAD_DIM)
    k = z_b[..., nq:nq + nkv].reshape(B, T, B_KV_HEADS, B_HEAD_DIM)
    v = z_b[..., nq + nkv:].reshape(B, T, B_KV_HEADS, B_HEAD_DIM)
    o_b = sliding_window_attention(q, k, v, sinks)
    return jnp.concatenate([o_a, o_b], axis=-1) @ w_out, v_a


def odd_mixer(h, w_in, vn_g, w_s, b_s, w_out):
    B, T, _ = h.shape
    z = jax.nn.gelu(h @ w_in, approximate=False)
    u, v = z[..., :C_WIDTH], z[..., C_WIDTH:]
    v = rms_norm(v, vn_g)
    nc = T // C_CHUNK
    vc = v.reshape(B, nc, C_CHUNK, C_GROUPS, C_WIDTH // C_GROUPS)
    ws = jnp.tril(w_s)
    vm = jnp.einsum('gij,bnjgc->bnigc', ws, vc) + b_s.T[:, :, None]
    return (u * vm.reshape(B, T, C_WIDTH)) @ w_out


def setup_inputs(seed: int = 0) -> dict:
    key = jax.random.key(seed)
    ks = iter(jax.random.split(key, 40))
    nrm = lambda shape, scale: jax.random.normal(next(ks), shape, jnp.float32) * scale
    uni = lambda shape, lo, hi: jax.random.uniform(next(ks), shape, jnp.float32, lo, hi)
    D, F = D_MODEL, FFN_DIM
    return {
        'x': nrm((BATCH, SEQ, D), 1.0),
        'norm_g': 1.0 + nrm((DEPTH, 3, D), 0.05),
        'ffn_wg': nrm((DEPTH, 2, D, F), D ** -0.5),
        'ffn_wu': nrm((DEPTH, 2, D, F), D ** -0.5),
        'ffn_wd': nrm((DEPTH, 2, F, D), F ** -0.5),
        'e_w_in': nrm((N_EVEN, D, E_IN_COLS), D ** -0.5),
        'e_b_qkv': nrm((N_EVEN, B_COLS), 0.02),
        'e_mu': uni((N_EVEN, A_COLS), 0.0, 1.0),
        'e_w0': uni((N_EVEN, A_WIDTH), -6.0, -1.0),
        'e_w2': nrm((N_EVEN, A_DECAY_RANK, A_WIDTH), 0.5 * A_DECAY_RANK ** -0.5),
        'e_a0': nrm((N_EVEN, A_WIDTH), 0.1),
        'e_a2': nrm((N_EVEN, A_ICLR_RANK, A_WIDTH), 0.5 * A_ICLR_RANK ** -0.5),
        'e_g2': nrm((N_EVEN, A_GATE_RANK, A_WIDTH), A_GATE_RANK ** -0.5),
        'e_k_k': 0.85 + nrm((N_EVEN, A_WIDTH), 0.05),
        'e_k_a': 1.0 + nrm((N_EVEN, A_WIDTH), 0.05),
        'e_r_k': nrm((N_EVEN, A_HEADS, A_HEAD_DIM), 0.1),
        'e_gn_g': 1.0 + nrm((N_EVEN, A_WIDTH), 0.1),
        'e_gn_b': nrm((N_EVEN, A_WIDTH), 0.02),
        'e_sinks': nrm((N_EVEN, B_HEADS), 0.5),
        'e_w_out': nrm((N_EVEN, MIX_WIDTH, D), MIX_WIDTH ** -0.5),
        'vres_v0': 1.0 + nrm((N_VRES, A_WIDTH), 0.1),
        'vres_v1': nrm((N_VRES, A_WIDTH, A_VRES_RANK), A_WIDTH ** -0.5),
        'vres_v2': nrm((N_VRES, A_VRES_RANK, A_WIDTH), 0.5 * A_VRES_RANK ** -0.5),
        'o_w_in': nrm((N_ODD, D, 2 * C_WIDTH), D ** -0.5),
        'o_vn_g': 1.0 + nrm((N_ODD, C_WIDTH), 0.05),
        'o_w_s': nrm((N_ODD, C_GROUPS, C_CHUNK, C_CHUNK), C_CHUNK ** -0.5),
        'o_b_s': 1.0 + nrm((N_ODD, C_GROUPS, C_CHUNK), 0.1),
        'o_w_out': nrm((N_ODD, C_WIDTH, D), C_WIDTH ** -0.5),
        'final_g': 1.0 + nrm((D,), 0.05),
    }


def reference(x, norm_g, ffn_wg, ffn_wu, ffn_wd, e_w_in, e_b_qkv, e_mu, e_w0, e_w2, e_a0, e_a2,
              e_g2, e_k_k, e_k_a, e_r_k, e_gn_g, e_gn_b, e_sinks, e_w_out, vres_v0, vres_v1, vres_v2,
              o_w_in, o_vn_g, o_w_s, o_b_s, o_w_out, final_g):
    v_first = None
    for layer in range(DEPTH):
        x = x + 0.5 * swiglu(rms_norm(x, norm_g[layer, 0]), ffn_wg[layer, 0], ffn_wu[layer, 0], ffn_wd[layer, 0])
        h = rms_norm(x, norm_g[layer, 1])
        if layer % 2 == 0:
            i = layer // 2
            vres = None if i == 0 else (vres_v0[i - 1], vres_v1[i - 1], vres_v2[i - 1])
            y, v_a = even_mixer(h, e_w_in[i], e_b_qkv[i], e_mu[i], e_w0[i], e_w2[i], e_a0[i], e_a2[i],
                                e_g2[i], e_k_k[i], e_k_a[i], e_r_k[i], e_gn_g[i], e_gn_b[i], e_sinks[i],
                                e_w_out[i], v_first, vres)
            if i == 0:
                v_first = v_a
        else:
            j = layer // 2
            y = odd_mixer(h, o_w_in[j], o_vn_g[j], o_w_s[j], o_b_s[j], o_w_out[j])
        x = x + y
        x = x + 0.5 * swiglu(rms_norm(x, norm_g[layer, 2]), ffn_wg[layer, 1], ffn_wu[layer, 1], ffn_wd[layer, 1])
    return rms_norm(x, final_g)
```

```python
import functools
import math

import jax
import jax.numpy as jnp
from jax import lax
from jax.experimental import pallas as pl
from jax.experimental.pallas import tpu as pltpu

F32 = jnp.float32
BF16 = jnp.bfloat16

NORM_EPS = 1e-6
GN_EPS = 64e-5
HEAD_DIM = 64
A_HEADS = 16
A_WIDTH = A_HEADS * HEAD_DIM
B_HEADS = 16
B_KV_HEADS = 4
B_GROUP = B_HEADS // B_KV_HEADS
WINDOW = 128
DECAY_RANK = 64
ICLR_RANK = 64
GATE_RANK = 160
LOW_PAD = 384
VRES_PAD = 128
C_GROUPS = 16
C_CHUNK = 128
SCAN_CHUNK = 64

VMEM_LIMIT = 56 * 1024 * 1024


def _cparams(sem):
    return pltpu.CompilerParams(dimension_semantics=sem, vmem_limit_bytes=VMEM_LIMIT)


def _dot(a, b):
    return jnp.dot(a, b, preferred_element_type=F32)


def _dot_nt(a, b):
    return lax.dot_general(a, b, (((1,), (1,)), ((), ())), preferred_element_type=F32)


def _dot_tn(a, b):
    return lax.dot_general(a, b, (((0,), (0,)), ((), ())), preferred_element_type=F32)


def _split_dot(x, ones_bf16, left=False):
    x1 = x.astype(BF16)
    r1 = x - x1.astype(F32)
    x2 = r1.astype(BF16)
    x3 = (r1 - x2.astype(F32)).astype(BF16)
    if left:
        return _dot(ones_bf16, x1) + _dot(ones_bf16, x2) + _dot(ones_bf16, x3)
    return _dot(x1, ones_bf16) + _dot(x2, ones_bf16) + _dot(x3, ones_bf16)


def _rms(x, g):
    return x * lax.rsqrt(jnp.mean(x * x, axis=-1, keepdims=True) + NORM_EPS) * g


def _ffn_kernel(x_ref, g_ref, wg_ref, wu_ref, wd_ref, o_ref, h_ref):
    f = pl.program_id(1)

    @pl.when(f == 0)
    def _():
        x = x_ref[...]
        h_ref[...] = _rms(x, g_ref[...]).astype(BF16)
        o_ref[...] = x

    h = h_ref[...]
    a = _dot(h, wg_ref[...])
    b = _dot(h, wu_ref[...])
    p = (0.5 * a * jax.nn.sigmoid(a) * b).astype(BF16)
    o_ref[...] += _dot(p, wd_ref[...])


def _ffn(x, g, wg, wu, wd, tm=512, tf=512):
    n, d = x.shape
    fdim = wg.shape[1]
    return pl.pallas_call(
        _ffn_kernel,
        grid=(n // tm, fdim // tf),
        in_specs=[
            pl.BlockSpec((tm, d), lambda i, f: (i, 0)),
            pl.BlockSpec((1, d), lambda i, f: (0, 0)),
            pl.BlockSpec((d, tf), lambda i, f: (0, f)),
            pl.BlockSpec((d, tf), lambda i, f: (0, f)),
            pl.BlockSpec((tf, d), lambda i, f: (f, 0)),
        ],
        out_specs=pl.BlockSpec((tm, d), lambda i, f: (i, 0)),
        out_shape=jax.ShapeDtypeStruct((n, d), F32),
        scratch_shapes=[pltpu.VMEM((tm, d), BF16)],
        compiler_params=_cparams(("parallel", "arbitrary")),
        name="ffn",
    )(x, g.reshape(1, d), wg, wu, wd)


def _proj_kernel(x_ref, g_ref, w_ref, b_ref, o_ref, h_ref, *, gelu):
    @pl.when(pl.program_id(1) == 0)
    def _():
        h_ref[...] = _rms(x_ref[...], g_ref[...]).astype(BF16)

    z = _dot(h_ref[...], w_ref[...]) + b_ref[...]
    if gelu:
        z = 0.5 * z * (1.0 + lax.erf(z * (2.0 ** -0.5)))
    o_ref[...] = z


def _proj(x, g, w, b, gelu, tm=512, tn=1024):
    n, d = x.shape
    cols = w.shape[1]
    return pl.pallas_call(
        functools.partial(_proj_kernel, gelu=gelu),
        grid=(n // tm, cols // tn),
        in_specs=[
            pl.BlockSpec((tm, d), lambda i, j: (i, 0)),
            pl.BlockSpec((1, d), lambda i, j: (0, 0)),
            pl.BlockSpec((d, tn), lambda i, j: (0, j)),
            pl.BlockSpec((1, tn), lambda i, j: (0, j)),
        ],
        out_specs=pl.BlockSpec((tm, tn), lambda i, j: (i, j)),
        out_shape=jax.ShapeDtypeStruct((n, cols), F32),
        scratch_shapes=[pltpu.VMEM((tm, d), BF16)],
        compiler_params=_cparams(("parallel", "arbitrary")),
        name="proj_gelu" if gelu else "proj",
    )(x, g.reshape(1, d), w, b.reshape(1, cols))


def _head_ones():
    r = lax.broadcasted_iota(jnp.int32, (A_WIDTH, A_WIDTH), 0) // HEAD_DIM
    c = lax.broadcasted_iota(jnp.int32, (A_WIDTH, A_WIDTH), 1) // HEAD_DIM
    return jnp.where(r == c, 1.0, 0.0).astype(BF16)


def _prep_kernel(*refs, seq_len, tm, vres):
    (zr_ref, zk_ref, zv_ref, zl_ref, pr_ref, pk_ref, pv_ref, plo_ref,
     mur_ref, muk_ref, muv_ref, mul_ref, w0_ref, w2_ref, a0_ref, a2_ref, g2_ref,
     kk_ref, ka_ref, rk_ref) = refs[:20]
    refs = refs[20:]
    if vres:
        vf_ref, v0_ref, v1_ref, v2_ref = refs[:4]
        refs = refs[4:]
    r_out, lw_out, k_out, v_out, kk_out, b_out, g_out, bonus_out = refs

    i = pl.program_id(0)
    keep_prev = ((i * tm) % seq_len != 0).astype(F32)
    row = lax.broadcasted_iota(jnp.int32, (tm, 1), 0)

    def shifted(z_ref, p_ref, mu_ref):
        z = z_ref[...]
        prev_row = p_ref[7:8, :] * keep_prev
        prev = jnp.where(row == 0, prev_row, pltpu.roll(z, 1, axis=0))
        return z + (prev - z) * mu_ref[...]

    r = shifted(zr_ref, pr_ref, mur_ref)
    k = shifted(zk_ref, pk_ref, muk_ref)
    v = shifted(zv_ref, pv_ref, muv_ref)
    low = shifted(zl_ref, plo_ref, mul_ref)

    if vres:
        mix = _dot(_dot(v.astype(BF16), v1_ref[...]).astype(BF16), v2_ref[...])
        v = v + (vf_ref[...] - v) * jax.nn.sigmoid(v0_ref[...] + mix)

    dw = w0_ref[...] + _dot(jnp.tanh(low).astype(BF16), w2_ref[...])
    lw_out[...] = -math.exp(-0.5) * jax.nn.sigmoid(dw)
    a = jax.nn.sigmoid(a0_ref[...] + _dot(low.astype(BF16), a2_ref[...]))
    g_out[...] = _dot(jax.nn.sigmoid(low).astype(BF16), g2_ref[...])

    ones = _head_ones()
    kk = k * kk_ref[...]
    nrm = jnp.sqrt(_split_dot(kk * kk, ones))
    kk = kk / jnp.maximum(nrm, 1e-12)
    k = k * (1.0 + (a - 1.0) * ka_ref[...])
    r_out[...] = r
    k_out[...] = k
    v_out[...] = v
    kk_out[...] = kk
    b_out[...] = kk * a
    bonus_out[...] = _split_dot(r * k * rk_ref[...], ones) * v


def _rwkv_prep(z, seq_len, mu_r, mu_k, mu_v, mu_l, w0, w2p, a0, a2p, g2p, k_k, k_a, r_k,
               vres, tm=256):
    n = z.shape[0]
    aw = A_WIDTH
    low_blk = (3 * aw + B_HEADS * HEAD_DIM + 2 * B_KV_HEADS * HEAD_DIM) // LOW_PAD
    pstep = tm // 8

    def prev_map(col):
        return lambda i: (jnp.maximum(i * pstep - 1, 0), col)

    row_vec = lambda w: pl.BlockSpec((1, w), lambda i: (0, 0))
    full = lambda a: pl.BlockSpec(a.shape, lambda i: (0, 0))
    in_specs = [
        pl.BlockSpec((tm, aw), lambda i: (i, 0)),
        pl.BlockSpec((tm, aw), lambda i: (i, 1)),
        pl.BlockSpec((tm, aw), lambda i: (i, 2)),
        pl.BlockSpec((tm, LOW_PAD), lambda i: (i, low_blk)),
        pl.BlockSpec((8, aw), prev_map(0)),
        pl.BlockSpec((8, aw), prev_map(1)),
        pl.BlockSpec((8, aw), prev_map(2)),
        pl.BlockSpec((8, LOW_PAD), prev_map(low_blk)),
        row_vec(aw), row_vec(aw), row_vec(aw), row_vec(LOW_PAD),
        row_vec(aw), full(w2p), row_vec(aw), full(a2p), full(g2p),
        row_vec(aw), row_vec(aw), row_vec(aw),
    ]
    args = [z, z, z, z, z, z, z, z, mu_r, mu_k, mu_v, mu_l, w0, w2p, a0, a2p, g2p, k_k, k_a, r_k]
    if vres is not None:
        v_first, v0, v1p, v2p = vres
        in_specs += [pl.BlockSpec((tm, aw), lambda i: (i, 0)), row_vec(aw), full(v1p), full(v2p)]
        args += [v_first, v0, v1p, v2p]
    out = jax.ShapeDtypeStruct((n, aw), F32)
    return pl.pallas_call(
        functools.partial(_prep_kernel, seq_len=seq_len, tm=tm, vres=vres is not None),
        grid=(n // tm,),
        in_specs=in_specs,
        out_specs=[pl.BlockSpec((tm, aw), lambda i: (i, 0))] * 8,
        out_shape=[out] * 8,
        compiler_params=_cparams(("parallel",)),
        name="rwkv_prep",
    )(*args)


def _scan_kernel(r_ref, lw_ref, k_ref, v_ref, kk_ref, b_ref, y_ref, h_ref, *, chunk):
    c = chunk

    @pl.when(pl.program_id(1) == 0)
    def _():
        h_ref[...] = jnp.zeros_like(h_ref)

    ri = lax.broadcasted_iota(jnp.int32, (c, c), 0)
    ci = lax.broadcasted_iota(jnp.int32, (c, c), 1)
    incl = ri >= ci
    strict = ri > ci
    eye = jnp.where(ri == ci, 1.0, 0.0)
    lw = lw_ref[...]
    cum = _split_dot(lw, jnp.where(incl, 1.0, 0.0).astype(BF16), left=True)
    p_in = jnp.exp(cum)
    p_ex = jnp.exp(cum - lw)
    p_inv = jnp.exp(-cum)
    tail = cum[c - 1:c, :]
    p_tail = jnp.exp(tail - cum)
    p_all = jnp.exp(tail)
    kk = kk_ref[...]
    k = k_ref[...]
    b = b_ref[...]
    a_t = (-kk * p_ex).astype(BF16)
    r_t = (r_ref[...] * p_in).astype(BF16)
    b_t = (b * p_inv).astype(BF16)
    k_t = (k * p_inv).astype(BF16)
    b_e = (b * p_tail).astype(BF16)
    k_e = (k * p_tail).astype(BF16)
    vb = v_ref[...].astype(BF16)

    for h in range(A_HEADS):
        sl = slice(h * HEAD_DIM, (h + 1) * HEAD_DIM)
        ar = jnp.concatenate([a_t[:, sl], r_t[:, sl]], axis=0)
        bk = jnp.concatenate([b_t[:, sl], k_t[:, sl]], axis=0)
        s = _dot_nt(ar, bk)
        l_ab = jnp.where(strict, s[:c, :c], 0.0)
        l_ak = jnp.where(strict, s[:c, c:], 0.0)
        m_rb = jnp.where(incl, s[c:, :c], 0.0)
        m_rk = jnp.where(incl, s[c:, c:], 0.0)
        inv = eye + l_ab
        pw = l_ab
        for _ in range(int(math.log2(c)) - 1):
            pwb = pw.astype(BF16)
            pw = _dot(pwb, pwb)
            inv = inv + _dot(inv.astype(BF16), pw.astype(BF16))
        h0 = h_ref[h]
        ah = _dot(ar, h0.astype(BF16))
        vh = vb[:, sl]
        rhs = ah[:c] + _dot(l_ak.astype(BF16), vh)
        u = _dot(inv.astype(BF16), rhs.astype(BF16))
        ub = u.astype(BF16)
        y = ah[c:] + _dot(m_rb.astype(BF16), ub) + _dot(m_rk.astype(BF16), vh)
        y_ref[:, sl] = y
        decay = jnp.transpose(jnp.broadcast_to(p_all[:, sl], (HEAD_DIM, HEAD_DIM)))
        h_ref[h] = decay * h0 + _dot_tn(b_e[:, sl], ub) + _dot_tn(k_e[:, sl], vh)


def _rwkv_scan(r, lw, k, v, kk, b, batch, seq_len, chunk=SCAN_CHUNK):
    n, aw = r.shape
    nc = seq_len // chunk
    spec = pl.BlockSpec((chunk, aw), lambda bi, ci: (bi * nc + ci, 0))
    return pl.pallas_call(
        functools.partial(_scan_kernel, chunk=chunk),
        grid=(batch, nc),
        in_specs=[spec] * 6,
        out_specs=spec,
        out_shape=jax.ShapeDtypeStruct((n, aw), F32),
        scratch_shapes=[pltpu.VMEM((A_HEADS, HEAD_DIM, HEAD_DIM), F32)],
        compiler_params=_cparams(("parallel", "arbitrary")),
        name="rwkv_scan",
    )(r, lw, k, v, kk, b)


def _swa_kernel(sink_ref, q_ref, kc_ref, kp_ref, vc_ref, vp_ref, o_ref):
    blk = WINDOW
    first_key = jnp.where(pl.program_id(1) > 0, 0, blk)
    qi = lax.broadcasted_iota(jnp.int32, (blk, 2 * blk), 0)
    kj = lax.broadcasted_iota(jnp.int32, (blk, 2 * blk), 1)
    dist = qi + blk - kj
    valid = (dist >= 0) & (dist < WINDOW) & (kj >= first_key)
    distf = dist.astype(F32)
    scale = HEAD_DIM ** -0.5
    kband = jnp.concatenate([kp_ref[...], kc_ref[...]], axis=0).astype(BF16)
    vband = jnp.concatenate([vp_ref[...], vc_ref[...]], axis=0).astype(BF16)
    q = q_ref[...].astype(BF16)
    for h in range(B_HEADS):
        kv = h // B_GROUP
        slope = 2.0 ** (-8.0 * (h + 1) / B_HEADS)
        qh = q[:, h * HEAD_DIM:(h + 1) * HEAD_DIM]
        kh = kband[:, kv * HEAD_DIM:(kv + 1) * HEAD_DIM]
        vh = vband[:, kv * HEAD_DIM:(kv + 1) * HEAD_DIM]
        s = _dot_nt(qh, kh) * scale - slope * distf
        s = jnp.where(valid, s, -1e30)
        sink = sink_ref[h]
        m = jnp.maximum(jnp.max(s, axis=-1, keepdims=True), sink)
        p = jnp.exp(s - m)
        den = jnp.sum(p, axis=-1, keepdims=True) + jnp.exp(sink - m)
        o = _dot(p.astype(BF16), vh) / den
        o_ref[:, h * HEAD_DIM:(h + 1) * HEAD_DIM] = o


def _swa(z, sinks, batch, seq_len):
    n = z.shape[0]
    nb = seq_len // WINDOW
    qw = B_HEADS * HEAD_DIM
    kvw = B_KV_HEADS * HEAD_DIM
    q_blk = 3 * A_WIDTH // qw
    k_blk = (3 * A_WIDTH + qw) // kvw
    cur = lambda col: (lambda bi, j: (bi * nb + j, col))
    prev = lambda col: (lambda bi, j: (bi * nb + jnp.maximum(j - 1, 0), col))
    return pl.pallas_call(
        _swa_kernel,
        grid=(batch, nb),
        in_specs=[
            pl.BlockSpec(memory_space=pltpu.SMEM),
            pl.BlockSpec((WINDOW, qw), cur(q_blk)),
            pl.BlockSpec((WINDOW, kvw), cur(k_blk)),
            pl.BlockSpec((WINDOW, kvw), prev(k_blk)),
            pl.BlockSpec((WINDOW, kvw), cur(k_blk + 1)),
            pl.BlockSpec((WINDOW, kvw), prev(k_blk + 1)),
        ],
        out_specs=pl.BlockSpec((WINDOW, qw), lambda bi, j: (bi * nb + j, 0)),
        out_shape=jax.ShapeDtypeStruct((n, qw), F32),
        compiler_params=_cparams(("parallel", "arbitrary")),
        name="swa",
    )(sinks, z, z, z, z, z)


def _mix_out_kernel(x_ref, y_ref, bonus_ref, g_ref, ob_ref, gng_ref, gnb_ref, wa_ref, wb_ref, o_ref):
    ones = _head_ones()
    y = y_ref[...]
    mean = _split_dot(y, ones) * (1.0 / HEAD_DIM)
    yc = y - mean
    var = _split_dot(yc * yc, ones) * (1.0 / HEAD_DIM)
    yn = yc * lax.rsqrt(var + GN_EPS) * gng_ref[...] + gnb_ref[...]
    oa = (yn + bonus_ref[...]) * g_ref[...]
    o_ref[...] = (x_ref[...] + _dot(oa.astype(BF16), wa_ref[...])
                  + _dot(ob_ref[...].astype(BF16), wb_ref[...]))


def _mix_out(x, y, bonus, g, ob, gn_g, gn_b, wa, wb, tm=256):
    n, d = x.shape
    aw = A_WIDTH
    bw = ob.shape[1]
    rows = lambda w: pl.BlockSpec((tm, w), lambda i: (i, 0))
    row_vec = lambda w: pl.BlockSpec((1, w), lambda i: (0, 0))
    return pl.pallas_call(
        _mix_out_kernel,
        grid=(n // tm,),
        in_specs=[rows(d), rows(aw), rows(aw), rows(aw), rows(bw), row_vec(aw), row_vec(aw),
                  pl.BlockSpec((aw, d), lambda i: (0, 0)), pl.BlockSpec((bw, d), lambda i: (0, 0))],
        out_specs=rows(d),
        out_shape=jax.ShapeDtypeStruct((n, d), F32),
        compiler_params=_cparams(("parallel",)),
        name="mix_out",
    )(x, y, bonus, g, ob, gn_g.reshape(1, aw), gn_b.reshape(1, aw), wa, wb)


def _sgu_kernel(x_ref, u_ref, v_ref, vg_ref, ws_ref, bs_ref, wo_ref, o_ref, gate_ref, *, tm):
    cw = C_CHUNK
    vn = _rms(v_ref[...], vg_ref[...]).astype(BF16)
    for ch in range(tm // C_CHUNK):
        rows = slice(ch * C_CHUNK, (ch + 1) * C_CHUNK)
        for grp in range(C_GROUPS):
            cols = slice(grp * cw, (grp + 1) * cw)
            vm = _dot(ws_ref[grp], vn[rows, cols]) + bs_ref[grp]
            gate_ref[rows, cols] = (u_ref[rows, cols] * vm).astype(BF16)
    o_ref[...] = x_ref[...] + _dot(gate_ref[...], wo_ref[...])


def _sgu(x, z, vn_g, ws, bs, wo, tm=256):
    n, d = x.shape
    cwid = wo.shape[0]
    return pl.pallas_call(
        functools.partial(_sgu_kernel, tm=tm),
        grid=(n // tm,),
        in_specs=[
            pl.BlockSpec((tm, d), lambda i: (i, 0)),
            pl.BlockSpec((tm, cwid), lambda i: (i, 0)),
            pl.BlockSpec((tm, cwid), lambda i: (i, 1)),
            pl.BlockSpec((1, cwid), lambda i: (0, 0)),
            pl.BlockSpec(ws.shape, lambda i: (0, 0, 0)),
            pl.BlockSpec(bs.shape, lambda i: (0, 0, 0)),
            pl.BlockSpec((cwid, d), lambda i: (0, 0)),
        ],
        out_specs=pl.BlockSpec((tm, d), lambda i: (i, 0)),
        out_shape=jax.ShapeDtypeStruct((n, d), F32),
        scratch_shapes=[pltpu.VMEM((tm, cwid), BF16)],
        compiler_params=_cparams(("parallel",)),
        name="sgu",
    )(x, z, z, vn_g.reshape(1, cwid), ws, bs, wo)


def _final_norm_kernel(x_ref, g_ref, o_ref):
    o_ref[...] = _rms(x_ref[...], g_ref[...])


def _final_norm(x, g, tm=512):
    n, d = x.shape
    return pl.pallas_call(
        _final_norm_kernel,
        grid=(n // tm,),
        in_specs=[pl.BlockSpec((tm, d), lambda i: (i, 0)), pl.BlockSpec((1, d), lambda i: (0, 0))],
        out_specs=pl.BlockSpec((tm, d), lambda i: (i, 0)),
        out_shape=jax.ShapeDtypeStruct((n, d), F32),
        compiler_params=_cparams(("parallel",)),
        name="final_norm",
    )(x, g.reshape(1, d))


def _pad_rows(w, lo, total):
    return jnp.pad(w, ((lo, total - lo - w.shape[0]), (0, 0)))


def _even_layer(x, batch, seq_len, g, w_in, b_qkv, mu, w0, w2, a0, a2, g2, k_k, k_a, r_k,
                gn_g, gn_b, sinks, w_out, v_first, vres):
    aw = A_WIDTH
    a_cols = 3 * aw + DECAY_RANK + ICLR_RANK + GATE_RANK
    n_low = a_cols - 3 * aw
    b_cols = w_in.shape[1] - a_cols
    pad_low = LOW_PAD - n_low
    tail = (-(3 * aw + b_cols + LOW_PAD)) % 1024
    w_cat = jnp.concatenate([w_in[:, :3 * aw], w_in[:, a_cols:], w_in[:, 3 * aw:a_cols],
                             jnp.zeros((w_in.shape[0], pad_low + tail), F32)], axis=1).astype(BF16)
    b_cat = jnp.concatenate([jnp.zeros((3 * aw,), F32), b_qkv, jnp.zeros((LOW_PAD + tail,), F32)])
    z = _proj(x, g, w_cat, b_cat, gelu=False)

    mu_l = jnp.pad(mu[3 * aw:], (0, pad_low))
    row = lambda t: t.reshape(1, -1)
    w2p = _pad_rows(w2, 0, LOW_PAD).astype(BF16)
    a2p = _pad_rows(a2, DECAY_RANK, LOW_PAD).astype(BF16)
    g2p = _pad_rows(g2, DECAY_RANK + ICLR_RANK, LOW_PAD).astype(BF16)
    if vres is not None:
        v0, v1, v2 = vres
        rank = v1.shape[1]
        vres_args = (v_first, row(v0), jnp.pad(v1, ((0, 0), (0, VRES_PAD - rank))).astype(BF16),
                     _pad_rows(v2, 0, VRES_PAD).astype(BF16))
    else:
        vres_args = None
    r, lw, k, v, kk, b, gate, bonus = _rwkv_prep(
        z, seq_len, row(mu[:aw]), row(mu[aw:2 * aw]), row(mu[2 * aw:3 * aw]), row(mu_l),
        row(w0), w2p, row(a0), a2p, g2p, row(k_k), row(k_a), row(r_k.reshape(-1)), vres_args)
    y = _rwkv_scan(r, lw, k, v, kk, b, batch, seq_len)
    ob = _swa(z, sinks, batch, seq_len)
    wo = w_out.astype(BF16)
    x = _mix_out(x, y, bonus, gate, ob, gn_g, gn_b, wo[:aw], wo[aw:])
    return x, v


def _odd_layer(x, g, w_in, vn_g, w_s, b_s, w_out):
    z = _proj(x, g, w_in.astype(BF16), jnp.zeros((w_in.shape[1],), F32), gelu=True)
    ws = jnp.tril(w_s).astype(BF16)
    return _sgu(x, z, vn_g, ws, b_s[:, :, None], w_out.astype(BF16))


def kernel(x, norm_g, ffn_wg, ffn_wu, ffn_wd, e_w_in, e_b_qkv, e_mu, e_w0, e_w2, e_a0, e_a2, e_g2, e_k_k, e_k_a, e_r_k, e_gn_g, e_gn_b, e_sinks, e_w_out, vres_v0, vres_v1, vres_v2, o_w_in, o_vn_g, o_w_s, o_b_s, o_w_out, final_g):
    batch, seq_len, d = x.shape
    depth = norm_g.shape[0]
    x = x.reshape(batch * seq_len, d)
    v_first = None
    for layer in range(depth):
        x = _ffn(x, norm_g[layer, 0], ffn_wg[layer, 0].astype(BF16), ffn_wu[layer, 0].astype(BF16),
                 ffn_wd[layer, 0].astype(BF16))
        if layer % 2 == 0:
            i = layer // 2
            vres = None if i == 0 else (vres_v0[i - 1], vres_v1[i - 1], vres_v2[i - 1])
            x, v_a = _even_layer(x, batch, seq_len, norm_g[layer, 1], e_w_in[i], e_b_qkv[i], e_mu[i],
                                 e_w0[i], e_w2[i], e_a0[i], e_a2[i], e_g2[i], e_k_k[i], e_k_a[i],
                                 e_r_k[i], e_gn_g[i], e_gn_b[i], e_sinks[i], e_w_out[i], v_first, vres)
            if i == 0:
                v_first = v_a
        else:
            j = layer // 2
            x = _odd_layer(x, norm_g[layer, 1], o_w_in[j], o_vn_g[j], o_w_s[j], o_b_s[j], o_w_out[j])
        x = _ffn(x, norm_g[layer, 2], ffn_wg[layer, 1].astype(BF16), ffn_wu[layer, 1].astype(BF16),
                 ffn_wd[layer, 1].astype(BF16))
    return _final_norm(x, final_g).reshape(batch, seq_len, d)
```

```python
import functools
import math

import jax
import jax.numpy as jnp
from jax import lax
from jax.experimental import pallas as pl
from jax.experimental.pallas import tpu as pltpu

F32 = jnp.float32
BF16 = jnp.bfloat16

NORM_EPS = 1e-6
GN_EPS = 64e-5
HEAD_DIM = 64
A_HEADS = 16
A_WIDTH = A_HEADS * HEAD_DIM
B_HEADS = 16
B_KV_HEADS = 4
B_GROUP = B_HEADS // B_KV_HEADS
WINDOW = 128
DECAY_RANK = 64
ICLR_RANK = 64
GATE_RANK = 160
LOW_PAD = 384
VRES_PAD = 128
C_GROUPS = 16
C_CHUNK = 128
SCAN_CHUNK = 64

VMEM_LIMIT = 56 * 1024 * 1024


def _cparams(sem):
    return pltpu.CompilerParams(dimension_semantics=sem, vmem_limit_bytes=VMEM_LIMIT)


def _dot(a, b):
    return jnp.dot(a, b, preferred_element_type=F32)


def _dot_nt(a, b):
    return lax.dot_general(a, b, (((1,), (1,)), ((), ())), preferred_element_type=F32)


def _dot_tn(a, b):
    return lax.dot_general(a, b, (((0,), (0,)), ((), ())), preferred_element_type=F32)


def _split_dot(x, ones_bf16, left=False):
    x1 = x.astype(BF16)
    r1 = x - x1.astype(F32)
    x2 = r1.astype(BF16)
    x3 = (r1 - x2.astype(F32)).astype(BF16)
    if left:
        return _dot(ones_bf16, x1) + _dot(ones_bf16, x2) + _dot(ones_bf16, x3)
    return _dot(x1, ones_bf16) + _dot(x2, ones_bf16) + _dot(x3, ones_bf16)


def _rms(x, g):
    return x * lax.rsqrt(jnp.mean(x * x, axis=-1, keepdims=True) + NORM_EPS) * g


def _ffn_kernel(x_ref, g_ref, wg_ref, wu_ref, wd_ref, o_ref, h_ref):
    f = pl.program_id(1)

    @pl.when(f == 0)
    def _():
        x = x_ref[...]
        h_ref[...] = _rms(x, g_ref[...]).astype(BF16)
        o_ref[...] = x

    h = h_ref[...]
    a = _dot(h, wg_ref[...])
    b = _dot(h, wu_ref[...])
    p = (0.5 * a * jax.nn.sigmoid(a) * b).astype(BF16)
    o_ref[...] += _dot(p, wd_ref[...])


def _ffn(x, g, wg, wu, wd, tm=512, tf=512):
    n, d = x.shape
    fdim = wg.shape[1]
    return pl.pallas_call(
        _ffn_kernel,
        grid=(n // tm, fdim // tf),
        in_specs=[
            pl.BlockSpec((tm, d), lambda i, f: (i, 0)),
            pl.BlockSpec((1, d), lambda i, f: (0, 0)),
            pl.BlockSpec((d, tf), lambda i, f: (0, f)),
            pl.BlockSpec((d, tf), lambda i, f: (0, f)),
            pl.BlockSpec((tf, d), lambda i, f: (f, 0)),
        ],
        out_specs=pl.BlockSpec((tm, d), lambda i, f: (i, 0)),
        out_shape=jax.ShapeDtypeStruct((n, d), F32),
        scratch_shapes=[pltpu.VMEM((tm, d), BF16)],
        compiler_params=_cparams(("parallel", "arbitrary")),
        name="ffn",
    )(x, g.reshape(1, d), wg, wu, wd)


def _proj_kernel(x_ref, g_ref, w_ref, b_ref, o_ref, h_ref, *, gelu):
    @pl.when(pl.program_id(1) == 0)
    def _():
        h_ref[...] = _rms(x_ref[...], g_ref[...]).astype(BF16)

    z = _dot(h_ref[...], w_ref[...]) + b_ref[...]
    if gelu:
        z = 0.5 * z * (1.0 + lax.erf(z * (2.0 ** -0.5)))
    o_ref[...] = z


def _proj(x, g, w, b, gelu, tm=1024, tn=1024):
    n, d = x.shape
    cols = w.shape[1]
    return pl.pallas_call(
        functools.partial(_proj_kernel, gelu=gelu),
        grid=(n // tm, cols // tn),
        in_specs=[
            pl.BlockSpec((tm, d), lambda i, j: (i, 0)),
            pl.BlockSpec((1, d), lambda i, j: (0, 0)),
            pl.BlockSpec((d, tn), lambda i, j: (0, j)),
            pl.BlockSpec((1, tn), lambda i, j: (0, j)),
        ],
        out_specs=pl.BlockSpec((tm, tn), lambda i, j: (i, j)),
        out_shape=jax.ShapeDtypeStruct((n, cols), F32),
        scratch_shapes=[pltpu.VMEM((tm, d), BF16)],
        compiler_params=_cparams(("parallel", "arbitrary")),
        name="proj_gelu" if gelu else "proj",
    )(x, g.reshape(1, d), w, b.reshape(1, cols))


HEAD_LANES = 128


def _split2(x):
    hi = x.astype(BF16)
    return hi, (x - hi.astype(F32)).astype(BF16)


def _head_sum(x):
    ch = lax.broadcasted_iota(jnp.int32, (A_WIDTH, HEAD_LANES), 0) // HEAD_DIM
    hh = lax.broadcasted_iota(jnp.int32, (A_WIDTH, HEAD_LANES), 1)
    down = jnp.where(ch == hh, 1.0, 0.0).astype(BF16)
    hi, lo = _split2(x)
    return _dot(hi, down) + _dot(lo, down)


def _head_bcast(s):
    hh = lax.broadcasted_iota(jnp.int32, (HEAD_LANES, A_WIDTH), 0)
    ch = lax.broadcasted_iota(jnp.int32, (HEAD_LANES, A_WIDTH), 1) // HEAD_DIM
    up = jnp.where(ch == hh, 1.0, 0.0).astype(BF16)
    hi, lo = _split2(s)
    return _dot(hi, up) + _dot(lo, up)


def _prep_kernel(*refs, seq_len, tm, vres):
    (zr_ref, zk_ref, zv_ref, zl_ref, pr_ref, pk_ref, pv_ref, plo_ref,
     mur_ref, muk_ref, muv_ref, mul_ref, w0_ref, w2_ref, a0_ref, a2_ref, g2_ref,
     kk_ref, ka_ref, rk_ref) = refs[:20]
    refs = refs[20:]
    if vres:
        vf_ref, v0_ref, v1_ref, v2_ref = refs[:4]
        refs = refs[4:]
    r_out, lw_out, k_out, v_out, kk_out, b_out, g_out, bonus_out = refs

    i = pl.program_id(0)
    keep_prev = ((i * tm) % seq_len != 0).astype(F32)
    row = lax.broadcasted_iota(jnp.int32, (tm, 1), 0)

    def shifted(z_ref, p_ref, mu_ref):
        z = z_ref[...]
        prev_row = p_ref[7:8, :] * keep_prev
        prev = jnp.where(row == 0, prev_row, pltpu.roll(z, 1, axis=0))
        return z + (prev - z) * mu_ref[...]

    r = shifted(zr_ref, pr_ref, mur_ref)
    k = shifted(zk_ref, pk_ref, muk_ref)
    v = shifted(zv_ref, pv_ref, muv_ref)
    low = shifted(zl_ref, plo_ref, mul_ref)

    if vres:
        mix = _dot(_dot(v.astype(BF16), v1_ref[...]).astype(BF16), v2_ref[...])
        v = v + (vf_ref[...] - v) * jax.nn.sigmoid(v0_ref[...] + mix)

    dw = w0_ref[...] + _dot(jnp.tanh(low).astype(BF16), w2_ref[...])
    lw_out[...] = -math.exp(-0.5) * jax.nn.sigmoid(dw)
    a = jax.nn.sigmoid(a0_ref[...] + _dot(low.astype(BF16), a2_ref[...]))
    g_out[...] = _dot(jax.nn.sigmoid(low).astype(BF16), g2_ref[...])

    kk = k * kk_ref[...]
    nrm = jnp.sqrt(_head_sum(kk * kk))
    kk = kk * _head_bcast(1.0 / jnp.maximum(nrm, 1e-12))
    k = k * (1.0 + (a - 1.0) * ka_ref[...])
    r_out[...] = r
    k_out[...] = k
    v_out[...] = v
    kk_out[...] = kk
    b_out[...] = kk * a
    bonus_out[...] = _head_bcast(_head_sum(r * k * rk_ref[...])) * v


def _rwkv_prep(z, seq_len, mu_r, mu_k, mu_v, mu_l, w0, w2p, a0, a2p, g2p, k_k, k_a, r_k,
               vres, tm=256):
    n = z.shape[0]
    aw = A_WIDTH
    low_blk = (3 * aw + B_HEADS * HEAD_DIM + 2 * B_KV_HEADS * HEAD_DIM) // LOW_PAD
    pstep = tm // 8

    def prev_map(col):
        return lambda i: (jnp.maximum(i * pstep - 1, 0), col)

    row_vec = lambda w: pl.BlockSpec((1, w), lambda i: (0, 0))
    full = lambda a: pl.BlockSpec(a.shape, lambda i: (0, 0))
    in_specs = [
        pl.BlockSpec((tm, aw), lambda i: (i, 0)),
        pl.BlockSpec((tm, aw), lambda i: (i, 1)),
        pl.BlockSpec((tm, aw), lambda i: (i, 2)),
        pl.BlockSpec((tm, LOW_PAD), lambda i: (i, low_blk)),
        pl.BlockSpec((8, aw), prev_map(0)),
        pl.BlockSpec((8, aw), prev_map(1)),
        pl.BlockSpec((8, aw), prev_map(2)),
        pl.BlockSpec((8, LOW_PAD), prev_map(low_blk)),
        row_vec(aw), row_vec(aw), row_vec(aw), row_vec(LOW_PAD),
        row_vec(aw), full(w2p), row_vec(aw), full(a2p), full(g2p),
        row_vec(aw), row_vec(aw), row_vec(aw),
    ]
    args = [z, z, z, z, z, z, z, z, mu_r, mu_k, mu_v, mu_l, w0, w2p, a0, a2p, g2p, k_k, k_a, r_k]
    if vres is not None:
        v_first, v0, v1p, v2p = vres
        in_specs += [pl.BlockSpec((tm, aw), lambda i: (i, 0)), row_vec(aw), full(v1p), full(v2p)]
        args += [v_first, v0, v1p, v2p]
    out = jax.ShapeDtypeStruct((n, aw), F32)
    return pl.pallas_call(
        functools.partial(_prep_kernel, seq_len=seq_len, tm=tm, vres=vres is not None),
        grid=(n // tm,),
        in_specs=in_specs,
        out_specs=[pl.BlockSpec((tm, aw), lambda i: (i, 0))] * 8,
        out_shape=[out] * 8,
        compiler_params=_cparams(("parallel",)),
        name="rwkv_prep",
    )(*args)


def _scan_kernel(r_ref, lw_ref, k_ref, v_ref, kk_ref, b_ref, y_ref, h_ref, *, chunk):
    c = chunk
    hd = HEAD_DIM
    pw = 2 * hd
    npair = A_HEADS // 2

    @pl.when(pl.program_id(1) == 0)
    def _():
        h_ref[...] = jnp.zeros_like(h_ref)

    ri = lax.broadcasted_iota(jnp.int32, (c, c), 0)
    ci = lax.broadcasted_iota(jnp.int32, (c, c), 1)
    lw = lw_ref[...]
    cum = _split_dot(lw, jnp.where(ri >= ci, 1.0, 0.0).astype(BF16), left=True)
    p_in = jnp.exp(cum)
    p_ex = jnp.exp(cum - lw)
    p_inv = jnp.exp(-cum)
    tail = cum[c - 1:c, :]
    p_tail = jnp.exp(tail - cum)
    p_all = jnp.exp(tail)
    kk = kk_ref[...]
    k = k_ref[...]
    b = b_ref[...]
    a_t = (-kk * p_ex).astype(BF16)
    r_t = (r_ref[...] * p_in).astype(BF16)
    b_t = (b * p_inv).astype(BF16)
    k_t = (k * p_inv).astype(BF16)
    b_e = (b * p_tail).astype(BF16)
    k_e = (k * p_tail).astype(BF16)
    vb = v_ref[...].astype(BF16)

    row = lax.broadcasted_iota(jnp.int32, (c, pw), 0)
    lane = lax.broadcasted_iota(jnp.int32, (c, pw), 1)
    left = lane < hd
    col = jnp.where(left, lane, lane - hd)
    strict = row > col
    incl = row >= col
    left2 = lax.broadcasted_iota(jnp.int32, (2 * c, pw), 1) < hd
    kr = lax.broadcasted_iota(jnp.int32, (pw, pw), 0)
    kc = lax.broadcasted_iota(jnp.int32, (pw, pw), 1)
    same_head = (kr < hd) == (kc < hd)
    zeros_cv = jnp.zeros((c, pw), BF16)

    pairs = [slice(p * pw, (p + 1) * pw) for p in range(npair)]
    every = range(npair)
    ar = [jnp.concatenate([a_t[:, s], r_t[:, s]], axis=0) for s in pairs]
    bk = [jnp.concatenate([b_t[:, s], k_t[:, s]], axis=0) for s in pairs]
    sc = [_dot_nt(jnp.concatenate([jnp.where(left2, ar[p], 0), jnp.where(left2, 0, ar[p])], axis=0), bk[p])
          for p in every]
    h0 = [h_ref[p] for p in every]
    ah = [_dot(ar[p], h0[p].astype(BF16)) for p in every]
    lt = [[jnp.where(strict, sc[p][q * 2 * c:q * 2 * c + c], 0.0) for q in range(2)] for p in every]
    mb = [[jnp.where(incl, sc[p][q * 2 * c + c:(q + 1) * 2 * c], 0.0).astype(BF16) for q in range(2)]
          for p in every]
    vv = [vb[:, s] for s in pairs]
    lv = [_dot(jnp.concatenate([lt[p][0], lt[p][1]], axis=0).astype(BF16),
               jnp.concatenate([zeros_cv, vv[p]], axis=0)) for p in every]
    rhs = [ah[p][:c] + jnp.where(left, lv[p][:c], lv[p][c:]) for p in every]
    x = [[jnp.where(left, lt[p][0], pltpu.roll(rhs[p], hd, axis=1)),
          jnp.where(left, lt[p][1], rhs[p])] for p in every]
    for _ in range(int(math.log2(c))):
        z = [[_dot(x[p][q][:, :hd].astype(BF16), x[p][q].astype(BF16)) for q in range(2)] for p in every]
        x = [[jnp.where(left, z[p][q], x[p][q] + z[p][q]) for q in range(2)] for p in every]
    uv = [jnp.concatenate([jnp.where(left, pltpu.roll(x[p][0], hd, axis=1), x[p][1]).astype(BF16), vv[p]],
                          axis=0) for p in every]
    yy = [_dot(jnp.concatenate([mb[p][0], mb[p][1]], axis=0), uv[p]) for p in every]
    for p in every:
        y_ref[:, pairs[p]] = ah[p][c:] + jnp.where(left, yy[p][:c], yy[p][c:])
    hn = [_dot_tn(jnp.concatenate([b_e[:, s], k_e[:, s]], axis=0), uv[p]) for p, s in enumerate(pairs)]
    for p in every:
        decay = jnp.transpose(jnp.broadcast_to(p_all[:, pairs[p]], (pw, pw)))
        h_ref[p] = decay * h0[p] + jnp.where(same_head, hn[p], 0.0)


def _rwkv_scan(r, lw, k, v, kk, b, batch, seq_len, chunk=SCAN_CHUNK):
    n, aw = r.shape
    nc = seq_len // chunk
    spec = pl.BlockSpec((chunk, aw), lambda bi, ci: (bi * nc + ci, 0))
    return pl.pallas_call(
        functools.partial(_scan_kernel, chunk=chunk),
        grid=(batch, nc),
        in_specs=[spec] * 6,
        out_specs=spec,
        out_shape=jax.ShapeDtypeStruct((n, aw), F32),
        scratch_shapes=[pltpu.VMEM((A_HEADS // 2, 2 * HEAD_DIM, 2 * HEAD_DIM), F32)],
        compiler_params=_cparams(("parallel", "arbitrary")),
        name="rwkv_scan",
    )(r, lw, k, v, kk, b)


def _swa_kernel(sink_ref, q_ref, kc_ref, kp_ref, vc_ref, vp_ref, o_ref):
    blk = WINDOW
    first_key = jnp.where(pl.program_id(1) > 0, 0, blk)
    qi = lax.broadcasted_iota(jnp.int32, (blk, 2 * blk), 0)
    kj = lax.broadcasted_iota(jnp.int32, (blk, 2 * blk), 1)
    dist = qi + blk - kj
    valid = (dist >= 0) & (dist < WINDOW) & (kj >= first_key)
    distf = dist.astype(F32)
    scale = HEAD_DIM ** -0.5
    kband = jnp.concatenate([kp_ref[...], kc_ref[...]], axis=0).astype(BF16)
    vband = jnp.concatenate([vp_ref[...], vc_ref[...]], axis=0).astype(BF16)
    q = q_ref[...].astype(BF16)
    for h in range(B_HEADS):
        kv = h // B_GROUP
        slope = 2.0 ** (-8.0 * (h + 1) / B_HEADS)
        qh = q[:, h * HEAD_DIM:(h + 1) * HEAD_DIM]
        kh = kband[:, kv * HEAD_DIM:(kv + 1) * HEAD_DIM]
        vh = vband[:, kv * HEAD_DIM:(kv + 1) * HEAD_DIM]
        s = _dot_nt(qh, kh) * scale - slope * distf
        s = jnp.where(valid, s, -1e30)
        sink = sink_ref[h]
        m = jnp.maximum(jnp.max(s, axis=-1, keepdims=True), sink)
        p = jnp.exp(s - m)
        den = jnp.sum(p, axis=-1, keepdims=True) + jnp.exp(sink - m)
        o = _dot(p.astype(BF16), vh) / den
        o_ref[:, h * HEAD_DIM:(h + 1) * HEAD_DIM] = o


def _swa(z, sinks, batch, seq_len):
    n = z.shape[0]
    nb = seq_len // WINDOW
    qw = B_HEADS * HEAD_DIM
    kvw = B_KV_HEADS * HEAD_DIM
    q_blk = 3 * A_WIDTH // qw
    k_blk = (3 * A_WIDTH + qw) // kvw
    cur = lambda col: (lambda bi, j: (bi * nb + j, col))
    prev = lambda col: (lambda bi, j: (bi * nb + jnp.maximum(j - 1, 0), col))
    return pl.pallas_call(
        _swa_kernel,
        grid=(batch, nb),
        in_specs=[
            pl.BlockSpec(memory_space=pltpu.SMEM),
            pl.BlockSpec((WINDOW, qw), cur(q_blk)),
            pl.BlockSpec((WINDOW, kvw), cur(k_blk)),
            pl.BlockSpec((WINDOW, kvw), prev(k_blk)),
            pl.BlockSpec((WINDOW, kvw), cur(k_blk + 1)),
            pl.BlockSpec((WINDOW, kvw), prev(k_blk + 1)),
        ],
        out_specs=pl.BlockSpec((WINDOW, qw), lambda bi, j: (bi * nb + j, 0)),
        out_shape=jax.ShapeDtypeStruct((n, qw), F32),
        compiler_params=_cparams(("parallel", "arbitrary")),
        name="swa",
    )(sinks, z, z, z, z, z)


def _mix_out_kernel(x_ref, y_ref, bonus_ref, g_ref, ob_ref, gng_ref, gnb_ref, wa_ref, wb_ref, o_ref):
    y = y_ref[...]
    yc = y - _head_bcast(_head_sum(y) * (1.0 / HEAD_DIM))
    var = _head_sum(yc * yc) * (1.0 / HEAD_DIM)
    yn = yc * _head_bcast(lax.rsqrt(var + GN_EPS)) * gng_ref[...] + gnb_ref[...]
    oa = (yn + bonus_ref[...]) * g_ref[...]
    o_ref[...] = (x_ref[...] + _dot(oa.astype(BF16), wa_ref[...])
                  + _dot(ob_ref[...].astype(BF16), wb_ref[...]))


def _mix_out(x, y, bonus, g, ob, gn_g, gn_b, wa, wb, tm=256):
    n, d = x.shape
    aw = A_WIDTH
    bw = ob.shape[1]
    rows = lambda w: pl.BlockSpec((tm, w), lambda i: (i, 0))
    row_vec = lambda w: pl.BlockSpec((1, w), lambda i: (0, 0))
    return pl.pallas_call(
        _mix_out_kernel,
        grid=(n // tm,),
        in_specs=[rows(d), rows(aw), rows(aw), rows(aw), rows(bw), row_vec(aw), row_vec(aw),
                  pl.BlockSpec((aw, d), lambda i: (0, 0)), pl.BlockSpec((bw, d), lambda i: (0, 0))],
        out_specs=rows(d),
        out_shape=jax.ShapeDtypeStruct((n, d), F32),
        compiler_params=_cparams(("parallel",)),
        name="mix_out",
    )(x, y, bonus, g, ob, gn_g.reshape(1, aw), gn_b.reshape(1, aw), wa, wb)


def _sgu_kernel(x_ref, u_ref, v_ref, vg_ref, ws_ref, bs_ref, wo_ref, o_ref, gate_ref, *, tm):
    cw = C_CHUNK
    vn = _rms(v_ref[...], vg_ref[...]).astype(BF16)
    for ch in range(tm // C_CHUNK):
        rows = slice(ch * C_CHUNK, (ch + 1) * C_CHUNK)
        for grp in range(C_GROUPS):
            cols = slice(grp * cw, (grp + 1) * cw)
            vm = _dot(ws_ref[grp], vn[rows, cols]) + bs_ref[grp]
            gate_ref[rows, cols] = (u_ref[rows, cols] * vm).astype(BF16)
    o_ref[...] = x_ref[...] + _dot(gate_ref[...], wo_ref[...])


def _sgu(x, z, vn_g, ws, bs, wo, tm=256):
    n, d = x.shape
    cwid = wo.shape[0]
    return pl.pallas_call(
        functools.partial(_sgu_kernel, tm=tm),
        grid=(n // tm,),
        in_specs=[
            pl.BlockSpec((tm, d), lambda i: (i, 0)),
            pl.BlockSpec((tm, cwid), lambda i: (i, 0)),
            pl.BlockSpec((tm, cwid), lambda i: (i, 1)),
            pl.BlockSpec((1, cwid), lambda i: (0, 0)),
            pl.BlockSpec(ws.shape, lambda i: (0, 0, 0)),
            pl.BlockSpec(bs.shape, lambda i: (0, 0, 0)),
            pl.BlockSpec((cwid, d), lambda i: (0, 0)),
        ],
        out_specs=pl.BlockSpec((tm, d), lambda i: (i, 0)),
        out_shape=jax.ShapeDtypeStruct((n, d), F32),
        scratch_shapes=[pltpu.VMEM((tm, cwid), BF16)],
        compiler_params=_cparams(("parallel",)),
        name="sgu",
    )(x, z, z, vn_g.reshape(1, cwid), ws, bs, wo)


def _final_norm_kernel(x_ref, g_ref, o_ref):
    o_ref[...] = _rms(x_ref[...], g_ref[...])


def _final_norm(x, g, tm=512):
    n, d = x.shape
    return pl.pallas_call(
        _final_norm_kernel,
        grid=(n // tm,),
        in_specs=[pl.BlockSpec((tm, d), lambda i: (i, 0)), pl.BlockSpec((1, d), lambda i: (0, 0))],
        out_specs=pl.BlockSpec((tm, d), lambda i: (i, 0)),
        out_shape=jax.ShapeDtypeStruct((n, d), F32),
        compiler_params=_cparams(("parallel",)),
        name="final_norm",
    )(x, g.reshape(1, d))


def _pad_rows(w, lo, total):
    return jnp.pad(w, ((lo, total - lo - w.shape[0]), (0, 0)))


def _even_layer(x, batch, seq_len, g, w_in, b_qkv, mu, w0, w2, a0, a2, g2, k_k, k_a, r_k,
                gn_g, gn_b, sinks, w_out, v_first, vres):
    aw = A_WIDTH
    a_cols = 3 * aw + DECAY_RANK + ICLR_RANK + GATE_RANK
    n_low = a_cols - 3 * aw
    b_cols = w_in.shape[1] - a_cols
    pad_low = LOW_PAD - n_low
    tail = (-(3 * aw + b_cols + LOW_PAD)) % 1024
    w_cat = jnp.concatenate([w_in[:, :3 * aw], w_in[:, a_cols:], w_in[:, 3 * aw:a_cols],
                             jnp.zeros((w_in.shape[0], pad_low + tail), F32)], axis=1).astype(BF16)
    b_cat = jnp.concatenate([jnp.zeros((3 * aw,), F32), b_qkv, jnp.zeros((LOW_PAD + tail,), F32)])
    z = _proj(x, g, w_cat, b_cat, gelu=False)

    mu_l = jnp.pad(mu[3 * aw:], (0, pad_low))
    row = lambda t: t.reshape(1, -1)
    w2p = _pad_rows(w2, 0, LOW_PAD).astype(BF16)
    a2p = _pad_rows(a2, DECAY_RANK, LOW_PAD).astype(BF16)
    g2p = _pad_rows(g2, DECAY_RANK + ICLR_RANK, LOW_PAD).astype(BF16)
    if vres is not None:
        v0, v1, v2 = vres
        rank = v1.shape[1]
        vres_args = (v_first, row(v0), jnp.pad(v1, ((0, 0), (0, VRES_PAD - rank))).astype(BF16),
                     _pad_rows(v2, 0, VRES_PAD).astype(BF16))
    else:
        vres_args = None
    r, lw, k, v, kk, b, gate, bonus = _rwkv_prep(
        z, seq_len, row(mu[:aw]), row(mu[aw:2 * aw]), row(mu[2 * aw:3 * aw]), row(mu_l),
        row(w0), w2p, row(a0), a2p, g2p, row(k_k), row(k_a), row(r_k.reshape(-1)), vres_args)
    y = _rwkv_scan(r, lw, k, v, kk, b, batch, seq_len)
    ob = _swa(z, sinks, batch, seq_len)
    wo = w_out.astype(BF16)
    x = _mix_out(x, y, bonus, gate, ob, gn_g, gn_b, wo[:aw], wo[aw:])
    return x, v


def _odd_layer(x, g, w_in, vn_g, w_s, b_s, w_out):
    z = _proj(x, g, w_in.astype(BF16), jnp.zeros((w_in.shape[1],), F32), gelu=True)
    ws = jnp.tril(w_s).astype(BF16)
    return _sgu(x, z, vn_g, ws, b_s[:, :, None], w_out.astype(BF16))


def kernel(x, norm_g, ffn_wg, ffn_wu, ffn_wd, e_w_in, e_b_qkv, e_mu, e_w0, e_w2, e_a0, e_a2, e_g2, e_k_k, e_k_a, e_r_k, e_gn_g, e_gn_b, e_sinks, e_w_out, vres_v0, vres_v1, vres_v2, o_w_in, o_vn_g, o_w_s, o_b_s, o_w_out, final_g):
    batch, seq_len, d = x.shape
    depth = norm_g.shape[0]
    x = x.reshape(batch * seq_len, d)
    v_first = None
    for layer in range(depth):
        x = _ffn(x, norm_g[layer, 0], ffn_wg[layer, 0].astype(BF16), ffn_wu[layer, 0].astype(BF16),
                 ffn_wd[layer, 0].astype(BF16))
        if layer % 2 == 0:
            i = layer // 2
            vres = None if i == 0 else (vres_v0[i - 1], vres_v1[i - 1], vres_v2[i - 1])
            x, v_a = _even_layer(x, batch, seq_len, norm_g[layer, 1], e_w_in[i], e_b_qkv[i], e_mu[i],
                                 e_w0[i], e_w2[i], e_a0[i], e_a2[i], e_g2[i], e_k_k[i], e_k_a[i],
                                 e_r_k[i], e_gn_g[i], e_gn_b[i], e_sinks[i], e_w_out[i], v_first, vres)
            if i == 0:
                v_first = v_a
        else:
            j = layer // 2
            x = _odd_layer(x, norm_g[layer, 1], o_w_in[j], o_vn_g[j], o_w_s[j], o_b_s[j], o_w_out[j])
        x = _ffn(x, norm_g[layer, 2], ffn_wg[layer, 1].astype(BF16), ffn_wu[layer, 1].astype(BF16),
                 ffn_wd[layer, 1].astype(BF16))
    return _final_norm(x, final_g).reshape(batch, seq_len, d)
```

```python
import functools
import math

import jax
import jax.numpy as jnp
from jax import lax
from jax.experimental import pallas as pl
from jax.experimental.pallas import tpu as pltpu

F32 = jnp.float32
BF16 = jnp.bfloat16

NORM_EPS = 1e-6
GN_EPS = 64e-5
HEAD_DIM = 64
A_HEADS = 16
A_WIDTH = A_HEADS * HEAD_DIM
B_HEADS = 16
B_KV_HEADS = 4
B_GROUP = B_HEADS // B_KV_HEADS
WINDOW = 128
DECAY_RANK = 64
ICLR_RANK = 64
GATE_RANK = 160
LOW_PAD = 384
VRES_PAD = 128
C_GROUPS = 16
C_CHUNK = 128
SCAN_CHUNK = 64

VMEM_LIMIT = 56 * 1024 * 1024


def _cparams(sem):
    return pltpu.CompilerParams(dimension_semantics=sem, vmem_limit_bytes=VMEM_LIMIT)


def _dot(a, b):
    return jnp.dot(a, b, preferred_element_type=F32)


def _dot_nt(a, b):
    return lax.dot_general(a, b, (((1,), (1,)), ((), ())), preferred_element_type=F32)


def _dot_tn(a, b):
    return lax.dot_general(a, b, (((0,), (0,)), ((), ())), preferred_element_type=F32)


def _split_dot(x, ones_bf16, left=False):
    x1 = x.astype(BF16)
    r1 = x - x1.astype(F32)
    x2 = r1.astype(BF16)
    x3 = (r1 - x2.astype(F32)).astype(BF16)
    if left:
        return _dot(ones_bf16, x1) + _dot(ones_bf16, x2) + _dot(ones_bf16, x3)
    return _dot(x1, ones_bf16) + _dot(x2, ones_bf16) + _dot(x3, ones_bf16)


def _rms(x, g):
    return x * lax.rsqrt(jnp.mean(x * x, axis=-1, keepdims=True) + NORM_EPS) * g


def _ffn_kernel(x_ref, g_ref, gout_ref, wg_ref, wu_ref, wd_ref, o_ref, h_ref):
    f = pl.program_id(1)

    @pl.when(f == 0)
    def _():
        x = x_ref[...]
        h_ref[...] = _rms(x, g_ref[...]).astype(BF16)
        o_ref[...] = x

    h = h_ref[...]
    a = _dot(h, wg_ref[...])
    b = _dot(h, wu_ref[...])
    p = (0.5 * a * jax.nn.sigmoid(a) * b).astype(BF16)
    o_ref[...] += _dot(p, wd_ref[...])

    if gout_ref is not None:
        @pl.when(f == pl.num_programs(1) - 1)
        def _():
            o_ref[...] = _rms(o_ref[...], gout_ref[...])


def _ffn_kernel_plain(x_ref, g_ref, wg_ref, wu_ref, wd_ref, o_ref, h_ref):
    _ffn_kernel(x_ref, g_ref, None, wg_ref, wu_ref, wd_ref, o_ref, h_ref)


def _ffn(x, g, wg, wu, wd, idx, g_out=None, tm=512, tf=512):
    n, d = x.shape
    fdim = wg.shape[2]
    vec = pl.BlockSpec((1, d), lambda i, f: (0, 0))
    in_specs = [pl.BlockSpec((tm, d), lambda i, f: (i, 0)), vec]
    args = [x, g.reshape(1, d)]
    if g_out is not None:
        in_specs.append(vec)
        args.append(g_out.reshape(1, d))
    in_specs += [
        pl.BlockSpec((None, d, tf), lambda i, f: (idx, 0, f)),
        pl.BlockSpec((None, d, tf), lambda i, f: (idx, 0, f)),
        pl.BlockSpec((None, tf, d), lambda i, f: (idx, f, 0)),
    ]
    return pl.pallas_call(
        _ffn_kernel if g_out is not None else _ffn_kernel_plain,
        grid=(n // tm, fdim // tf),
        in_specs=in_specs,
        out_specs=pl.BlockSpec((tm, d), lambda i, f: (i, 0)),
        out_shape=jax.ShapeDtypeStruct((n, d), F32),
        scratch_shapes=[pltpu.VMEM((tm, d), BF16)],
        compiler_params=_cparams(("parallel", "arbitrary")),
        name="ffn",
    )(*args, wg, wu, wd)


def _proj_kernel(x_ref, g_ref, w_ref, b_ref, o_ref, h_ref, *, gelu):
    @pl.when(pl.program_id(1) == 0)
    def _():
        h_ref[...] = _rms(x_ref[...], g_ref[...]).astype(BF16)

    z = _dot(h_ref[...], w_ref[...]) + b_ref[...]
    if gelu:
        z = 0.5 * z * (1.0 + lax.erf(z * (2.0 ** -0.5)))
    o_ref[...] = z


def _proj(x, g, w, b, gelu, tm=1024, tn=1024):
    n, d = x.shape
    cols = w.shape[1]
    return pl.pallas_call(
        functools.partial(_proj_kernel, gelu=gelu),
        grid=(n // tm, cols // tn),
        in_specs=[
            pl.BlockSpec((tm, d), lambda i, j: (i, 0)),
            pl.BlockSpec((1, d), lambda i, j: (0, 0)),
            pl.BlockSpec((d, tn), lambda i, j: (0, j)),
            pl.BlockSpec((1, tn), lambda i, j: (0, j)),
        ],
        out_specs=pl.BlockSpec((tm, tn), lambda i, j: (i, j)),
        out_shape=jax.ShapeDtypeStruct((n, cols), F32),
        scratch_shapes=[pltpu.VMEM((tm, d), BF16)],
        compiler_params=_cparams(("parallel", "arbitrary")),
        name="proj_gelu" if gelu else "proj",
    )(x, g.reshape(1, d), w, b.reshape(1, cols))


HEAD_LANES = 128


def _split2(x):
    hi = x.astype(BF16)
    return hi, (x - hi.astype(F32)).astype(BF16)


def _head_sum(x):
    ch = lax.broadcasted_iota(jnp.int32, (A_WIDTH, HEAD_LANES), 0) // HEAD_DIM
    hh = lax.broadcasted_iota(jnp.int32, (A_WIDTH, HEAD_LANES), 1)
    down = jnp.where(ch == hh, 1.0, 0.0).astype(BF16)
    hi, lo = _split2(x)
    return _dot(hi, down) + _dot(lo, down)


def _head_bcast(s):
    hh = lax.broadcasted_iota(jnp.int32, (HEAD_LANES, A_WIDTH), 0)
    ch = lax.broadcasted_iota(jnp.int32, (HEAD_LANES, A_WIDTH), 1) // HEAD_DIM
    up = jnp.where(ch == hh, 1.0, 0.0).astype(BF16)
    hi, lo = _split2(s)
    return _dot(hi, up) + _dot(lo, up)


def _rwkv_kernel(*refs, rows, chunk, vres, emit_v):
    (zr_ref, zk_ref, zv_ref, zl_ref, pr_ref, pk_ref, pv_ref, plo_ref,
     mur_ref, muk_ref, muv_ref, mul_ref, w0_ref, w2_ref, a0_ref, a2_ref, g2_ref,
     kk_ref, ka_ref, rk_ref, gng_ref, gnb_ref) = refs[:22]
    refs = refs[22:]
    if vres:
        vf_ref, v0_ref, v1_ref, v2_ref = refs[:4]
        refs = refs[4:]
    oa_out = refs[0]
    refs = refs[1:]
    if emit_v:
        v_out = refs[0]
        refs = refs[1:]
    h_ref, r_sc, lw_sc, k_sc, v_sc, kk_sc, b_sc, g_sc, bonus_sc, y_sc = refs

    first = pl.program_id(1) == 0

    @pl.when(first)
    def _():
        h_ref[...] = jnp.zeros_like(h_ref)

    keep_prev = jnp.where(first, 0.0, 1.0)
    row = lax.broadcasted_iota(jnp.int32, (rows, 1), 0)

    def shifted(z_ref, p_ref, mu_ref):
        z = z_ref[...]
        prev_row = p_ref[7:8, :] * keep_prev
        prev = jnp.where(row == 0, prev_row, pltpu.roll(z, 1, axis=0))
        return z + (prev - z) * mu_ref[...]

    r = shifted(zr_ref, pr_ref, mur_ref)
    k = shifted(zk_ref, pk_ref, muk_ref)
    v = shifted(zv_ref, pv_ref, muv_ref)
    low = shifted(zl_ref, plo_ref, mul_ref)

    if vres:
        mix = _dot(_dot(v.astype(BF16), v1_ref[...]).astype(BF16), v2_ref[...])
        v = v + (vf_ref[...] - v) * jax.nn.sigmoid(v0_ref[...] + mix)

    dw = w0_ref[...] + _dot(jnp.tanh(low).astype(BF16), w2_ref[...])
    lw_sc[...] = -math.exp(-0.5) * jax.nn.sigmoid(dw)
    a = jax.nn.sigmoid(a0_ref[...] + _dot(low.astype(BF16), a2_ref[...]))
    g_sc[...] = _dot(jax.nn.sigmoid(low).astype(BF16), g2_ref[...])

    kk = k * kk_ref[...]
    nrm = jnp.sqrt(_head_sum(kk * kk))
    kk = kk * _head_bcast(1.0 / jnp.maximum(nrm, 1e-12))
    k = k * (1.0 + (a - 1.0) * ka_ref[...])
    r_sc[...] = r
    k_sc[...] = k
    v_sc[...] = v
    if emit_v:
        v_out[...] = v
    kk_sc[...] = kk
    b_sc[...] = kk * a
    bonus_sc[...] = _head_bcast(_head_sum(r * k * rk_ref[...])) * v

    for ci in range(rows // chunk):
        rs = pl.ds(ci * chunk, chunk)
        _scan_chunk(r_sc[rs, :], lw_sc[rs, :], k_sc[rs, :], v_sc[rs, :], kk_sc[rs, :], b_sc[rs, :],
                    h_ref, y_sc, rs)

    y = y_sc[...]
    yc = y - _head_bcast(_head_sum(y) * (1.0 / HEAD_DIM))
    var = _head_sum(yc * yc) * (1.0 / HEAD_DIM)
    yn = yc * _head_bcast(lax.rsqrt(var + GN_EPS)) * gng_ref[...] + gnb_ref[...]
    oa_out[...] = ((yn + bonus_sc[...]) * g_sc[...]).astype(oa_out.dtype)


def _rwkv(z, batch, seq_len, mu_r, mu_k, mu_v, mu_l, w0, w2p, a0, a2p, g2p, k_k, k_a, r_k,
          gn_g, gn_b, vres, emit_v, rows=256, chunk=SCAN_CHUNK):
    n = z.shape[0]
    aw = A_WIDTH
    low_blk = (3 * aw + B_HEADS * HEAD_DIM + 2 * B_KV_HEADS * HEAD_DIM) // LOW_PAD
    nr = seq_len // rows
    pstep = rows // 8

    def cur(col):
        return lambda bi, j: (bi * nr + j, col)

    def prev_map(col):
        return lambda bi, j: (jnp.maximum((bi * nr + j) * pstep - 1, 0), col)

    row_vec = lambda w: pl.BlockSpec((1, w), lambda bi, j: (0, 0))
    full = lambda a: pl.BlockSpec(a.shape, lambda bi, j: (0, 0))
    in_specs = [
        pl.BlockSpec((rows, aw), cur(0)),
        pl.BlockSpec((rows, aw), cur(1)),
        pl.BlockSpec((rows, aw), cur(2)),
        pl.BlockSpec((rows, LOW_PAD), cur(low_blk)),
        pl.BlockSpec((8, aw), prev_map(0)),
        pl.BlockSpec((8, aw), prev_map(1)),
        pl.BlockSpec((8, aw), prev_map(2)),
        pl.BlockSpec((8, LOW_PAD), prev_map(low_blk)),
        row_vec(aw), row_vec(aw), row_vec(aw), row_vec(LOW_PAD),
        row_vec(aw), full(w2p), row_vec(aw), full(a2p), full(g2p),
        row_vec(aw), row_vec(aw), row_vec(aw), row_vec(aw), row_vec(aw),
    ]
    args = [z, z, z, z, z, z, z, z, mu_r, mu_k, mu_v, mu_l, w0, w2p, a0, a2p, g2p, k_k, k_a, r_k,
            gn_g, gn_b]
    if vres is not None:
        v_first, v0, v1p, v2p = vres
        in_specs += [pl.BlockSpec((rows, aw), cur(0)), row_vec(aw), full(v1p), full(v2p)]
        args += [v_first, v0, v1p, v2p]
    out_specs = [pl.BlockSpec((rows, aw), cur(0))]
    out_shape = [jax.ShapeDtypeStruct((n, aw), BF16)]
    if emit_v:
        out_specs.append(pl.BlockSpec((rows, aw), cur(0)))
        out_shape.append(jax.ShapeDtypeStruct((n, aw), F32))
    state = pltpu.VMEM((A_HEADS // 2, 2 * HEAD_DIM, 2 * HEAD_DIM), F32)
    return pl.pallas_call(
        functools.partial(_rwkv_kernel, rows=rows, chunk=chunk, vres=vres is not None, emit_v=emit_v),
        grid=(batch, nr),
        in_specs=in_specs,
        out_specs=out_specs,
        out_shape=out_shape,
        scratch_shapes=[state] + [pltpu.VMEM((rows, aw), F32)] * 9,
        compiler_params=_cparams(("parallel", "arbitrary")),
        name="rwkv",
    )(*args)


def _scan_chunk(r, lw, k, v, kk, b, h_ref, y_ref, rows):
    c = r.shape[0]
    hd = HEAD_DIM
    pw = 2 * hd
    npair = A_HEADS // 2

    ri = lax.broadcasted_iota(jnp.int32, (c, c), 0)
    ci = lax.broadcasted_iota(jnp.int32, (c, c), 1)
    cum = _split_dot(lw, jnp.where(ri >= ci, 1.0, 0.0).astype(BF16), left=True)
    p_in = jnp.exp(cum)
    p_ex = jnp.exp(cum - lw)
    p_inv = jnp.exp(-cum)
    tail = cum[c - 1:c, :]
    p_tail = jnp.exp(tail - cum)
    p_all = jnp.exp(tail)
    a_t = (-kk * p_ex).astype(BF16)
    r_t = (r * p_in).astype(BF16)
    b_t = (b * p_inv).astype(BF16)
    k_t = (k * p_inv).astype(BF16)
    b_e = (b * p_tail).astype(BF16)
    k_e = (k * p_tail).astype(BF16)
    vb = v.astype(BF16)

    row = lax.broadcasted_iota(jnp.int32, (c, pw), 0)
    lane = lax.broadcasted_iota(jnp.int32, (c, pw), 1)
    left = lane < hd
    col = jnp.where(left, lane, lane - hd)
    strict = row > col
    incl = row >= col
    left2 = lax.broadcasted_iota(jnp.int32, (2 * c, pw), 1) < hd
    kr = lax.broadcasted_iota(jnp.int32, (pw, pw), 0)
    kc = lax.broadcasted_iota(jnp.int32, (pw, pw), 1)
    same_head = (kr < hd) == (kc < hd)
    zeros_cv = jnp.zeros((c, pw), BF16)

    pairs = [slice(p * pw, (p + 1) * pw) for p in range(npair)]
    every = range(npair)
    ar = [jnp.concatenate([a_t[:, s], r_t[:, s]], axis=0) for s in pairs]
    bk = [jnp.concatenate([b_t[:, s], k_t[:, s]], axis=0) for s in pairs]
    sc = [_dot_nt(jnp.concatenate([jnp.where(left2, ar[p], 0), jnp.where(left2, 0, ar[p])], axis=0), bk[p])
          for p in every]
    h0 = [h_ref[p] for p in every]
    ah = [_dot(ar[p], h0[p].astype(BF16)) for p in every]
    lt = [[jnp.where(strict, sc[p][q * 2 * c:q * 2 * c + c], 0.0) for q in range(2)] for p in every]
    mb = [[jnp.where(incl, sc[p][q * 2 * c + c:(q + 1) * 2 * c], 0.0).astype(BF16) for q in range(2)]
          for p in every]
    vv = [vb[:, s] for s in pairs]
    lv = [_dot(jnp.concatenate([lt[p][0], lt[p][1]], axis=0).astype(BF16),
               jnp.concatenate([zeros_cv, vv[p]], axis=0)) for p in every]
    rhs = [ah[p][:c] + jnp.where(left, lv[p][:c], lv[p][c:]) for p in every]
    x = [[jnp.where(left, lt[p][0], pltpu.roll(rhs[p], hd, axis=1)),
          jnp.where(left, lt[p][1], rhs[p])] for p in every]
    for _ in range(int(math.log2(c))):
        z = [[_dot(x[p][q][:, :hd].astype(BF16), x[p][q].astype(BF16)) for q in range(2)] for p in every]
        x = [[jnp.where(left, z[p][q], x[p][q] + z[p][q]) for q in range(2)] for p in every]
    uv = [jnp.concatenate([jnp.where(left, pltpu.roll(x[p][0], hd, axis=1), x[p][1]).astype(BF16), vv[p]],
                          axis=0) for p in every]
    yy = [_dot(jnp.concatenate([mb[p][0], mb[p][1]], axis=0), uv[p]) for p in every]
    for p in every:
        y_ref[rows, pairs[p]] = ah[p][c:] + jnp.where(left, yy[p][:c], yy[p][c:])
    hn = [_dot_tn(jnp.concatenate([b_e[:, s], k_e[:, s]], axis=0), uv[p]) for p, s in enumerate(pairs)]
    for p in every:
        decay = jnp.transpose(jnp.broadcast_to(p_all[:, pairs[p]], (pw, pw)))
        h_ref[p] = decay * h0[p] + jnp.where(same_head, hn[p], 0.0)


def _swa_kernel(sink_ref, q_ref, kc_ref, kp_ref, vc_ref, vp_ref, o_ref):
    blk = WINDOW
    first_key = jnp.where(pl.program_id(1) > 0, 0, blk)
    qi = lax.broadcasted_iota(jnp.int32, (blk, 2 * blk), 0)
    kj = lax.broadcasted_iota(jnp.int32, (blk, 2 * blk), 1)
    dist = qi + blk - kj
    valid = (dist >= 0) & (dist < WINDOW) & (kj >= first_key)
    distf = dist.astype(F32)
    scale = HEAD_DIM ** -0.5
    kband = jnp.concatenate([kp_ref[...], kc_ref[...]], axis=0).astype(BF16)
    vband = jnp.concatenate([vp_ref[...], vc_ref[...]], axis=0).astype(BF16)
    q = q_ref[...].astype(BF16)
    for h in range(B_HEADS):
        kv = h // B_GROUP
        slope = 2.0 ** (-8.0 * (h + 1) / B_HEADS)
        qh = q[:, h * HEAD_DIM:(h + 1) * HEAD_DIM]
        kh = kband[:, kv * HEAD_DIM:(kv + 1) * HEAD_DIM]
        vh = vband[:, kv * HEAD_DIM:(kv + 1) * HEAD_DIM]
        s = _dot_nt(qh, kh) * scale - slope * distf
        s = jnp.where(valid, s, -1e30)
        sink = sink_ref[h]
        m = jnp.maximum(jnp.max(s, axis=-1, keepdims=True), sink)
        p = jnp.exp(s - m)
        den = jnp.sum(p, axis=-1, keepdims=True) + jnp.exp(sink - m)
        o = _dot(p.astype(BF16), vh) / den
        o_ref[:, h * HEAD_DIM:(h + 1) * HEAD_DIM] = o.astype(o_ref.dtype)


def _swa(z, sinks, batch, seq_len):
    n = z.shape[0]
    nb = seq_len // WINDOW
    qw = B_HEADS * HEAD_DIM
    kvw = B_KV_HEADS * HEAD_DIM
    q_blk = 3 * A_WIDTH // qw
    k_blk = (3 * A_WIDTH + qw) // kvw
    cur = lambda col: (lambda bi, j: (bi * nb + j, col))
    prev = lambda col: (lambda bi, j: (bi * nb + jnp.maximum(j - 1, 0), col))
    return pl.pallas_call(
        _swa_kernel,
        grid=(batch, nb),
        in_specs=[
            pl.BlockSpec(memory_space=pltpu.SMEM),
            pl.BlockSpec((WINDOW, qw), cur(q_blk)),
            pl.BlockSpec((WINDOW, kvw), cur(k_blk)),
            pl.BlockSpec((WINDOW, kvw), prev(k_blk)),
            pl.BlockSpec((WINDOW, kvw), cur(k_blk + 1)),
            pl.BlockSpec((WINDOW, kvw), prev(k_blk + 1)),
        ],
        out_specs=pl.BlockSpec((WINDOW, qw), lambda bi, j: (bi * nb + j, 0)),
        out_shape=jax.ShapeDtypeStruct((n, qw), BF16),
        compiler_params=_cparams(("parallel", "arbitrary")),
        name="swa",
    )(sinks, z, z, z, z, z)


def _mix_out_kernel(x_ref, oa_ref, ob_ref, wa_ref, wb_ref, o_ref):
    o_ref[...] = x_ref[...] + _dot(oa_ref[...], wa_ref[...]) + _dot(ob_ref[...], wb_ref[...])


def _mix_out(x, oa, ob, wa, wb, tm=512):
    n, d = x.shape
    aw = oa.shape[1]
    bw = ob.shape[1]
    rows = lambda w: pl.BlockSpec((tm, w), lambda i: (i, 0))
    return pl.pallas_call(
        _mix_out_kernel,
        grid=(n // tm,),
        in_specs=[rows(d), rows(aw), rows(bw),
                  pl.BlockSpec((aw, d), lambda i: (0, 0)), pl.BlockSpec((bw, d), lambda i: (0, 0))],
        out_specs=rows(d),
        out_shape=jax.ShapeDtypeStruct((n, d), F32),
        compiler_params=_cparams(("parallel",)),
        name="mix_out",
    )(x, oa, ob, wa, wb)


def _sgu_kernel(x_ref, u_ref, v_ref, vg_ref, ws_ref, bs_ref, wo_ref, o_ref, gate_ref, *, tm):
    cw = C_CHUNK
    vn = _rms(v_ref[...], vg_ref[...]).astype(BF16)
    for ch in range(tm // C_CHUNK):
        rows = slice(ch * C_CHUNK, (ch + 1) * C_CHUNK)
        for grp in range(C_GROUPS):
            cols = slice(grp * cw, (grp + 1) * cw)
            vm = _dot(ws_ref[grp], vn[rows, cols]) + bs_ref[grp]
            gate_ref[rows, cols] = (u_ref[rows, cols] * vm).astype(BF16)
    o_ref[...] = x_ref[...] + _dot(gate_ref[...], wo_ref[...])


def _sgu(x, z, vn_g, ws, bs, wo, tm=256):
    n, d = x.shape
    cwid = wo.shape[0]
    return pl.pallas_call(
        functools.partial(_sgu_kernel, tm=tm),
        grid=(n // tm,),
        in_specs=[
            pl.BlockSpec((tm, d), lambda i: (i, 0)),
            pl.BlockSpec((tm, cwid), lambda i: (i, 0)),
            pl.BlockSpec((tm, cwid), lambda i: (i, 1)),
            pl.BlockSpec((1, cwid), lambda i: (0, 0)),
            pl.BlockSpec(ws.shape, lambda i: (0, 0, 0)),
            pl.BlockSpec(bs.shape, lambda i: (0, 0, 0)),
            pl.BlockSpec((cwid, d), lambda i: (0, 0)),
        ],
        out_specs=pl.BlockSpec((tm, d), lambda i: (i, 0)),
        out_shape=jax.ShapeDtypeStruct((n, d), F32),
        scratch_shapes=[pltpu.VMEM((tm, cwid), BF16)],
        compiler_params=_cparams(("parallel",)),
        name="sgu",
    )(x, z, z, vn_g.reshape(1, cwid), ws, bs, wo)


def _pad_rows(w, lo, total):
    return jnp.pad(w, ((lo, total - lo - w.shape[0]), (0, 0)))


def _even_layer(x, batch, seq_len, g, w_in, b_qkv, mu, w0, w2, a0, a2, g2, k_k, k_a, r_k,
                gn_g, gn_b, sinks, w_out, v_first, vres):
    aw = A_WIDTH
    a_cols = 3 * aw + DECAY_RANK + ICLR_RANK + GATE_RANK
    n_low = a_cols - 3 * aw
    b_cols = w_in.shape[1] - a_cols
    pad_low = LOW_PAD - n_low
    tail = (-(3 * aw + b_cols + LOW_PAD)) % 1024
    w_cat = jnp.concatenate([w_in[:, :3 * aw], w_in[:, a_cols:], w_in[:, 3 * aw:a_cols],
                             jnp.zeros((w_in.shape[0], pad_low + tail), F32)], axis=1).astype(BF16)
    b_cat = jnp.concatenate([jnp.zeros((3 * aw,), F32), b_qkv, jnp.zeros((LOW_PAD + tail,), F32)])
    z = _proj(x, g, w_cat, b_cat, gelu=False)

    mu_l = jnp.pad(mu[3 * aw:], (0, pad_low))
    row = lambda t: t.reshape(1, -1)
    w2p = _pad_rows(w2, 0, LOW_PAD).astype(BF16)
    a2p = _pad_rows(a2, DECAY_RANK, LOW_PAD).astype(BF16)
    g2p = _pad_rows(g2, DECAY_RANK + ICLR_RANK, LOW_PAD).astype(BF16)
    if vres is not None:
        v0, v1, v2 = vres
        rank = v1.shape[1]
        vres_args = (v_first, row(v0), jnp.pad(v1, ((0, 0), (0, VRES_PAD - rank))).astype(BF16),
                     _pad_rows(v2, 0, VRES_PAD).astype(BF16))
    else:
        vres_args = None
    outs = _rwkv(z, batch, seq_len, row(mu[:aw]), row(mu[aw:2 * aw]), row(mu[2 * aw:3 * aw]), row(mu_l),
                 row(w0), w2p, row(a0), a2p, g2p, row(k_k), row(k_a), row(r_k.reshape(-1)),
                 row(gn_g), row(gn_b), vres_args, emit_v=vres is None)
    oa, v = outs if vres is None else (outs[0], None)
    ob = _swa(z, sinks, batch, seq_len)
    wo = w_out.astype(BF16)
    x = _mix_out(x, oa, ob, wo[:aw], wo[aw:])
    return x, v


def _odd_layer(x, g, w_in, vn_g, w_s, b_s, w_out):
    z = _proj(x, g, w_in.astype(BF16), jnp.zeros((w_in.shape[1],), F32), gelu=True)
    ws = jnp.tril(w_s).astype(BF16)
    return _sgu(x, z, vn_g, ws, b_s[:, :, None], w_out.astype(BF16))


def kernel(x, norm_g, ffn_wg, ffn_wu, ffn_wd, e_w_in, e_b_qkv, e_mu, e_w0, e_w2, e_a0, e_a2, e_g2, e_k_k, e_k_a, e_r_k, e_gn_g, e_gn_b, e_sinks, e_w_out, vres_v0, vres_v1, vres_v2, o_w_in, o_vn_g, o_w_s, o_b_s, o_w_out, final_g):
    batch, seq_len, d = x.shape
    depth = norm_g.shape[0]
    x = x.reshape(batch * seq_len, d)
    fdim = ffn_wg.shape[-1]
    wg = ffn_wg.astype(BF16).reshape(depth * 2, d, fdim)
    wu = ffn_wu.astype(BF16).reshape(depth * 2, d, fdim)
    wd = ffn_wd.astype(BF16).reshape(depth * 2, fdim, d)
    v_first = None
    for layer in range(depth):
        x = _ffn(x, norm_g[layer, 0], wg, wu, wd, 2 * layer)
        if layer % 2 == 0:
            i = layer // 2
            vres = None if i == 0 else (vres_v0[i - 1], vres_v1[i - 1], vres_v2[i - 1])
            x, v_a = _even_layer(x, batch, seq_len, norm_g[layer, 1], e_w_in[i], e_b_qkv[i], e_mu[i],
                                 e_w0[i], e_w2[i], e_a0[i], e_a2[i], e_g2[i], e_k_k[i], e_k_a[i],
                                 e_r_k[i], e_gn_g[i], e_gn_b[i], e_sinks[i], e_w_out[i], v_first, vres)
            if i == 0:
                v_first = v_a
        else:
            j = layer // 2
            x = _odd_layer(x, norm_g[layer, 1], o_w_in[j], o_vn_g[j], o_w_s[j], o_b_s[j], o_w_out[j])
        x = _ffn(x, norm_g[layer, 2], wg, wu, wd, 2 * layer + 1,
                 g_out=final_g if layer == depth - 1 else None)
    return x.reshape(batch, seq_len, d)
```

```python
import functools
import math

import jax
import jax.numpy as jnp
from jax import lax
from jax.experimental import pallas as pl
from jax.experimental.pallas import tpu as pltpu

F32 = jnp.float32
BF16 = jnp.bfloat16

NORM_EPS = 1e-6
GN_EPS = 64e-5
HEAD_DIM = 64
A_HEADS = 16
A_WIDTH = A_HEADS * HEAD_DIM
B_HEADS = 16
B_KV_HEADS = 4
B_GROUP = B_HEADS // B_KV_HEADS
WINDOW = 128
DECAY_RANK = 64
ICLR_RANK = 64
GATE_RANK = 160
LOW_PAD = 384
VRES_PAD = 128
C_GROUPS = 16
C_CHUNK = 128
SCAN_CHUNK = 64

VMEM_LIMIT = 60 * 1024 * 1024


def _cparams(sem):
    return pltpu.CompilerParams(dimension_semantics=sem, vmem_limit_bytes=VMEM_LIMIT)


def _dot(a, b):
    return jnp.dot(a, b, preferred_element_type=F32)


def _dot_nt(a, b):
    return lax.dot_general(a, b, (((1,), (1,)), ((), ())), preferred_element_type=F32)


def _dot_tn(a, b):
    return lax.dot_general(a, b, (((0,), (0,)), ((), ())), preferred_element_type=F32)


def _split_dot(x, ones_bf16, left=False):
    x1 = x.astype(BF16)
    r1 = x - x1.astype(F32)
    x2 = r1.astype(BF16)
    x3 = (r1 - x2.astype(F32)).astype(BF16)
    if left:
        return _dot(ones_bf16, x1) + _dot(ones_bf16, x2) + _dot(ones_bf16, x3)
    return _dot(x1, ones_bf16) + _dot(x2, ones_bf16) + _dot(x3, ones_bf16)


def _rms(x, g):
    return x * lax.rsqrt(jnp.mean(x * x, axis=-1, keepdims=True) + NORM_EPS) * g


def _ffn_kernel(x_ref, g_ref, gout_ref, wg_ref, wu_ref, wd_ref, o_ref, h_ref):
    f = pl.program_id(1)

    @pl.when(f == 0)
    def _():
        x = x_ref[...]
        h_ref[...] = _rms(x, g_ref[...]).astype(BF16)
        o_ref[...] = x

    h = h_ref[...]
    a = _dot(h, wg_ref[...])
    b = _dot(h, wu_ref[...])
    p = (0.5 * a * jax.nn.sigmoid(a) * b).astype(BF16)
    o_ref[...] += _dot(p, wd_ref[...])

    if gout_ref is not None:
        @pl.when(f == pl.num_programs(1) - 1)
        def _():
            o_ref[...] = _rms(o_ref[...], gout_ref[...])


def _ffn_kernel_plain(x_ref, g_ref, wg_ref, wu_ref, wd_ref, o_ref, h_ref):
    _ffn_kernel(x_ref, g_ref, None, wg_ref, wu_ref, wd_ref, o_ref, h_ref)


def _ffn(x, g, wg, wu, wd, idx, g_out=None, tm=1024, tf=512):
    n, d = x.shape
    fdim = wg.shape[2]
    vec = pl.BlockSpec((1, d), lambda i, f: (0, 0))
    in_specs = [pl.BlockSpec((tm, d), lambda i, f: (i, 0)), vec]
    args = [x, g.reshape(1, d)]
    if g_out is not None:
        in_specs.append(vec)
        args.append(g_out.reshape(1, d))
    in_specs += [
        pl.BlockSpec((None, d, tf), lambda i, f: (idx, 0, f)),
        pl.BlockSpec((None, d, tf), lambda i, f: (idx, 0, f)),
        pl.BlockSpec((None, tf, d), lambda i, f: (idx, f, 0)),
    ]
    return pl.pallas_call(
        _ffn_kernel if g_out is not None else _ffn_kernel_plain,
        grid=(n // tm, fdim // tf),
        in_specs=in_specs,
        out_specs=pl.BlockSpec((tm, d), lambda i, f: (i, 0)),
        out_shape=jax.ShapeDtypeStruct((n, d), F32),
        scratch_shapes=[pltpu.VMEM((tm, d), BF16)],
        compiler_params=_cparams(("parallel", "arbitrary")),
        name="ffn",
    )(*args, wg, wu, wd)


def _proj_kernel(x_ref, g_ref, w_ref, b_ref, o_ref, h_ref, *, gelu):
    @pl.when(pl.program_id(1) == 0)
    def _():
        h_ref[...] = _rms(x_ref[...], g_ref[...]).astype(BF16)

    z = _dot(h_ref[...], w_ref[...]) + b_ref[...]
    if gelu:
        z = 0.5 * z * (1.0 + lax.erf(z * (2.0 ** -0.5)))
    o_ref[...] = z


def _proj(x, g, w, b, gelu, tm=1024, tn=1024):
    n, d = x.shape
    cols = w.shape[1]
    return pl.pallas_call(
        functools.partial(_proj_kernel, gelu=gelu),
        grid=(n // tm, cols // tn),
        in_specs=[
            pl.BlockSpec((tm, d), lambda i, j: (i, 0)),
            pl.BlockSpec((1, d), lambda i, j: (0, 0)),
            pl.BlockSpec((d, tn), lambda i, j: (0, j)),
            pl.BlockSpec((1, tn), lambda i, j: (0, j)),
        ],
        out_specs=pl.BlockSpec((tm, tn), lambda i, j: (i, j)),
        out_shape=jax.ShapeDtypeStruct((n, cols), F32),
        scratch_shapes=[pltpu.VMEM((tm, d), BF16)],
        compiler_params=_cparams(("parallel", "arbitrary")),
        name="proj_gelu" if gelu else "proj",
    )(x, g.reshape(1, d), w, b.reshape(1, cols))


HEAD_LANES = 128


def _split2(x):
    hi = x.astype(BF16)
    return hi, (x - hi.astype(F32)).astype(BF16)


def _head_sum(x):
    ch = lax.broadcasted_iota(jnp.int32, (A_WIDTH, HEAD_LANES), 0) // HEAD_DIM
    hh = lax.broadcasted_iota(jnp.int32, (A_WIDTH, HEAD_LANES), 1)
    down = jnp.where(ch == hh, 1.0, 0.0).astype(BF16)
    hi, lo = _split2(x)
    return _dot(hi, down) + _dot(lo, down)


def _head_bcast(s):
    hh = lax.broadcasted_iota(jnp.int32, (HEAD_LANES, A_WIDTH), 0)
    ch = lax.broadcasted_iota(jnp.int32, (HEAD_LANES, A_WIDTH), 1) // HEAD_DIM
    up = jnp.where(ch == hh, 1.0, 0.0).astype(BF16)
    hi, lo = _split2(s)
    return _dot(hi, up) + _dot(lo, up)


def _rwkv_kernel(*refs, rows, chunk, vres, emit_v):
    (zr_ref, zk_ref, zv_ref, zl_ref, pr_ref, pk_ref, pv_ref, plo_ref,
     mur_ref, muk_ref, muv_ref, mul_ref, w0_ref, w2_ref, a0_ref, a2_ref, g2_ref,
     kk_ref, ka_ref, rk_ref, gng_ref, gnb_ref) = refs[:22]
    refs = refs[22:]
    if vres:
        vf_ref, v0_ref, v1_ref, v2_ref = refs[:4]
        refs = refs[4:]
    oa_out = refs[0]
    refs = refs[1:]
    if emit_v:
        v_out = refs[0]
        refs = refs[1:]
    h_ref, r_sc, lw_sc, k_sc, v_sc, kk_sc, b_sc, g_sc, bonus_sc, y_sc = refs

    first = pl.program_id(1) == 0

    @pl.when(first)
    def _():
        h_ref[...] = jnp.zeros_like(h_ref)

    keep_prev = jnp.where(first, 0.0, 1.0)
    row = lax.broadcasted_iota(jnp.int32, (rows, 1), 0)

    def shifted(z_ref, p_ref, mu_ref):
        z = z_ref[...]
        prev_row = p_ref[7:8, :] * keep_prev
        prev = jnp.where(row == 0, prev_row, pltpu.roll(z, 1, axis=0))
        return z + (prev - z) * mu_ref[...]

    r = shifted(zr_ref, pr_ref, mur_ref)
    k = shifted(zk_ref, pk_ref, muk_ref)
    v = shifted(zv_ref, pv_ref, muv_ref)
    low = shifted(zl_ref, plo_ref, mul_ref)

    if vres:
        mix = _dot(_dot(v.astype(BF16), v1_ref[...]).astype(BF16), v2_ref[...])
        v = v + (vf_ref[...] - v) * jax.nn.sigmoid(v0_ref[...] + mix)

    dw = w0_ref[...] + _dot(jnp.tanh(low).astype(BF16), w2_ref[...])
    lw_sc[...] = -math.exp(-0.5) * jax.nn.sigmoid(dw)
    a = jax.nn.sigmoid(a0_ref[...] + _dot(low.astype(BF16), a2_ref[...]))
    g_sc[...] = _dot(jax.nn.sigmoid(low).astype(BF16), g2_ref[...])

    kk = k * kk_ref[...]
    nrm = jnp.sqrt(_head_sum(kk * kk))
    kk = kk * _head_bcast(1.0 / jnp.maximum(nrm, 1e-12))
    k = k * (1.0 + (a - 1.0) * ka_ref[...])
    r_sc[...] = r
    k_sc[...] = k
    v_sc[...] = v
    if emit_v:
        v_out[...] = v
    kk_sc[...] = kk
    b_sc[...] = kk * a
    bonus_sc[...] = _head_bcast(_head_sum(r * k * rk_ref[...])) * v

    for ci in range(rows // chunk):
        rs = pl.ds(ci * chunk, chunk)
        _scan_chunk(r_sc[rs, :], lw_sc[rs, :], k_sc[rs, :], v_sc[rs, :], kk_sc[rs, :], b_sc[rs, :],
                    h_ref, y_sc, rs)

    y = y_sc[...]
    yc = y - _head_bcast(_head_sum(y) * (1.0 / HEAD_DIM))
    var = _head_sum(yc * yc) * (1.0 / HEAD_DIM)
    yn = yc * _head_bcast(lax.rsqrt(var + GN_EPS)) * gng_ref[...] + gnb_ref[...]
    oa_out[...] = ((yn + bonus_sc[...]) * g_sc[...]).astype(oa_out.dtype)


def _rwkv(z, batch, seq_len, mu_r, mu_k, mu_v, mu_l, w0, w2p, a0, a2p, g2p, k_k, k_a, r_k,
          gn_g, gn_b, vres, emit_v, rows=256, chunk=SCAN_CHUNK):
    n = z.shape[0]
    aw = A_WIDTH
    low_blk = (3 * aw + B_HEADS * HEAD_DIM + 2 * B_KV_HEADS * HEAD_DIM) // LOW_PAD
    nr = seq_len // rows
    pstep = rows // 8

    def cur(col):
        return lambda bi, j: (bi * nr + j, col)

    def prev_map(col):
        return lambda bi, j: (jnp.maximum((bi * nr + j) * pstep - 1, 0), col)

    row_vec = lambda w: pl.BlockSpec((1, w), lambda bi, j: (0, 0))
    full = lambda a: pl.BlockSpec(a.shape, lambda bi, j: (0, 0))
    in_specs = [
        pl.BlockSpec((rows, aw), cur(0)),
        pl.BlockSpec((rows, aw), cur(1)),
        pl.BlockSpec((rows, aw), cur(2)),
        pl.BlockSpec((rows, LOW_PAD), cur(low_blk)),
        pl.BlockSpec((8, aw), prev_map(0)),
        pl.BlockSpec((8, aw), prev_map(1)),
        pl.BlockSpec((8, aw), prev_map(2)),
        pl.BlockSpec((8, LOW_PAD), prev_map(low_blk)),
        row_vec(aw), row_vec(aw), row_vec(aw), row_vec(LOW_PAD),
        row_vec(aw), full(w2p), row_vec(aw), full(a2p), full(g2p),
        row_vec(aw), row_vec(aw), row_vec(aw), row_vec(aw), row_vec(aw),
    ]
    args = [z, z, z, z, z, z, z, z, mu_r, mu_k, mu_v, mu_l, w0, w2p, a0, a2p, g2p, k_k, k_a, r_k,
            gn_g, gn_b]
    if vres is not None:
        v_first, v0, v1p, v2p = vres
        in_specs += [pl.BlockSpec((rows, aw), cur(0)), row_vec(aw), full(v1p), full(v2p)]
        args += [v_first, v0, v1p, v2p]
    out_specs = [pl.BlockSpec((rows, aw), cur(0))]
    out_shape = [jax.ShapeDtypeStruct((n, aw), BF16)]
    if emit_v:
        out_specs.append(pl.BlockSpec((rows, aw), cur(0)))
        out_shape.append(jax.ShapeDtypeStruct((n, aw), F32))
    state = pltpu.VMEM((A_HEADS // 2, 2 * HEAD_DIM, 2 * HEAD_DIM), F32)
    return pl.pallas_call(
        functools.partial(_rwkv_kernel, rows=rows, chunk=chunk, vres=vres is not None, emit_v=emit_v),
        grid=(batch, nr),
        in_specs=in_specs,
        out_specs=out_specs,
        out_shape=out_shape,
        scratch_shapes=[state] + [pltpu.VMEM((rows, aw), F32)] * 9,
        compiler_params=_cparams(("parallel", "arbitrary")),
        name="rwkv",
    )(*args)


def _scan_chunk(r, lw, k, v, kk, b, h_ref, y_ref, rows):
    c = r.shape[0]
    hd = HEAD_DIM
    pw = 2 * hd
    npair = A_HEADS // 2

    ri = lax.broadcasted_iota(jnp.int32, (c, c), 0)
    ci = lax.broadcasted_iota(jnp.int32, (c, c), 1)
    cum = _split_dot(lw, jnp.where(ri >= ci, 1.0, 0.0).astype(BF16), left=True)
    p_in = jnp.exp(cum)
    p_ex = jnp.exp(cum - lw)
    p_inv = jnp.exp(-cum)
    tail = cum[c - 1:c, :]
    p_tail = jnp.exp(tail - cum)
    p_all = jnp.exp(tail)
    a_t = (-kk * p_ex).astype(BF16)
    r_t = (r * p_in).astype(BF16)
    b_t = (b * p_inv).astype(BF16)
    k_t = (k * p_inv).astype(BF16)
    b_e = (b * p_tail).astype(BF16)
    k_e = (k * p_tail).astype(BF16)
    vb = v.astype(BF16)

    row = lax.broadcasted_iota(jnp.int32, (c, pw), 0)
    lane = lax.broadcasted_iota(jnp.int32, (c, pw), 1)
    left = lane < hd
    col = jnp.where(left, lane, lane - hd)
    strict = row > col
    incl = row >= col
    left2 = lax.broadcasted_iota(jnp.int32, (2 * c, pw), 1) < hd
    kr = lax.broadcasted_iota(jnp.int32, (pw, pw), 0)
    kc = lax.broadcasted_iota(jnp.int32, (pw, pw), 1)
    same_head = (kr < hd) == (kc < hd)
    zeros_cv = jnp.zeros((c, pw), BF16)

    pairs = [slice(p * pw, (p + 1) * pw) for p in range(npair)]
    every = range(npair)
    ar = [jnp.concatenate([a_t[:, s], r_t[:, s]], axis=0) for s in pairs]
    bk = [jnp.concatenate([b_t[:, s], k_t[:, s]], axis=0) for s in pairs]
    sc = [_dot_nt(jnp.concatenate([jnp.where(left2, ar[p], 0), jnp.where(left2, 0, ar[p])], axis=0), bk[p])
          for p in every]
    h0 = [h_ref[p] for p in every]
    ah = [_dot(ar[p], h0[p].astype(BF16)) for p in every]
    lt = [[jnp.where(strict, sc[p][q * 2 * c:q * 2 * c + c], 0.0) for q in range(2)] for p in every]
    mb = [[jnp.where(incl, sc[p][q * 2 * c + c:(q + 1) * 2 * c], 0.0).astype(BF16) for q in range(2)]
          for p in every]
    vv = [vb[:, s] for s in pairs]
    lv = [_dot(jnp.concatenate([lt[p][0], lt[p][1]], axis=0).astype(BF16),
               jnp.concatenate([zeros_cv, vv[p]], axis=0)) for p in every]
    rhs = [ah[p][:c] + jnp.where(left, lv[p][:c], lv[p][c:]) for p in every]
    x = [[jnp.where(left, lt[p][0], pltpu.roll(rhs[p], hd, axis=1)),
          jnp.where(left, lt[p][1], rhs[p])] for p in every]
    for _ in range(int(math.log2(c))):
        z = [[_dot(x[p][q][:, :hd].astype(BF16), x[p][q].astype(BF16)) for q in range(2)] for p in every]
        x = [[jnp.where(left, z[p][q], x[p][q] + z[p][q]) for q in range(2)] for p in every]
    uv = [jnp.concatenate([jnp.where(left, pltpu.roll(x[p][0], hd, axis=1), x[p][1]).astype(BF16), vv[p]],
                          axis=0) for p in every]
    yy = [_dot(jnp.concatenate([mb[p][0], mb[p][1]], axis=0), uv[p]) for p in every]
    for p in every:
        y_ref[rows, pairs[p]] = ah[p][c:] + jnp.where(left, yy[p][:c], yy[p][c:])
    hn = [_dot_tn(jnp.concatenate([b_e[:, s], k_e[:, s]], axis=0), uv[p]) for p, s in enumerate(pairs)]
    for p in every:
        decay = jnp.transpose(jnp.broadcast_to(p_all[:, pairs[p]], (pw, pw)))
        h_ref[p] = decay * h0[p] + jnp.where(same_head, hn[p], 0.0)


def _swa_kernel(sink_ref, q_ref, kc_ref, kp_ref, vc_ref, vp_ref, o_ref):
    blk = WINDOW
    first_key = jnp.where(pl.program_id(1) > 0, 0, blk)
    kj = lax.broadcasted_iota(jnp.int32, (2 * blk, blk), 0)
    qi = lax.broadcasted_iota(jnp.int32, (2 * blk, blk), 1)
    dist = qi + blk - kj
    valid = (dist >= 0) & (dist < WINDOW) & (kj >= first_key)
    distf = dist.astype(F32)
    scale = HEAD_DIM ** -0.5
    kband = jnp.concatenate([kp_ref[...], kc_ref[...]], axis=0).astype(BF16)
    vband = jnp.concatenate([vp_ref[...], vc_ref[...]], axis=0).astype(BF16)
    q = q_ref[...].astype(BF16)
    heads = range(B_HEADS)
    head_cols = lambda t, i: t[:, i * HEAD_DIM:(i + 1) * HEAD_DIM]
    qk = [_dot_nt(head_cols(kband, h // B_GROUP), head_cols(q, h)) for h in heads]
    s = [jnp.where(valid, qk[h] * scale - 2.0 ** (-8.0 * (h + 1) / B_HEADS) * distf, -1e30) for h in heads]
    m = [jnp.maximum(jnp.max(s[h], axis=0, keepdims=True), sink_ref[h]) for h in heads]
    p = [jnp.exp(s[h] - m[h]) for h in heads]
    den = [jnp.sum(p[h], axis=0, keepdims=True) + jnp.exp(sink_ref[h] - m[h]) for h in heads]
    pv = [_dot_tn(head_cols(vband, h // B_GROUP), p[h].astype(BF16)) for h in heads]
    for h in heads:
        o_ref[h * HEAD_DIM:(h + 1) * HEAD_DIM, :] = (pv[h] / den[h]).astype(o_ref.dtype)


def _swa(z, sinks, batch, seq_len):
    n = z.shape[0]
    nb = seq_len // WINDOW
    qw = B_HEADS * HEAD_DIM
    kvw = B_KV_HEADS * HEAD_DIM
    q_blk = 3 * A_WIDTH // qw
    k_blk = (3 * A_WIDTH + qw) // kvw
    cur = lambda col: (lambda bi, j: (bi * nb + j, col))
    prev = lambda col: (lambda bi, j: (bi * nb + jnp.maximum(j - 1, 0), col))
    return pl.pallas_call(
        _swa_kernel,
        grid=(batch, nb),
        in_specs=[
            pl.BlockSpec(memory_space=pltpu.SMEM),
            pl.BlockSpec((WINDOW, qw), cur(q_blk)),
            pl.BlockSpec((WINDOW, kvw), cur(k_blk)),
            pl.BlockSpec((WINDOW, kvw), prev(k_blk)),
            pl.BlockSpec((WINDOW, kvw), cur(k_blk + 1)),
            pl.BlockSpec((WINDOW, kvw), prev(k_blk + 1)),
        ],
        out_specs=pl.BlockSpec((qw, WINDOW), lambda bi, j: (0, bi * nb + j)),
        out_shape=jax.ShapeDtypeStruct((qw, n), BF16),
        compiler_params=_cparams(("parallel", "arbitrary")),
        name="swa",
    )(sinks, z, z, z, z, z)


def _mix_out_kernel(x_ref, oa_ref, obt_ref, wa_ref, wb_ref, o_ref):
    o_ref[...] = x_ref[...] + _dot(oa_ref[...], wa_ref[...]) + _dot_tn(obt_ref[...], wb_ref[...])


def _mix_out(x, oa, obt, wa, wb, tm=512):
    n, d = x.shape
    aw = oa.shape[1]
    bw = obt.shape[0]
    rows = lambda w: pl.BlockSpec((tm, w), lambda i: (i, 0))
    return pl.pallas_call(
        _mix_out_kernel,
        grid=(n // tm,),
        in_specs=[rows(d), rows(aw), pl.BlockSpec((bw, tm), lambda i: (0, i)),
                  pl.BlockSpec((aw, d), lambda i: (0, 0)), pl.BlockSpec((bw, d), lambda i: (0, 0))],
        out_specs=rows(d),
        out_shape=jax.ShapeDtypeStruct((n, d), F32),
        compiler_params=_cparams(("parallel",)),
        name="mix_out",
    )(x, oa, obt, wa, wb)


def _sgu_kernel(x_ref, u_ref, v_ref, vg_ref, ws_ref, bs_ref, wo_ref, o_ref, gate_ref, *, tm):
    cw = C_CHUNK
    vn = _rms(v_ref[...], vg_ref[...]).astype(BF16)
    for ch in range(tm // C_CHUNK):
        rows = slice(ch * C_CHUNK, (ch + 1) * C_CHUNK)
        for grp in range(C_GROUPS):
            cols = slice(grp * cw, (grp + 1) * cw)
            vm = _dot(ws_ref[grp], vn[rows, cols]) + bs_ref[grp]
            gate_ref[rows, cols] = (u_ref[rows, cols] * vm).astype(BF16)
    o_ref[...] = x_ref[...] + _dot(gate_ref[...], wo_ref[...])


def _sgu(x, z, vn_g, ws, bs, wo, tm=256):
    n, d = x.shape
    cwid = wo.shape[0]
    return pl.pallas_call(
        functools.partial(_sgu_kernel, tm=tm),
        grid=(n // tm,),
        in_specs=[
            pl.BlockSpec((tm, d), lambda i: (i, 0)),
            pl.BlockSpec((tm, cwid), lambda i: (i, 0)),
            pl.BlockSpec((tm, cwid), lambda i: (i, 1)),
            pl.BlockSpec((1, cwid), lambda i: (0, 0)),
            pl.BlockSpec(ws.shape, lambda i: (0, 0, 0)),
            pl.BlockSpec(bs.shape, lambda i: (0, 0, 0)),
            pl.BlockSpec((cwid, d), lambda i: (0, 0)),
        ],
        out_specs=pl.BlockSpec((tm, d), lambda i: (i, 0)),
        out_shape=jax.ShapeDtypeStruct((n, d), F32),
        scratch_shapes=[pltpu.VMEM((tm, cwid), BF16)],
        compiler_params=_cparams(("parallel",)),
        name="sgu",
    )(x, z, z, vn_g.reshape(1, cwid), ws, bs, wo)


def _pad_rows(w, lo, total):
    return jnp.pad(w, ((lo, total - lo - w.shape[0]), (0, 0)))


def _even_layer(x, batch, seq_len, g, w_in, b_qkv, mu, w0, w2, a0, a2, g2, k_k, k_a, r_k,
                gn_g, gn_b, sinks, w_out, v_first, vres):
    aw = A_WIDTH
    a_cols = 3 * aw + DECAY_RANK + ICLR_RANK + GATE_RANK
    n_low = a_cols - 3 * aw
    b_cols = w_in.shape[1] - a_cols
    pad_low = LOW_PAD - n_low
    tail = (-(3 * aw + b_cols + LOW_PAD)) % 1024
    w_cat = jnp.concatenate([w_in[:, :3 * aw], w_in[:, a_cols:], w_in[:, 3 * aw:a_cols],
                             jnp.zeros((w_in.shape[0], pad_low + tail), F32)], axis=1).astype(BF16)
    b_cat = jnp.concatenate([jnp.zeros((3 * aw,), F32), b_qkv, jnp.zeros((LOW_PAD + tail,), F32)])
    z = _proj(x, g, w_cat, b_cat, gelu=False)

    mu_l = jnp.pad(mu[3 * aw:], (0, pad_low))
    row = lambda t: t.reshape(1, -1)
    w2p = _pad_rows(w2, 0, LOW_PAD).astype(BF16)
    a2p = _pad_rows(a2, DECAY_RANK, LOW_PAD).astype(BF16)
    g2p = _pad_rows(g2, DECAY_RANK + ICLR_RANK, LOW_PAD).astype(BF16)
    if vres is not None:
        v0, v1, v2 = vres
        rank = v1.shape[1]
        vres_args = (v_first, row(v0), jnp.pad(v1, ((0, 0), (0, VRES_PAD - rank))).astype(BF16),
                     _pad_rows(v2, 0, VRES_PAD).astype(BF16))
    else:
        vres_args = None
    outs = _rwkv(z, batch, seq_len, row(mu[:aw]), row(mu[aw:2 * aw]), row(mu[2 * aw:3 * aw]), row(mu_l),
                 row(w0), w2p, row(a0), a2p, g2p, row(k_k), row(k_a), row(r_k.reshape(-1)),
                 row(gn_g), row(gn_b), vres_args, emit_v=vres is None)
    oa, v = outs if vres is None else (outs[0], None)
    ob = _swa(z, sinks, batch, seq_len)
    wo = w_out.astype(BF16)
    x = _mix_out(x, oa, ob, wo[:aw], wo[aw:])
    return x, v


def _odd_layer(x, g, w_in, vn_g, w_s, b_s, w_out):
    z = _proj(x, g, w_in.astype(BF16), jnp.zeros((w_in.shape[1],), F32), gelu=True)
    ws = jnp.tril(w_s).astype(BF16)
    return _sgu(x, z, vn_g, ws, b_s[:, :, None], w_out.astype(BF16))


def kernel(x, norm_g, ffn_wg, ffn_wu, ffn_wd, e_w_in, e_b_qkv, e_mu, e_w0, e_w2, e_a0, e_a2, e_g2, e_k_k, e_k_a, e_r_k, e_gn_g, e_gn_b, e_sinks, e_w_out, vres_v0, vres_v1, vres_v2, o_w_in, o_vn_g, o_w_s, o_b_s, o_w_out, final_g):
    batch, seq_len, d = x.shape
    depth = norm_g.shape[0]
    x = x.reshape(batch * seq_len, d)
    fdim = ffn_wg.shape[-1]
    wg = ffn_wg.astype(BF16).reshape(depth * 2, d, fdim)
    wu = ffn_wu.astype(BF16).reshape(depth * 2, d, fdim)
    wd = ffn_wd.astype(BF16).reshape(depth * 2, fdim, d)
    v_first = None
    for layer in range(depth):
        x = _ffn(x, norm_g[layer, 0], wg, wu, wd, 2 * layer)
        if layer % 2 == 0:
            i = layer // 2
            vres = None if i == 0 else (vres_v0[i - 1], vres_v1[i - 1], vres_v2[i - 1])
            x, v_a = _even_layer(x, batch, seq_len, norm_g[layer, 1], e_w_in[i], e_b_qkv[i], e_mu[i],
                                 e_w0[i], e_w2[i], e_a0[i], e_a2[i], e_g2[i], e_k_k[i], e_k_a[i],
                                 e_r_k[i], e_gn_g[i], e_gn_b[i], e_sinks[i], e_w_out[i], v_first, vres)
            if i == 0:
                v_first = v_a
        else:
            j = layer // 2
            x = _odd_layer(x, norm_g[layer, 1], o_w_in[j], o_vn_g[j], o_w_s[j], o_b_s[j], o_w_out[j])
        x = _ffn(x, norm_g[layer, 2], wg, wu, wd, 2 * layer + 1,
                 g_out=final_g if layer == depth - 1 else None)
    return x.reshape(batch, seq_len, d)
```

```python
import functools
import math

import jax
import jax.numpy as jnp
from jax import lax
from jax.experimental import pallas as pl
from jax.experimental.pallas import tpu as pltpu

F32 = jnp.float32
BF16 = jnp.bfloat16

NORM_EPS = 1e-6
GN_EPS = 64e-5
HEAD_DIM = 64
A_HEADS = 16
A_WIDTH = A_HEADS * HEAD_DIM
B_HEADS = 16
B_KV_HEADS = 4
B_GROUP = B_HEADS // B_KV_HEADS
WINDOW = 128
DECAY_RANK = 64
ICLR_RANK = 64
GATE_RANK = 160
LOW_PAD = 384
VRES_PAD = 128
C_GROUPS = 16
C_CHUNK = 128
SCAN_CHUNK = 64

VMEM_LIMIT = 60 * 1024 * 1024


def _cparams(sem):
    return pltpu.CompilerParams(dimension_semantics=sem, vmem_limit_bytes=VMEM_LIMIT)


def _dot(a, b):
    return jnp.dot(a, b, preferred_element_type=F32)


def _dot_nt(a, b):
    return lax.dot_general(a, b, (((1,), (1,)), ((), ())), preferred_element_type=F32)


def _dot_tn(a, b):
    return lax.dot_general(a, b, (((0,), (0,)), ((), ())), preferred_element_type=F32)


def _split_dot(x, ones_bf16, left=False):
    x1 = x.astype(BF16)
    r1 = x - x1.astype(F32)
    x2 = r1.astype(BF16)
    x3 = (r1 - x2.astype(F32)).astype(BF16)
    if left:
        return _dot(ones_bf16, x1) + _dot(ones_bf16, x2) + _dot(ones_bf16, x3)
    return _dot(x1, ones_bf16) + _dot(x2, ones_bf16) + _dot(x3, ones_bf16)


def _rms(x, g):
    return x * lax.rsqrt(jnp.mean(x * x, axis=-1, keepdims=True) + NORM_EPS) * g


def _ffn_kernel(x_ref, g_ref, gout_ref, wg_ref, wu_ref, wd_ref, o_ref, h_ref):
    f = pl.program_id(1)

    @pl.when(f == 0)
    def _():
        x = x_ref[...]
        h_ref[...] = _rms(x, g_ref[...]).astype(BF16)
        o_ref[...] = x

    h = h_ref[...]
    a = _dot(h, wg_ref[...])
    b = _dot(h, wu_ref[...])
    p = (0.5 * a * jax.nn.sigmoid(a) * b).astype(BF16)
    o_ref[...] += _dot(p, wd_ref[...])

    if gout_ref is not None:
        @pl.when(f == pl.num_programs(1) - 1)
        def _():
            o_ref[...] = _rms(o_ref[...], gout_ref[...])


def _ffn_kernel_plain(x_ref, g_ref, wg_ref, wu_ref, wd_ref, o_ref, h_ref):
    _ffn_kernel(x_ref, g_ref, None, wg_ref, wu_ref, wd_ref, o_ref, h_ref)


def _ffn(x, g, wg, wu, wd, idx, g_out=None, tm=1024, tf=512):
    n, d = x.shape
    fdim = wg.shape[2]
    vec = pl.BlockSpec((1, d), lambda i, f: (0, 0))
    in_specs = [pl.BlockSpec((tm, d), lambda i, f: (i, 0)), vec]
    args = [x, g.reshape(1, d)]
    if g_out is not None:
        in_specs.append(vec)
        args.append(g_out.reshape(1, d))
    in_specs += [
        pl.BlockSpec((None, d, tf), lambda i, f: (idx, 0, f)),
        pl.BlockSpec((None, d, tf), lambda i, f: (idx, 0, f)),
        pl.BlockSpec((None, tf, d), lambda i, f: (idx, f, 0)),
    ]
    return pl.pallas_call(
        _ffn_kernel if g_out is not None else _ffn_kernel_plain,
        grid=(n // tm, fdim // tf),
        in_specs=in_specs,
        out_specs=pl.BlockSpec((tm, d), lambda i, f: (i, 0)),
        out_shape=jax.ShapeDtypeStruct((n, d), F32),
        scratch_shapes=[pltpu.VMEM((tm, d), BF16)],
        compiler_params=_cparams(("parallel", "arbitrary")),
        name="ffn",
    )(*args, wg, wu, wd)


def _proj_kernel(x_ref, g_ref, w_ref, b_ref, o_ref, h_ref, *, gelu, tn):
    j = pl.program_id(1)

    @pl.when(j == 0)
    def _():
        h_ref[...] = _rms(x_ref[...], g_ref[...]).astype(BF16)

    cols = pl.ds(pl.multiple_of(j * tn, tn), tn)
    z = _dot(h_ref[...], w_ref[:, cols]) + b_ref[:, cols]
    if gelu:
        z = 0.5 * z * (1.0 + lax.erf(z * (2.0 ** -0.5)))
    o_ref[...] = z


def _proj(x, g, w, b, gelu, tm=1024, tn=1024):
    n, d = x.shape
    cols = w.shape[1]
    resident = lambda shape: pl.BlockSpec(shape, lambda i, j: (0, 0), pipeline_mode=pl.Buffered(1))
    return pl.pallas_call(
        functools.partial(_proj_kernel, gelu=gelu, tn=tn),
        grid=(n // tm, cols // tn),
        in_specs=[
            pl.BlockSpec((tm, d), lambda i, j: (i, 0)),
            pl.BlockSpec((1, d), lambda i, j: (0, 0)),
            resident((d, cols)),
            resident((1, cols)),
        ],
        out_specs=pl.BlockSpec((tm, tn), lambda i, j: (i, j)),
        out_shape=jax.ShapeDtypeStruct((n, cols), F32),
        scratch_shapes=[pltpu.VMEM((tm, d), BF16)],
        compiler_params=_cparams(("parallel", "arbitrary")),
        name="proj_gelu" if gelu else "proj",
    )(x, g.reshape(1, d), w, b.reshape(1, cols))


MXU_TILE = 256


def _head_sum(x):
    rh = lax.broadcasted_iota(jnp.int32, (MXU_TILE, MXU_TILE), 0) // HEAD_DIM
    ch = lax.broadcasted_iota(jnp.int32, (MXU_TILE, MXU_TILE), 1) // HEAD_DIM
    ones = jnp.where(rh == ch, 1.0, 0.0).astype(BF16)
    hi = x.astype(BF16)
    lo = (x - hi.astype(F32)).astype(BF16)
    blocks = [slice(s, s + MXU_TILE) for s in range(0, x.shape[1], MXU_TILE)]
    return jnp.concatenate([_dot(hi[:, s], ones) + _dot(lo[:, s], ones) for s in blocks], axis=1)


def _rwkv_kernel(*refs, rows, chunk, vres, emit_v):
    (zr_ref, zk_ref, zv_ref, zl_ref, pr_ref, pk_ref, pv_ref, plo_ref,
     mur_ref, muk_ref, muv_ref, mul_ref, w0_ref, w2_ref, a0_ref, a2_ref, g2_ref,
     kk_ref, ka_ref, rk_ref, gng_ref, gnb_ref) = refs[:22]
    refs = refs[22:]
    if vres:
        vf_ref, v0_ref, v1_ref, v2_ref = refs[:4]
        refs = refs[4:]
    oa_out = refs[0]
    refs = refs[1:]
    if emit_v:
        v_out = refs[0]
        refs = refs[1:]
    h_ref, r_sc, lw_sc, k_sc, v_sc, kk_sc, b_sc, g_sc, bonus_sc, y_sc = refs

    first = pl.program_id(1) == 0

    @pl.when(first)
    def _():
        h_ref[...] = jnp.zeros_like(h_ref)

    keep_prev = jnp.where(first, 0.0, 1.0)
    row = lax.broadcasted_iota(jnp.int32, (rows, 1), 0)

    def shifted(z_ref, p_ref, mu_ref):
        z = z_ref[...]
        prev_row = p_ref[7:8, :] * keep_prev
        prev = jnp.where(row == 0, prev_row, pltpu.roll(z, 1, axis=0))
        return z + (prev - z) * mu_ref[...]

    r = shifted(zr_ref, pr_ref, mur_ref)
    k = shifted(zk_ref, pk_ref, muk_ref)
    v = shifted(zv_ref, pv_ref, muv_ref)
    low = shifted(zl_ref, plo_ref, mul_ref)

    if vres:
        mix = _dot(_dot(v.astype(BF16), v1_ref[...]).astype(BF16), v2_ref[...])
        v = v + (vf_ref[...] - v) * jax.nn.sigmoid(v0_ref[...] + mix)

    dw = w0_ref[...] + _dot(jnp.tanh(low).astype(BF16), w2_ref[...])
    lw_sc[...] = -math.exp(-0.5) * jax.nn.sigmoid(dw)
    a = jax.nn.sigmoid(a0_ref[...] + _dot(low.astype(BF16), a2_ref[...]))
    g_sc[...] = _dot(jax.nn.sigmoid(low).astype(BF16), g2_ref[...])

    kk = k * kk_ref[...]
    kk = kk * jnp.minimum(lax.rsqrt(_head_sum(kk * kk)), 1e12)
    k = k * (1.0 + (a - 1.0) * ka_ref[...])
    r_sc[...] = r
    k_sc[...] = k
    v_sc[...] = v
    if emit_v:
        v_out[...] = v
    kk_sc[...] = kk
    b_sc[...] = kk * a
    bonus_sc[...] = _head_sum(r * k * rk_ref[...]) * v

    _scan_tile((r_sc, lw_sc, k_sc, v_sc, kk_sc, b_sc), h_ref, y_sc, chunk)

    y = y_sc[...]
    yc = y - _head_sum(y) * (1.0 / HEAD_DIM)
    var = _head_sum(yc * yc) * (1.0 / HEAD_DIM)
    yn = yc * lax.rsqrt(var + GN_EPS) * gng_ref[...] + gnb_ref[...]
    oa_out[...] = ((yn + bonus_sc[...]) * g_sc[...]).astype(oa_out.dtype)


def _rwkv(z, batch, seq_len, mu_r, mu_k, mu_v, mu_l, w0, w2p, a0, a2p, g2p, k_k, k_a, r_k,
          gn_g, gn_b, vres, emit_v, rows=256, chunk=SCAN_CHUNK):
    n = z.shape[0]
    aw = A_WIDTH
    low_blk = (3 * aw + B_HEADS * HEAD_DIM + 2 * B_KV_HEADS * HEAD_DIM) // LOW_PAD
    nr = seq_len // rows
    pstep = rows // 8

    def cur(col):
        return lambda bi, j: (bi * nr + j, col)

    def prev_map(col):
        return lambda bi, j: (jnp.maximum((bi * nr + j) * pstep - 1, 0), col)

    row_vec = lambda w: pl.BlockSpec((1, w), lambda bi, j: (0, 0))
    full = lambda a: pl.BlockSpec(a.shape, lambda bi, j: (0, 0))
    in_specs = [
        pl.BlockSpec((rows, aw), cur(0)),
        pl.BlockSpec((rows, aw), cur(1)),
        pl.BlockSpec((rows, aw), cur(2)),
        pl.BlockSpec((rows, LOW_PAD), cur(low_blk)),
        pl.BlockSpec((8, aw), prev_map(0)),
        pl.BlockSpec((8, aw), prev_map(1)),
        pl.BlockSpec((8, aw), prev_map(2)),
        pl.BlockSpec((8, LOW_PAD), prev_map(low_blk)),
        row_vec(aw), row_vec(aw), row_vec(aw), row_vec(LOW_PAD),
        row_vec(aw), full(w2p), row_vec(aw), full(a2p), full(g2p),
        row_vec(aw), row_vec(aw), row_vec(aw), row_vec(aw), row_vec(aw),
    ]
    args = [z, z, z, z, z, z, z, z, mu_r, mu_k, mu_v, mu_l, w0, w2p, a0, a2p, g2p, k_k, k_a, r_k,
            gn_g, gn_b]
    if vres is not None:
        v_first, v0, v1p, v2p = vres
        in_specs += [pl.BlockSpec((rows, aw), cur(0)), row_vec(aw), full(v1p), full(v2p)]
        args += [v_first, v0, v1p, v2p]
    out_specs = [pl.BlockSpec((rows, aw), cur(0))]
    out_shape = [jax.ShapeDtypeStruct((n, aw), BF16)]
    if emit_v:
        out_specs.append(pl.BlockSpec((rows, aw), cur(0)))
        out_shape.append(jax.ShapeDtypeStruct((n, aw), F32))
    state = pltpu.VMEM((A_HEADS // 2, 2 * HEAD_DIM, 2 * HEAD_DIM), F32)
    return pl.pallas_call(
        functools.partial(_rwkv_kernel, rows=rows, chunk=chunk, vres=vres is not None, emit_v=emit_v),
        grid=(batch, nr),
        in_specs=in_specs,
        out_specs=out_specs,
        out_shape=out_shape,
        scratch_shapes=[state] + [pltpu.VMEM((rows, aw), F32)] * 9,
        compiler_params=_cparams(("parallel", "arbitrary")),
        name="rwkv",
    )(*args)


def _scan_tile(refs, h_ref, y_ref, chunk, group=8):
    c = chunk
    hd = HEAD_DIM
    pw = 2 * hd
    npair = A_HEADS // 2
    nchunk = refs[0].shape[0] // c
    pairs = [slice(p * pw, (p + 1) * pw) for p in range(npair)]
    every = [(ci, p) for ci in range(nchunk) for p in range(npair)]

    ri = lax.broadcasted_iota(jnp.int32, (c, c), 0)
    ci_ = lax.broadcasted_iota(jnp.int32, (c, c), 1)
    tri_ones = jnp.where(ri >= ci_, 1.0, 0.0).astype(BF16)
    row = lax.broadcasted_iota(jnp.int32, (c, pw), 0)
    lane = lax.broadcasted_iota(jnp.int32, (c, pw), 1)
    left = lane < hd
    col = jnp.where(left, lane, lane - hd)
    strict = row > col
    incl = row >= col
    eye_right = jnp.where((row == col) & (lane >= hd), 1.0, 0.0)
    left2 = lax.broadcasted_iota(jnp.int32, (2 * c, pw), 1) < hd
    kr = lax.broadcasted_iota(jnp.int32, (pw, pw), 0)
    kc = lax.broadcasted_iota(jnp.int32, (pw, pw), 1)
    same_head = (kr < hd) == (kc < hd)
    zeros_cv = jnp.zeros((c, pw), BF16)

    ar, bk, bke, vv, p_all = {}, {}, {}, {}, []
    for ci in range(nchunk):
        rs = pl.ds(ci * c, c)
        r, lw, k, v, kk, b = (ref[rs, :] for ref in refs)
        cum = _split_dot(lw, tri_ones, left=True)
        p_inv = jnp.exp(-cum)
        tail = cum[c - 1:c, :]
        p_tail = jnp.exp(tail - cum)
        p_all.append(jnp.exp(tail))
        a_t = (-kk * jnp.exp(cum - lw)).astype(BF16)
        r_t = (r * jnp.exp(cum)).astype(BF16)
        b_t = (b * p_inv).astype(BF16)
        k_t = (k * p_inv).astype(BF16)
        b_e = (b * p_tail).astype(BF16)
        k_e = (k * p_tail).astype(BF16)
        vb = v.astype(BF16)
        for p, s in enumerate(pairs):
            ar[ci, p] = jnp.concatenate([a_t[:, s], r_t[:, s]], axis=0)
            bk[ci, p] = jnp.concatenate([b_t[:, s], k_t[:, s]], axis=0)
            bke[ci, p] = jnp.concatenate([b_e[:, s], k_e[:, s]], axis=0)
            vv[ci, p] = vb[:, s]

    mb, lv, inv = {}, {}, {}
    for g in range(0, len(every), group):
        grp = every[g:g + group]
        sc = {i: _dot_nt(jnp.concatenate([jnp.where(left2, ar[i], 0), jnp.where(left2, 0, ar[i])], axis=0),
                         bk[i]) for i in grp}
        lt = {i: [jnp.where(strict, sc[i][q * 2 * c:q * 2 * c + c], 0.0) for q in range(2)] for i in grp}
        for i in grp:
            mb[i] = jnp.concatenate([jnp.where(incl, sc[i][q * 2 * c + c:(q + 1) * 2 * c], 0.0).astype(BF16)
                                     for q in range(2)], axis=0)
        lvf = {i: _dot(jnp.concatenate(lt[i], axis=0).astype(BF16),
                       jnp.concatenate([zeros_cv, vv[i]], axis=0)) for i in grp}
        for i in grp:
            lv[i] = jnp.where(left, lvf[i][:c], lvf[i][c:])
        x = {i: [jnp.where(left, lt[i][q], eye_right) for q in range(2)] for i in grp}
        for _ in range(int(math.log2(c))):
            z = {i: [_dot(x[i][q][:, :hd].astype(BF16), x[i][q].astype(BF16)) for q in range(2)] for i in grp}
            x = {i: [jnp.where(left, z[i][q], x[i][q] + z[i][q]) for q in range(2)] for i in grp}
        for i in grp:
            inv[i] = jnp.where(left, pltpu.roll(x[i][0], hd, axis=1), x[i][1]).astype(BF16)

    h = [h_ref[p] for p in range(npair)]
    for ci in range(nchunk):
        here = [(ci, p) for p in range(npair)]
        ah = [_dot(ar[i], h[p].astype(BF16)) for p, i in enumerate(here)]
        rhs = [ah[p][:c] + lv[i] for p, i in enumerate(here)]
        rhs = [jnp.concatenate([jnp.where(left, t, 0.0), jnp.where(left, 0.0, t)], axis=0).astype(BF16)
               for t in rhs]
        uv = [jnp.concatenate([_dot(inv[i], rhs[p]).astype(BF16), vv[i]], axis=0)
              for p, i in enumerate(here)]
        yy = [_dot(mb[i], uv[p]) for p, i in enumerate(here)]
        for p in range(npair):
            y_ref[pl.ds(ci * c, c), pairs[p]] = ah[p][c:] + jnp.where(left, yy[p][:c], yy[p][c:])
        hn = [_dot_tn(bke[i], uv[p]) for p, i in enumerate(here)]
        h = [jnp.transpose(jnp.broadcast_to(p_all[ci][:, pairs[p]], (pw, pw))) * h[p]
             + jnp.where(same_head, hn[p], 0.0) for p in range(npair)]
    for p in range(npair):
        h_ref[p] = h[p]


def _swa_kernel(sink_ref, q_ref, kc_ref, kp_ref, vc_ref, vp_ref, o_ref):
    blk = WINDOW
    first_key = jnp.where(pl.program_id(1) > 0, 0, blk)
    kj = lax.broadcasted_iota(jnp.int32, (2 * blk, blk), 0)
    qi = lax.broadcasted_iota(jnp.int32, (2 * blk, blk), 1)
    dist = qi + blk - kj
    valid = (dist >= 0) & (dist < WINDOW) & (kj >= first_key)
    distf = dist.astype(F32)
    scale = HEAD_DIM ** -0.5
    kband = jnp.concatenate([kp_ref[...], kc_ref[...]], axis=0).astype(BF16)
    vband = jnp.concatenate([vp_ref[...], vc_ref[...]], axis=0).astype(BF16)
    q = q_ref[...].astype(BF16)
    heads = range(B_HEADS)
    head_cols = lambda t, i: t[:, i * HEAD_DIM:(i + 1) * HEAD_DIM]
    qk = [_dot_nt(head_cols(kband, h // B_GROUP), head_cols(q, h)) for h in heads]
    s = [jnp.where(valid, qk[h] * scale - 2.0 ** (-8.0 * (h + 1) / B_HEADS) * distf, -1e30) for h in heads]
    m = [jnp.maximum(jnp.max(s[h], axis=0, keepdims=True), sink_ref[h]) for h in heads]
    p = [jnp.exp(s[h] - m[h]) for h in heads]
    den = [jnp.sum(p[h], axis=0, keepdims=True) + jnp.exp(sink_ref[h] - m[h]) for h in heads]
    pv = [_dot_tn(head_cols(vband, h // B_GROUP), p[h].astype(BF16)) for h in heads]
    for h in heads:
        o_ref[h * HEAD_DIM:(h + 1) * HEAD_DIM, :] = (pv[h] / den[h]).astype(o_ref.dtype)


def _swa(z, sinks, batch, seq_len):
    n = z.shape[0]
    nb = seq_len // WINDOW
    qw = B_HEADS * HEAD_DIM
    kvw = B_KV_HEADS * HEAD_DIM
    q_blk = 3 * A_WIDTH // qw
    k_blk = (3 * A_WIDTH + qw) // kvw
    cur = lambda col: (lambda bi, j: (bi * nb + j, col))
    prev = lambda col: (lambda bi, j: (bi * nb + jnp.maximum(j - 1, 0), col))
    return pl.pallas_call(
        _swa_kernel,
        grid=(batch, nb),
        in_specs=[
            pl.BlockSpec(memory_space=pltpu.SMEM),
            pl.BlockSpec((WINDOW, qw), cur(q_blk)),
            pl.BlockSpec((WINDOW, kvw), cur(k_blk)),
            pl.BlockSpec((WINDOW, kvw), prev(k_blk)),
            pl.BlockSpec((WINDOW, kvw), cur(k_blk + 1)),
            pl.BlockSpec((WINDOW, kvw), prev(k_blk + 1)),
        ],
        out_specs=pl.BlockSpec((qw, WINDOW), lambda bi, j: (0, bi * nb + j)),
        out_shape=jax.ShapeDtypeStruct((qw, n), BF16),
        compiler_params=_cparams(("parallel", "arbitrary")),
        name="swa",
    )(sinks, z, z, z, z, z)


def _mix_out_kernel(x_ref, oa_ref, obt_ref, wa_ref, wb_ref, o_ref):
    o_ref[...] = x_ref[...] + _dot(oa_ref[...], wa_ref[...]) + _dot_tn(obt_ref[...], wb_ref[...])


def _mix_out(x, oa, obt, wa, wb, tm=512):
    n, d = x.shape
    aw = oa.shape[1]
    bw = obt.shape[0]
    rows = lambda w: pl.BlockSpec((tm, w), lambda i: (i, 0))
    return pl.pallas_call(
        _mix_out_kernel,
        grid=(n // tm,),
        in_specs=[rows(d), rows(aw), pl.BlockSpec((bw, tm), lambda i: (0, i)),
                  pl.BlockSpec((aw, d), lambda i: (0, 0)), pl.BlockSpec((bw, d), lambda i: (0, 0))],
        out_specs=rows(d),
        out_shape=jax.ShapeDtypeStruct((n, d), F32),
        compiler_params=_cparams(("parallel",)),
        name="mix_out",
    )(x, oa, obt, wa, wb)


def _sgu_kernel(x_ref, u_ref, v_ref, vg_ref, ws_ref, bs_ref, wo_ref, o_ref, gate_ref, *, tm):
    cw = C_CHUNK
    vn = _rms(v_ref[...], vg_ref[...]).astype(BF16)
    for ch in range(tm // C_CHUNK):
        rows = slice(ch * C_CHUNK, (ch + 1) * C_CHUNK)
        for grp in range(C_GROUPS):
            cols = slice(grp * cw, (grp + 1) * cw)
            vm = _dot(ws_ref[grp], vn[rows, cols]) + bs_ref[grp]
            gate_ref[rows, cols] = (u_ref[rows, cols] * vm).astype(BF16)
    o_ref[...] = x_ref[...] + _dot(gate_ref[...], wo_ref[...])


def _sgu(x, z, vn_g, ws, bs, wo, tm=256):
    n, d = x.shape
    cwid = wo.shape[0]
    return pl.pallas_call(
        functools.partial(_sgu_kernel, tm=tm),
        grid=(n // tm,),
        in_specs=[
            pl.BlockSpec((tm, d), lambda i: (i, 0)),
            pl.BlockSpec((tm, cwid), lambda i: (i, 0)),
            pl.BlockSpec((tm, cwid), lambda i: (i, 1)),
            pl.BlockSpec((1, cwid), lambda i: (0, 0)),
            pl.BlockSpec(ws.shape, lambda i: (0, 0, 0)),
            pl.BlockSpec(bs.shape, lambda i: (0, 0, 0)),
            pl.BlockSpec((cwid, d), lambda i: (0, 0)),
        ],
        out_specs=pl.BlockSpec((tm, d), lambda i: (i, 0)),
        out_shape=jax.ShapeDtypeStruct((n, d), F32),
        scratch_shapes=[pltpu.VMEM((tm, cwid), BF16)],
        compiler_params=_cparams(("parallel",)),
        name="sgu",
    )(x, z, z, vn_g.reshape(1, cwid), ws, bs, wo)


def _pad_rows(w, lo, total):
    return jnp.pad(w, ((lo, total - lo - w.shape[0]), (0, 0)))


def _even_layer(x, batch, seq_len, g, w_in, b_qkv, mu, w0, w2, a0, a2, g2, k_k, k_a, r_k,
                gn_g, gn_b, sinks, w_out, v_first, vres):
    aw = A_WIDTH
    a_cols = 3 * aw + DECAY_RANK + ICLR_RANK + GATE_RANK
    n_low = a_cols - 3 * aw
    b_cols = w_in.shape[1] - a_cols
    pad_low = LOW_PAD - n_low
    tail = (-(3 * aw + b_cols + LOW_PAD)) % 1024
    w_cat = jnp.concatenate([w_in[:, :3 * aw], w_in[:, a_cols:], w_in[:, 3 * aw:a_cols],
                             jnp.zeros((w_in.shape[0], pad_low + tail), F32)], axis=1).astype(BF16)
    b_cat = jnp.concatenate([jnp.zeros((3 * aw,), F32), b_qkv, jnp.zeros((LOW_PAD + tail,), F32)])
    z = _proj(x, g, w_cat, b_cat, gelu=False)

    mu_l = jnp.pad(mu[3 * aw:], (0, pad_low))
    row = lambda t: t.reshape(1, -1)
    w2p = _pad_rows(w2, 0, LOW_PAD).astype(BF16)
    a2p = _pad_rows(a2, DECAY_RANK, LOW_PAD).astype(BF16)
    g2p = _pad_rows(g2, DECAY_RANK + ICLR_RANK, LOW_PAD).astype(BF16)
    if vres is not None:
        v0, v1, v2 = vres
        rank = v1.shape[1]
        vres_args = (v_first, row(v0), jnp.pad(v1, ((0, 0), (0, VRES_PAD - rank))).astype(BF16),
                     _pad_rows(v2, 0, VRES_PAD).astype(BF16))
    else:
        vres_args = None
    outs = _rwkv(z, batch, seq_len, row(mu[:aw]), row(mu[aw:2 * aw]), row(mu[2 * aw:3 * aw]), row(mu_l),
                 row(w0), w2p, row(a0), a2p, g2p, row(k_k), row(k_a), row(r_k.reshape(-1)),
                 row(gn_g), row(gn_b), vres_args, emit_v=vres is None)
    oa, v = outs if vres is None else (outs[0], None)
    ob = _swa(z, sinks, batch, seq_len)
    wo = w_out.astype(BF16)
    x = _mix_out(x, oa, ob, wo[:aw], wo[aw:])
    return x, v


def _odd_layer(x, g, w_in, vn_g, w_s, b_s, w_out):
    z = _proj(x, g, w_in.astype(BF16), jnp.zeros((w_in.shape[1],), F32), gelu=True)
    ws = jnp.tril(w_s).astype(BF16)
    return _sgu(x, z, vn_g, ws, b_s[:, :, None], w_out.astype(BF16))


def kernel(x, norm_g, ffn_wg, ffn_wu, ffn_wd, e_w_in, e_b_qkv, e_mu, e_w0, e_w2, e_a0, e_a2, e_g2, e_k_k, e_k_a, e_r_k, e_gn_g, e_gn_b, e_sinks, e_w_out, vres_v0, vres_v1, vres_v2, o_w_in, o_vn_g, o_w_s, o_b_s, o_w_out, final_g):
    batch, seq_len, d = x.shape
    depth = norm_g.shape[0]
    x = x.reshape(batch * seq_len, d)
    fdim = ffn_wg.shape[-1]
    wg = ffn_wg.astype(BF16).reshape(depth * 2, d, fdim)
    wu = ffn_wu.astype(BF16).reshape(depth * 2, d, fdim)
    wd = ffn_wd.astype(BF16).reshape(depth * 2, fdim, d)
    v_first = None
    for layer in range(depth):
        x = _ffn(x, norm_g[layer, 0], wg, wu, wd, 2 * layer)
        if layer % 2 == 0:
            i = layer // 2
            vres = None if i == 0 else (vres_v0[i - 1], vres_v1[i - 1], vres_v2[i - 1])
            x, v_a = _even_layer(x, batch, seq_len, norm_g[layer, 1], e_w_in[i], e_b_qkv[i], e_mu[i],
                                 e_w0[i], e_w2[i], e_a0[i], e_a2[i], e_g2[i], e_k_k[i], e_k_a[i],
                                 e_r_k[i], e_gn_g[i], e_gn_b[i], e_sinks[i], e_w_out[i], v_first, vres)
            if i == 0:
                v_first = v_a
        else:
            j = layer // 2
            x = _odd_layer(x, norm_g[layer, 1], o_w_in[j], o_vn_g[j], o_w_s[j], o_b_s[j], o_w_out[j])
        x = _ffn(x, norm_g[layer, 2], wg, wu, wd, 2 * layer + 1,
                 g_out=final_g if layer == depth - 1 else None)
    return x.reshape(batch, seq_len, d)
```

```python
import functools
import math

import jax
import jax.numpy as jnp
from jax import lax
from jax.experimental import pallas as pl
from jax.experimental.pallas import tpu as pltpu

F32 = jnp.float32
BF16 = jnp.bfloat16

NORM_EPS = 1e-6
GN_EPS = 64e-5
HEAD_DIM = 64
A_HEADS = 16
A_WIDTH = A_HEADS * HEAD_DIM
B_HEADS = 16
B_KV_HEADS = 4
B_GROUP = B_HEADS // B_KV_HEADS
WINDOW = 128
DECAY_RANK = 64
ICLR_RANK = 64
GATE_RANK = 160
LOW_PAD = 384
VRES_PAD = 128
C_GROUPS = 16
C_CHUNK = 128
SCAN_CHUNK = 64

VMEM_LIMIT = 60 * 1024 * 1024


def _cparams(sem):
    return pltpu.CompilerParams(dimension_semantics=sem, vmem_limit_bytes=VMEM_LIMIT)


def _dot(a, b):
    return jnp.dot(a, b, preferred_element_type=F32)


def _dot_nt(a, b):
    return lax.dot_general(a, b, (((1,), (1,)), ((), ())), preferred_element_type=F32)


def _dot_tn(a, b):
    return lax.dot_general(a, b, (((0,), (0,)), ((), ())), preferred_element_type=F32)


def _split_dot(x, ones_bf16, left=False):
    x1 = x.astype(BF16)
    r1 = x - x1.astype(F32)
    x2 = r1.astype(BF16)
    x3 = (r1 - x2.astype(F32)).astype(BF16)
    if left:
        return _dot(ones_bf16, x1) + _dot(ones_bf16, x2) + _dot(ones_bf16, x3)
    return _dot(x1, ones_bf16) + _dot(x2, ones_bf16) + _dot(x3, ones_bf16)


def _rms(x, g):
    return x * lax.rsqrt(jnp.mean(x * x, axis=-1, keepdims=True) + NORM_EPS) * g


def _ffn_kernel(x_ref, g_ref, gout_ref, wg_ref, wu_ref, wd_ref, o_ref, h_ref):
    f = pl.program_id(1)

    def half_swiglu(h):
        a = _dot(h, wg_ref[...])
        b = _dot(h, wu_ref[...])
        p = (0.5 * a * jax.nn.sigmoid(a) * b).astype(BF16)
        return _dot(p, wd_ref[...])

    @pl.when(f == 0)
    def _():
        half = x_ref.shape[0] // 2
        for s in range(2):
            rows = pl.ds(s * half, half)
            x = x_ref[rows, :]
            h = _rms(x, g_ref[...]).astype(BF16)
            h_ref[rows, :] = h
            o_ref[rows, :] = x + half_swiglu(h)

    @pl.when(f > 0)
    def _():
        o_ref[...] += half_swiglu(h_ref[...])

    if gout_ref is not None:
        @pl.when(f == pl.num_programs(1) - 1)
        def _():
            o_ref[...] = _rms(o_ref[...], gout_ref[...])


def _ffn_kernel_plain(x_ref, g_ref, wg_ref, wu_ref, wd_ref, o_ref, h_ref):
    _ffn_kernel(x_ref, g_ref, None, wg_ref, wu_ref, wd_ref, o_ref, h_ref)


def _ffn(x, g, wg, wu, wd, idx, g_out=None, tm=1024, tf=512):
    n, d = x.shape
    fdim = wg.shape[2]
    vec = pl.BlockSpec((1, d), lambda i, f: (0, 0))
    in_specs = [pl.BlockSpec((tm, d), lambda i, f: (i, 0)), vec]
    args = [x, g.reshape(1, d)]
    if g_out is not None:
        in_specs.append(vec)
        args.append(g_out.reshape(1, d))
    in_specs += [
        pl.BlockSpec((None, d, tf), lambda i, f: (idx, 0, f)),
        pl.BlockSpec((None, d, tf), lambda i, f: (idx, 0, f)),
        pl.BlockSpec((None, tf, d), lambda i, f: (idx, f, 0)),
    ]
    return pl.pallas_call(
        _ffn_kernel if g_out is not None else _ffn_kernel_plain,
        grid=(n // tm, fdim // tf),
        in_specs=in_specs,
        out_specs=pl.BlockSpec((tm, d), lambda i, f: (i, 0)),
        out_shape=jax.ShapeDtypeStruct((n, d), F32),
        scratch_shapes=[pltpu.VMEM((tm, d), BF16)],
        compiler_params=_cparams(("parallel", "arbitrary")),
        name="ffn",
    )(*args, wg, wu, wd)


def _proj_kernel(x_ref, g_ref, w_ref, b_ref, o_ref, h_ref, *, gelu, tn):
    j = pl.program_id(1)

    cols = pl.ds(pl.multiple_of(j * tn, tn), tn)

    def project(h):
        z = _dot(h, w_ref[:, cols]) + b_ref[:, cols]
        if gelu:
            z = 0.5 * z * (1.0 + lax.erf(z * (2.0 ** -0.5)))
        return z

    @pl.when(j == 0)
    def _():
        half = x_ref.shape[0] // 2
        for s in range(2):
            rows = pl.ds(s * half, half)
            h = _rms(x_ref[rows, :], g_ref[...]).astype(BF16)
            h_ref[rows, :] = h
            o_ref[rows, :] = project(h)

    @pl.when(j > 0)
    def _():
        o_ref[...] = project(h_ref[...])


def _proj(x, g, w, b, gelu, tm=1024, tn=1024):
    n, d = x.shape
    cols = w.shape[1]
    resident = lambda shape: pl.BlockSpec(shape, lambda i, j: (0, 0), pipeline_mode=pl.Buffered(1))
    return pl.pallas_call(
        functools.partial(_proj_kernel, gelu=gelu, tn=tn),
        grid=(n // tm, cols // tn),
        in_specs=[
            pl.BlockSpec((tm, d), lambda i, j: (i, 0)),
            pl.BlockSpec((1, d), lambda i, j: (0, 0)),
            resident((d, cols)),
            resident((1, cols)),
        ],
        out_specs=pl.BlockSpec((tm, tn), lambda i, j: (i, j)),
        out_shape=jax.ShapeDtypeStruct((n, cols), F32),
        scratch_shapes=[pltpu.VMEM((tm, d), BF16)],
        compiler_params=_cparams(("parallel", "arbitrary")),
        name="proj_gelu" if gelu else "proj",
    )(x, g.reshape(1, d), w, b.reshape(1, cols))


MXU_TILE = 256


def _head_sum(x):
    rh = lax.broadcasted_iota(jnp.int32, (MXU_TILE, MXU_TILE), 0) // HEAD_DIM
    ch = lax.broadcasted_iota(jnp.int32, (MXU_TILE, MXU_TILE), 1) // HEAD_DIM
    ones = jnp.where(rh == ch, 1.0, 0.0).astype(BF16)
    hi = x.astype(BF16)
    lo = (x - hi.astype(F32)).astype(BF16)
    blocks = [slice(s, s + MXU_TILE) for s in range(0, x.shape[1], MXU_TILE)]
    return jnp.concatenate([_dot(hi[:, s], ones) + _dot(lo[:, s], ones) for s in blocks], axis=1)


def _rwkv_kernel(*refs, rows, chunk, vres, emit_v):
    (zr_ref, zk_ref, zv_ref, zl_ref, pr_ref, pk_ref, pv_ref, plo_ref,
     mur_ref, muk_ref, muv_ref, mul_ref, w0_ref, w2_ref, a0_ref, a2_ref, g2_ref,
     kk_ref, ka_ref, rk_ref, gng_ref, gnb_ref) = refs[:22]
    refs = refs[22:]
    if vres:
        vf_ref, v0_ref, v1_ref, v2_ref = refs[:4]
        refs = refs[4:]
    oa_out = refs[0]
    refs = refs[1:]
    if emit_v:
        v_out = refs[0]
        refs = refs[1:]
    h_ref, r_sc, lw_sc, k_sc, v_sc, kk_sc, b_sc, g_sc, bonus_sc, y_sc = refs

    first = pl.program_id(1) == 0

    @pl.when(first)
    def _():
        h_ref[...] = jnp.zeros_like(h_ref)

    keep_prev = jnp.where(first, 0.0, 1.0)
    row = lax.broadcasted_iota(jnp.int32, (rows, 1), 0)

    def shifted(z_ref, p_ref, mu_ref):
        z = z_ref[...]
        prev_row = p_ref[7:8, :] * keep_prev
        prev = jnp.where(row == 0, prev_row, pltpu.roll(z, 1, axis=0))
        return z + (prev - z) * mu_ref[...]

    r = shifted(zr_ref, pr_ref, mur_ref)
    k = shifted(zk_ref, pk_ref, muk_ref)
    v = shifted(zv_ref, pv_ref, muv_ref)
    low = shifted(zl_ref, plo_ref, mul_ref)

    if vres:
        mix = _dot(_dot(v.astype(BF16), v1_ref[...]).astype(BF16), v2_ref[...])
        v = v + (vf_ref[...] - v) * jax.nn.sigmoid(v0_ref[...] + mix)

    dw = w0_ref[...] + _dot(jnp.tanh(low).astype(BF16), w2_ref[...])
    lw_sc[...] = -math.exp(-0.5) * jax.nn.sigmoid(dw)
    a = jax.nn.sigmoid(a0_ref[...] + _dot(low.astype(BF16), a2_ref[...]))
    g_sc[...] = _dot(jax.nn.sigmoid(low).astype(BF16), g2_ref[...])

    kk = k * kk_ref[...]
    kk = kk * jnp.minimum(lax.rsqrt(_head_sum(kk * kk)), 1e12)
    k = k * (1.0 + (a - 1.0) * ka_ref[...])
    r_sc[...] = r
    k_sc[...] = k
    v_sc[...] = v
    if emit_v:
        v_out[...] = v
    kk_sc[...] = kk
    b_sc[...] = kk * a
    bonus_sc[...] = _head_sum(r * k * rk_ref[...]) * v

    _scan_tile((r_sc, lw_sc, k_sc, v_sc, kk_sc, b_sc), h_ref, y_sc, chunk)

    y = y_sc[...]
    yc = y - _head_sum(y) * (1.0 / HEAD_DIM)
    var = _head_sum(yc * yc) * (1.0 / HEAD_DIM)
    yn = yc * lax.rsqrt(var + GN_EPS) * gng_ref[...] + gnb_ref[...]
    oa_out[...] = ((yn + bonus_sc[...]) * g_sc[...]).astype(oa_out.dtype)


def _rwkv(z, batch, seq_len, mu_r, mu_k, mu_v, mu_l, w0, w2p, a0, a2p, g2p, k_k, k_a, r_k,
          gn_g, gn_b, vres, emit_v, rows=256, chunk=SCAN_CHUNK):
    n = z.shape[0]
    aw = A_WIDTH
    low_blk = (3 * aw + B_HEADS * HEAD_DIM + 2 * B_KV_HEADS * HEAD_DIM) // LOW_PAD
    nr = seq_len // rows
    pstep = rows // 8

    def cur(col):
        return lambda bi, j: (bi * nr + j, col)

    def prev_map(col):
        return lambda bi, j: (jnp.maximum((bi * nr + j) * pstep - 1, 0), col)

    row_vec = lambda w: pl.BlockSpec((1, w), lambda bi, j: (0, 0))
    full = lambda a: pl.BlockSpec(a.shape, lambda bi, j: (0, 0))
    in_specs = [
        pl.BlockSpec((rows, aw), cur(0)),
        pl.BlockSpec((rows, aw), cur(1)),
        pl.BlockSpec((rows, aw), cur(2)),
        pl.BlockSpec((rows, LOW_PAD), cur(low_blk)),
        pl.BlockSpec((8, aw), prev_map(0)),
        pl.BlockSpec((8, aw), prev_map(1)),
        pl.BlockSpec((8, aw), prev_map(2)),
        pl.BlockSpec((8, LOW_PAD), prev_map(low_blk)),
        row_vec(aw), row_vec(aw), row_vec(aw), row_vec(LOW_PAD),
        row_vec(aw), full(w2p), row_vec(aw), full(a2p), full(g2p),
        row_vec(aw), row_vec(aw), row_vec(aw), row_vec(aw), row_vec(aw),
    ]
    args = [z, z, z, z, z, z, z, z, mu_r, mu_k, mu_v, mu_l, w0, w2p, a0, a2p, g2p, k_k, k_a, r_k,
            gn_g, gn_b]
    if vres is not None:
        v_first, v0, v1p, v2p = vres
        in_specs += [pl.BlockSpec((rows, aw), cur(0)), row_vec(aw), full(v1p), full(v2p)]
        args += [v_first, v0, v1p, v2p]
    out_specs = [pl.BlockSpec((rows, aw), cur(0))]
    out_shape = [jax.ShapeDtypeStruct((n, aw), BF16)]
    if emit_v:
        out_specs.append(pl.BlockSpec((rows, aw), cur(0)))
        out_shape.append(jax.ShapeDtypeStruct((n, aw), F32))
    state = pltpu.VMEM((A_HEADS // 2, 2 * HEAD_DIM, 2 * HEAD_DIM), F32)
    return pl.pallas_call(
        functools.partial(_rwkv_kernel, rows=rows, chunk=chunk, vres=vres is not None, emit_v=emit_v),
        grid=(batch, nr),
        in_specs=in_specs,
        out_specs=out_specs,
        out_shape=out_shape,
        scratch_shapes=[state] + [pltpu.VMEM((rows, aw), F32)] * 9,
        compiler_params=_cparams(("parallel", "arbitrary")),
        name="rwkv",
    )(*args)


def _scan_tile(refs, h_ref, y_ref, chunk, group=8):
    c = chunk
    hd = HEAD_DIM
    assert c == hd, "the [power | inverse] lane layout below needs chunk == head width"
    pw = 2 * hd
    npair = A_HEADS // 2
    nchunk = refs[0].shape[0] // c
    pairs = [slice(p * pw, (p + 1) * pw) for p in range(npair)]
    every = [(ci, p) for ci in range(nchunk) for p in range(npair)]

    ri = lax.broadcasted_iota(jnp.int32, (c, c), 0)
    ci_ = lax.broadcasted_iota(jnp.int32, (c, c), 1)
    tri_ones = jnp.where(ri >= ci_, 1.0, 0.0).astype(BF16)
    row = lax.broadcasted_iota(jnp.int32, (c, pw), 0)
    lane = lax.broadcasted_iota(jnp.int32, (c, pw), 1)
    left = lane < hd
    col = jnp.where(left, lane, lane - hd)
    strict = row > col
    incl = row >= col
    eye_right = jnp.where((row == col) & (lane >= hd), 1.0, 0.0)
    left2 = lax.broadcasted_iota(jnp.int32, (2 * c, pw), 1) < hd
    kr = lax.broadcasted_iota(jnp.int32, (pw, pw), 0)
    kc = lax.broadcasted_iota(jnp.int32, (pw, pw), 1)
    same_head = (kr < hd) == (kc < hd)
    zeros_cv = jnp.zeros((c, pw), BF16)

    ar, bk, bke, vv, p_all = {}, {}, {}, {}, []
    for ci in range(nchunk):
        rs = pl.ds(ci * c, c)
        r, lw, k, v, kk, b = (ref[rs, :] for ref in refs)
        cum = _split_dot(lw, tri_ones, left=True)
        p_inv = jnp.exp(-cum)
        tail = cum[c - 1:c, :]
        p_tail = jnp.exp(tail - cum)
        p_all.append(jnp.exp(tail))
        a_t = (-kk * jnp.exp(cum - lw)).astype(BF16)
        r_t = (r * jnp.exp(cum)).astype(BF16)
        b_t = (b * p_inv).astype(BF16)
        k_t = (k * p_inv).astype(BF16)
        b_e = (b * p_tail).astype(BF16)
        k_e = (k * p_tail).astype(BF16)
        vb = v.astype(BF16)
        for p, s in enumerate(pairs):
            ar[ci, p] = jnp.concatenate([a_t[:, s], r_t[:, s]], axis=0)
            bk[ci, p] = jnp.concatenate([b_t[:, s], k_t[:, s]], axis=0)
            bke[ci, p] = jnp.concatenate([b_e[:, s], k_e[:, s]], axis=0)
            vv[ci, p] = vb[:, s]

    mb, lv, inv = {}, {}, {}
    for g in range(0, len(every), group):
        grp = every[g:g + group]
        sc = {i: _dot_nt(jnp.concatenate([jnp.where(left2, ar[i], 0), jnp.where(left2, 0, ar[i])], axis=0),
                         bk[i]) for i in grp}
        lt = {i: [jnp.where(strict, sc[i][q * 2 * c:q * 2 * c + c], 0.0) for q in range(2)] for i in grp}
        for i in grp:
            mb[i] = jnp.concatenate([jnp.where(incl, sc[i][q * 2 * c + c:(q + 1) * 2 * c], 0.0).astype(BF16)
                                     for q in range(2)], axis=0)
        lvf = {i: _dot(jnp.concatenate(lt[i], axis=0).astype(BF16),
                       jnp.concatenate([zeros_cv, vv[i]], axis=0)) for i in grp}
        for i in grp:
            lv[i] = jnp.where(left, lvf[i][:c], lvf[i][c:])
        x = {i: [jnp.where(left, lt[i][q], eye_right) for q in range(2)] for i in grp}
        for _ in range(int(math.log2(c))):
            z = {i: [_dot(t[:, :hd].astype(BF16), t.astype(BF16)) for t in x[i]] for i in grp}
            x = {i: [jnp.where(left, u, t + u) for t, u in zip(x[i], z[i])] for i in grp}
        for i in grp:
            inv[i] = jnp.where(left, pltpu.roll(x[i][0], hd, axis=1), x[i][1]).astype(BF16)

    h = [h_ref[p] for p in range(npair)]
    for ci in range(nchunk):
        here = [(ci, p) for p in range(npair)]
        ah = [_dot(ar[i], h[p].astype(BF16)) for p, i in enumerate(here)]
        rhs = [ah[p][:c] + lv[i] for p, i in enumerate(here)]
        rhs = [jnp.concatenate([jnp.where(left, t, 0.0), jnp.where(left, 0.0, t)], axis=0).astype(BF16)
               for t in rhs]
        uv = [jnp.concatenate([_dot(inv[i], rhs[p]).astype(BF16), vv[i]], axis=0)
              for p, i in enumerate(here)]
        yy = [_dot(mb[i], uv[p]) for p, i in enumerate(here)]
        for p in range(npair):
            y_ref[pl.ds(ci * c, c), pairs[p]] = ah[p][c:] + jnp.where(left, yy[p][:c], yy[p][c:])
        hn = [_dot_tn(bke[i], uv[p]) for p, i in enumerate(here)]
        h = [jnp.transpose(jnp.broadcast_to(p_all[ci][:, pairs[p]], (pw, pw))) * h[p]
             + jnp.where(same_head, hn[p], 0.0) for p in range(npair)]
    for p in range(npair):
        h_ref[p] = h[p]


def _swa_kernel(sink_ref, q_ref, kc_ref, kp_ref, vc_ref, vp_ref, o_ref):
    blk = WINDOW
    first_key = jnp.where(pl.program_id(1) > 0, 0, blk)
    kj = lax.broadcasted_iota(jnp.int32, (2 * blk, blk), 0)
    qi = lax.broadcasted_iota(jnp.int32, (2 * blk, blk), 1)
    dist = qi + blk - kj
    valid = (dist >= 0) & (dist < WINDOW) & (kj >= first_key)
    distf = dist.astype(F32)
    scale = HEAD_DIM ** -0.5
    kband = jnp.concatenate([kp_ref[...], kc_ref[...]], axis=0).astype(BF16)
    vband = jnp.concatenate([vp_ref[...], vc_ref[...]], axis=0).astype(BF16)
    q = q_ref[...].astype(BF16)
    heads = range(B_HEADS)
    head_cols = lambda t, i: t[:, i * HEAD_DIM:(i + 1) * HEAD_DIM]
    qk = [_dot_nt(head_cols(kband, h // B_GROUP), head_cols(q, h)) for h in heads]
    s = [jnp.where(valid, qk[h] * scale - 2.0 ** (-8.0 * (h + 1) / B_HEADS) * distf, -1e30) for h in heads]
    m = [jnp.maximum(jnp.max(s[h], axis=0, keepdims=True), sink_ref[h]) for h in heads]
    p = [jnp.exp(s[h] - m[h]) for h in heads]
    den = [jnp.sum(p[h], axis=0, keepdims=True) + jnp.exp(sink_ref[h] - m[h]) for h in heads]
    pv = [_dot_tn(head_cols(vband, h // B_GROUP), p[h].astype(BF16)) for h in heads]
    for h in heads:
        o_ref[h * HEAD_DIM:(h + 1) * HEAD_DIM, :] = (pv[h] / den[h]).astype(o_ref.dtype)


def _swa(z, sinks, batch, seq_len):
    n = z.shape[0]
    nb = seq_len // WINDOW
    qw = B_HEADS * HEAD_DIM
    kvw = B_KV_HEADS * HEAD_DIM
    q_blk = 3 * A_WIDTH // qw
    k_blk = (3 * A_WIDTH + qw) // kvw
    cur = lambda col: (lambda bi, j: (bi * nb + j, col))
    prev = lambda col: (lambda bi, j: (bi * nb + jnp.maximum(j - 1, 0), col))
    return pl.pallas_call(
        _swa_kernel,
        grid=(batch, nb),
        in_specs=[
            pl.BlockSpec(memory_space=pltpu.SMEM),
            pl.BlockSpec((WINDOW, qw), cur(q_blk)),
            pl.BlockSpec((WINDOW, kvw), cur(k_blk)),
            pl.BlockSpec((WINDOW, kvw), prev(k_blk)),
            pl.BlockSpec((WINDOW, kvw), cur(k_blk + 1)),
            pl.BlockSpec((WINDOW, kvw), prev(k_blk + 1)),
        ],
        out_specs=pl.BlockSpec((qw, WINDOW), lambda bi, j: (0, bi * nb + j)),
        out_shape=jax.ShapeDtypeStruct((qw, n), BF16),
        compiler_params=_cparams(("parallel", "arbitrary")),
        name="swa",
    )(sinks, z, z, z, z, z)


def _mix_out_kernel(x_ref, oa_ref, obt_ref, wa_ref, wb_ref, o_ref):
    o_ref[...] = x_ref[...] + _dot(oa_ref[...], wa_ref[...]) + _dot_tn(obt_ref[...], wb_ref[...])


def _mix_out(x, oa, obt, wa, wb, tm=512):
    n, d = x.shape
    aw = oa.shape[1]
    bw = obt.shape[0]
    rows = lambda w: pl.BlockSpec((tm, w), lambda i: (i, 0))
    return pl.pallas_call(
        _mix_out_kernel,
        grid=(n // tm,),
        in_specs=[rows(d), rows(aw), pl.BlockSpec((bw, tm), lambda i: (0, i)),
                  pl.BlockSpec((aw, d), lambda i: (0, 0)), pl.BlockSpec((bw, d), lambda i: (0, 0))],
        out_specs=rows(d),
        out_shape=jax.ShapeDtypeStruct((n, d), F32),
        compiler_params=_cparams(("parallel",)),
        name="mix_out",
    )(x, oa, obt, wa, wb)


def _sgu_kernel(x_ref, u_ref, v_ref, vg_ref, ws_ref, bs_ref, wo_ref, o_ref, gate_ref, *, tm):
    cw = C_CHUNK
    vn = _rms(v_ref[...], vg_ref[...]).astype(BF16)
    for ch in range(tm // C_CHUNK):
        rows = slice(ch * C_CHUNK, (ch + 1) * C_CHUNK)
        for grp in range(C_GROUPS):
            cols = slice(grp * cw, (grp + 1) * cw)
            vm = _dot(ws_ref[grp], vn[rows, cols]) + bs_ref[grp]
            gate_ref[rows, cols] = (u_ref[rows, cols] * vm).astype(BF16)
    o_ref[...] = x_ref[...] + _dot(gate_ref[...], wo_ref[...])


def _sgu(x, z, vn_g, ws, bs, wo, tm=256):
    n, d = x.shape
    cwid = wo.shape[0]
    return pl.pallas_call(
        functools.partial(_sgu_kernel, tm=tm),
        grid=(n // tm,),
        in_specs=[
            pl.BlockSpec((tm, d), lambda i: (i, 0)),
            pl.BlockSpec((tm, cwid), lambda i: (i, 0)),
            pl.BlockSpec((tm, cwid), lambda i: (i, 1)),
            pl.BlockSpec((1, cwid), lambda i: (0, 0)),
            pl.BlockSpec(ws.shape, lambda i: (0, 0, 0)),
            pl.BlockSpec(bs.shape, lambda i: (0, 0, 0)),
            pl.BlockSpec((cwid, d), lambda i: (0, 0)),
        ],
        out_specs=pl.BlockSpec((tm, d), lambda i: (i, 0)),
        out_shape=jax.ShapeDtypeStruct((n, d), F32),
        scratch_shapes=[pltpu.VMEM((tm, cwid), BF16)],
        compiler_params=_cparams(("parallel",)),
        name="sgu",
    )(x, z, z, vn_g.reshape(1, cwid), ws, bs, wo)


def _pad_rows(w, lo, total):
    return jnp.pad(w, ((lo, total - lo - w.shape[0]), (0, 0)))


def _even_layer(x, batch, seq_len, g, w_in, b_qkv, mu, w0, w2, a0, a2, g2, k_k, k_a, r_k,
                gn_g, gn_b, sinks, w_out, v_first, vres):
    aw = A_WIDTH
    a_cols = 3 * aw + DECAY_RANK + ICLR_RANK + GATE_RANK
    n_low = a_cols - 3 * aw
    b_cols = w_in.shape[1] - a_cols
    pad_low = LOW_PAD - n_low
    tail = (-(3 * aw + b_cols + LOW_PAD)) % 1024
    w_cat = jnp.concatenate([w_in[:, :3 * aw], w_in[:, a_cols:], w_in[:, 3 * aw:a_cols],
                             jnp.zeros((w_in.shape[0], pad_low + tail), F32)], axis=1).astype(BF16)
    b_cat = jnp.concatenate([jnp.zeros((3 * aw,), F32), b_qkv, jnp.zeros((LOW_PAD + tail,), F32)])
    z = _proj(x, g, w_cat, b_cat, gelu=False)

    mu_l = jnp.pad(mu[3 * aw:], (0, pad_low))
    row = lambda t: t.reshape(1, -1)
    w2p = _pad_rows(w2, 0, LOW_PAD).astype(BF16)
    a2p = _pad_rows(a2, DECAY_RANK, LOW_PAD).astype(BF16)
    g2p = _pad_rows(g2, DECAY_RANK + ICLR_RANK, LOW_PAD).astype(BF16)
    if vres is not None:
        v0, v1, v2 = vres
        rank = v1.shape[1]
        vres_args = (v_first, row(v0), jnp.pad(v1, ((0, 0), (0, VRES_PAD - rank))).astype(BF16),
                     _pad_rows(v2, 0, VRES_PAD).astype(BF16))
    else:
        vres_args = None
    outs = _rwkv(z, batch, seq_len, row(mu[:aw]), row(mu[aw:2 * aw]), row(mu[2 * aw:3 * aw]), row(mu_l),
                 row(w0), w2p, row(a0), a2p, g2p, row(k_k), row(k_a), row(r_k.reshape(-1)),
                 row(gn_g), row(gn_b), vres_args, emit_v=vres is None)
    oa, v = outs if vres is None else (outs[0], None)
    ob = _swa(z, sinks, batch, seq_len)
    wo = w_out.astype(BF16)
    x = _mix_out(x, oa, ob, wo[:aw], wo[aw:])
    return x, v


def _odd_layer(x, g, w_in, vn_g, w_s, b_s, w_out):
    z = _proj(x, g, w_in.astype(BF16), jnp.zeros((w_in.shape[1],), F32), gelu=True)
    ws = jnp.tril(w_s).astype(BF16)
    return _sgu(x, z, vn_g, ws, b_s[:, :, None], w_out.astype(BF16))


def kernel(x, norm_g, ffn_wg, ffn_wu, ffn_wd, e_w_in, e_b_qkv, e_mu, e_w0, e_w2, e_a0, e_a2, e_g2, e_k_k, e_k_a, e_r_k, e_gn_g, e_gn_b, e_sinks, e_w_out, vres_v0, vres_v1, vres_v2, o_w_in, o_vn_g, o_w_s, o_b_s, o_w_out, final_g):
    batch, seq_len, d = x.shape
    depth = norm_g.shape[0]
    x = x.reshape(batch * seq_len, d)
    fdim = ffn_wg.shape[-1]
    wg = ffn_wg.astype(BF16).reshape(depth * 2, d, fdim)
    wu = ffn_wu.astype(BF16).reshape(depth * 2, d, fdim)
    wd = ffn_wd.astype(BF16).reshape(depth * 2, fdim, d)
    v_first = None
    for layer in range(depth):
        x = _ffn(x, norm_g[layer, 0], wg, wu, wd, 2 * layer)
        if layer % 2 == 0:
            i = layer // 2
            vres = None if i == 0 else (vres_v0[i - 1], vres_v1[i - 1], vres_v2[i - 1])
            x, v_a = _even_layer(x, batch, seq_len, norm_g[layer, 1], e_w_in[i], e_b_qkv[i], e_mu[i],
                                 e_w0[i], e_w2[i], e_a0[i], e_a2[i], e_g2[i], e_k_k[i], e_k_a[i],
                                 e_r_k[i], e_gn_g[i], e_gn_b[i], e_sinks[i], e_w_out[i], v_first, vres)
            if i == 0:
                v_first = v_a
        else:
            j = layer // 2
            x = _odd_layer(x, norm_g[layer, 1], o_w_in[j], o_vn_g[j], o_w_s[j], o_b_s[j], o_w_out[j])
        x = _ffn(x, norm_g[layer, 2], wg, wu, wd, 2 * layer + 1,
                 g_out=final_g if layer == depth - 1 else None)
    return x.reshape(batch, seq_len, d)
```

```python
import functools
import math

import jax
import jax.numpy as jnp
from jax import lax
from jax.experimental import pallas as pl
from jax.experimental.pallas import tpu as pltpu

F32 = jnp.float32
BF16 = jnp.bfloat16

NORM_EPS = 1e-6
GN_EPS = 64e-5
HEAD_DIM = 64
A_HEADS = 16
A_WIDTH = A_HEADS * HEAD_DIM
B_HEADS = 16
B_KV_HEADS = 4
B_GROUP = B_HEADS // B_KV_HEADS
WINDOW = 128
DECAY_RANK = 64
ICLR_RANK = 64
GATE_RANK = 160
LOW_PAD = 384
VRES_PAD = 128
C_GROUPS = 16
C_CHUNK = 128
SCAN_CHUNK = 64

VMEM_LIMIT = 60 * 1024 * 1024


def _cparams(sem):
    return pltpu.CompilerParams(dimension_semantics=sem, vmem_limit_bytes=VMEM_LIMIT)


def _dot(a, b):
    return jnp.dot(a, b, preferred_element_type=F32)


def _dot_nt(a, b):
    return lax.dot_general(a, b, (((1,), (1,)), ((), ())), preferred_element_type=F32)


def _dot_tn(a, b):
    return lax.dot_general(a, b, (((0,), (0,)), ((), ())), preferred_element_type=F32)


def _split_dot(x, ones_bf16, left=False):
    x1 = x.astype(BF16)
    r1 = x - x1.astype(F32)
    x2 = r1.astype(BF16)
    x3 = (r1 - x2.astype(F32)).astype(BF16)
    if left:
        return _dot(ones_bf16, x1) + _dot(ones_bf16, x2) + _dot(ones_bf16, x3)
    return _dot(x1, ones_bf16) + _dot(x2, ones_bf16) + _dot(x3, ones_bf16)


def _rms(x, g):
    return x * lax.rsqrt(jnp.mean(x * x, axis=-1, keepdims=True) + NORM_EPS) * g


def _ffn_kernel(x_ref, g_ref, gout_ref, wg_ref, wu_ref, wd_ref, o_ref, h_ref):
    f = pl.program_id(1)

    def half_swiglu(h):
        a = _dot(h, wg_ref[...])
        b = _dot(h, wu_ref[...])
        p = (0.5 * a * jax.nn.sigmoid(a) * b).astype(BF16)
        return _dot(p, wd_ref[...])

    @pl.when(f == 0)
    def _():
        half = x_ref.shape[0] // 2
        for s in range(2):
            rows = pl.ds(s * half, half)
            x = x_ref[rows, :]
            h = _rms(x, g_ref[...]).astype(BF16)
            h_ref[rows, :] = h
            o_ref[rows, :] = x + half_swiglu(h)

    @pl.when(f > 0)
    def _():
        o_ref[...] += half_swiglu(h_ref[...])

    if gout_ref is not None:
        @pl.when(f == pl.num_programs(1) - 1)
        def _():
            o_ref[...] = _rms(o_ref[...], gout_ref[...])


def _ffn_kernel_plain(x_ref, g_ref, wg_ref, wu_ref, wd_ref, o_ref, h_ref):
    _ffn_kernel(x_ref, g_ref, None, wg_ref, wu_ref, wd_ref, o_ref, h_ref)


def _ffn(x, g, wg, wu, wd, idx, g_out=None, tm=1024, tf=512):
    n, d = x.shape
    fdim = wg.shape[2]
    vec = pl.BlockSpec((1, d), lambda i, f: (0, 0))
    in_specs = [pl.BlockSpec((tm, d), lambda i, f: (i, 0)), vec]
    args = [x, g.reshape(1, d)]
    if g_out is not None:
        in_specs.append(vec)
        args.append(g_out.reshape(1, d))
    in_specs += [
        pl.BlockSpec((None, d, tf), lambda i, f: (idx, 0, f)),
        pl.BlockSpec((None, d, tf), lambda i, f: (idx, 0, f)),
        pl.BlockSpec((None, tf, d), lambda i, f: (idx, f, 0)),
    ]
    return pl.pallas_call(
        _ffn_kernel if g_out is not None else _ffn_kernel_plain,
        grid=(n // tm, fdim // tf),
        in_specs=in_specs,
        out_specs=pl.BlockSpec((tm, d), lambda i, f: (i, 0)),
        out_shape=jax.ShapeDtypeStruct((n, d), F32),
        scratch_shapes=[pltpu.VMEM((tm, d), BF16)],
        compiler_params=_cparams(("parallel", "arbitrary")),
        name="ffn",
    )(*args, wg, wu, wd)


def _proj_kernel(x_ref, g_ref, w_ref, b_ref, o_ref, h_ref, *, gelu, tn):
    j = pl.program_id(1)

    cols = pl.ds(pl.multiple_of(j * tn, tn), tn)

    def project(h):
        z = _dot(h, w_ref[:, cols]) + b_ref[:, cols]
        if gelu:
            z = 0.5 * z * (1.0 + lax.erf(z * (2.0 ** -0.5)))
        return z

    @pl.when(j == 0)
    def _():
        half = x_ref.shape[0] // 2
        for s in range(2):
            rows = pl.ds(s * half, half)
            h = _rms(x_ref[rows, :], g_ref[...]).astype(BF16)
            h_ref[rows, :] = h
            o_ref[rows, :] = project(h)

    @pl.when(j > 0)
    def _():
        o_ref[...] = project(h_ref[...])


def _proj(x, g, w, b, gelu, tm=1024, tn=1024):
    n, d = x.shape
    cols = w.shape[1]
    resident = lambda shape: pl.BlockSpec(shape, lambda i, j: (0, 0), pipeline_mode=pl.Buffered(1))
    return pl.pallas_call(
        functools.partial(_proj_kernel, gelu=gelu, tn=tn),
        grid=(n // tm, cols // tn),
        in_specs=[
            pl.BlockSpec((tm, d), lambda i, j: (i, 0)),
            pl.BlockSpec((1, d), lambda i, j: (0, 0)),
            resident((d, cols)),
            resident((1, cols)),
        ],
        out_specs=pl.BlockSpec((tm, tn), lambda i, j: (i, j)),
        out_shape=jax.ShapeDtypeStruct((n, cols), F32),
        scratch_shapes=[pltpu.VMEM((tm, d), BF16)],
        compiler_params=_cparams(("parallel", "arbitrary")),
        name="proj_gelu" if gelu else "proj",
    )(x, g.reshape(1, d), w, b.reshape(1, cols))


MXU_TILE = 256


def _head_sum(x):
    rh = lax.broadcasted_iota(jnp.int32, (MXU_TILE, MXU_TILE), 0) // HEAD_DIM
    ch = lax.broadcasted_iota(jnp.int32, (MXU_TILE, MXU_TILE), 1) // HEAD_DIM
    ones = jnp.where(rh == ch, 1.0, 0.0).astype(BF16)
    hi = x.astype(BF16)
    lo = (x - hi.astype(F32)).astype(BF16)
    blocks = [slice(s, s + MXU_TILE) for s in range(0, x.shape[1], MXU_TILE)]
    return jnp.concatenate([_dot(hi[:, s], ones) + _dot(lo[:, s], ones) for s in blocks], axis=1)


def _rwkv_kernel(*refs, rows, chunk, vres, emit_v):
    (zr_ref, zk_ref, zv_ref, zl_ref, pr_ref, pk_ref, pv_ref, plo_ref,
     mur_ref, muk_ref, muv_ref, mul_ref, w0_ref, w2_ref, a0_ref, a2_ref, g2_ref,
     kk_ref, ka_ref, rk_ref, gng_ref, gnb_ref) = refs[:22]
    refs = refs[22:]
    if vres:
        vf_ref, v0_ref, v1_ref, v2_ref = refs[:4]
        refs = refs[4:]
    oa_out = refs[0]
    refs = refs[1:]
    if emit_v:
        v_out = refs[0]
        refs = refs[1:]
    h_ref, r_sc, lw_sc, k_sc, v_sc, kk_sc, b_sc, g_sc, bonus_sc, y_sc = refs

    first = pl.program_id(1) == 0

    @pl.when(first)
    def _():
        h_ref[...] = jnp.zeros_like(h_ref)

    keep_prev = jnp.where(first, 0.0, 1.0)
    row = lax.broadcasted_iota(jnp.int32, (rows, 1), 0)

    def shifted(z_ref, p_ref, mu_ref):
        z = z_ref[...]
        prev_row = p_ref[7:8, :] * keep_prev
        prev = jnp.where(row == 0, prev_row, pltpu.roll(z, 1, axis=0))
        return z + (prev - z) * mu_ref[...]

    r = shifted(zr_ref, pr_ref, mur_ref)
    k = shifted(zk_ref, pk_ref, muk_ref)
    v = shifted(zv_ref, pv_ref, muv_ref)
    low = shifted(zl_ref, plo_ref, mul_ref)

    if vres:
        mix = _dot(_dot(v.astype(BF16), v1_ref[...]).astype(BF16), v2_ref[...])
        v = v + (vf_ref[...] - v) * jax.nn.sigmoid(v0_ref[...] + mix)

    dw = w0_ref[...] + _dot(jnp.tanh(low).astype(BF16), w2_ref[...])
    lw_sc[...] = -math.exp(-0.5) * jax.nn.sigmoid(dw)
    a = jax.nn.sigmoid(a0_ref[...] + _dot(low.astype(BF16), a2_ref[...]))
    g_sc[...] = _dot(jax.nn.sigmoid(low).astype(BF16), g2_ref[...])

    kk = k * kk_ref[...]
    kk = kk * jnp.minimum(lax.rsqrt(_head_sum(kk * kk)), 1e12)
    k = k * (1.0 + (a - 1.0) * ka_ref[...])
    r_sc[...] = r
    k_sc[...] = k
    v_sc[...] = v
    if emit_v:
        v_out[...] = v
    kk_sc[...] = kk
    b_sc[...] = kk * a
    bonus_sc[...] = _head_sum(r * k * rk_ref[...]) * v

    _scan_tile((r_sc, lw_sc, k_sc, v_sc, kk_sc, b_sc), h_ref, y_sc, chunk)

    y = y_sc[...]
    yc = y - _head_sum(y) * (1.0 / HEAD_DIM)
    var = _head_sum(yc * yc) * (1.0 / HEAD_DIM)
    yn = yc * lax.rsqrt(var + GN_EPS) * gng_ref[...] + gnb_ref[...]
    oa_out[...] = ((yn + bonus_sc[...]) * g_sc[...]).astype(oa_out.dtype)


def _rwkv(z, batch, seq_len, mu_r, mu_k, mu_v, mu_l, w0, w2p, a0, a2p, g2p, k_k, k_a, r_k,
          gn_g, gn_b, vres, emit_v, rows=256, chunk=SCAN_CHUNK):
    n = z.shape[0]
    aw = A_WIDTH
    low_blk = (3 * aw + B_HEADS * HEAD_DIM + 2 * B_KV_HEADS * HEAD_DIM) // LOW_PAD
    nr = seq_len // rows
    pstep = rows // 8

    def cur(col):
        return lambda bi, j: (bi * nr + j, col)

    def prev_map(col):
        return lambda bi, j: (jnp.maximum((bi * nr + j) * pstep - 1, 0), col)

    row_vec = lambda w: pl.BlockSpec((1, w), lambda bi, j: (0, 0))
    full = lambda a: pl.BlockSpec(a.shape, lambda bi, j: (0, 0))
    in_specs = [
        pl.BlockSpec((rows, aw), cur(0)),
        pl.BlockSpec((rows, aw), cur(1)),
        pl.BlockSpec((rows, aw), cur(2)),
        pl.BlockSpec((rows, LOW_PAD), cur(low_blk)),
        pl.BlockSpec((8, aw), prev_map(0)),
        pl.BlockSpec((8, aw), prev_map(1)),
        pl.BlockSpec((8, aw), prev_map(2)),
        pl.BlockSpec((8, LOW_PAD), prev_map(low_blk)),
        row_vec(aw), row_vec(aw), row_vec(aw), row_vec(LOW_PAD),
        row_vec(aw), full(w2p), row_vec(aw), full(a2p), full(g2p),
        row_vec(aw), row_vec(aw), row_vec(aw), row_vec(aw), row_vec(aw),
    ]
    args = [z, z, z, z, z, z, z, z, mu_r, mu_k, mu_v, mu_l, w0, w2p, a0, a2p, g2p, k_k, k_a, r_k,
            gn_g, gn_b]
    if vres is not None:
        v_first, v0, v1p, v2p = vres
        in_specs += [pl.BlockSpec((rows, aw), cur(0)), row_vec(aw), full(v1p), full(v2p)]
        args += [v_first, v0, v1p, v2p]
    out_specs = [pl.BlockSpec((rows, aw), cur(0))]
    out_shape = [jax.ShapeDtypeStruct((n, aw), BF16)]
    if emit_v:
        out_specs.append(pl.BlockSpec((rows, aw), cur(0)))
        out_shape.append(jax.ShapeDtypeStruct((n, aw), F32))
    state = pltpu.VMEM((A_HEADS // 2, 2 * HEAD_DIM, 2 * HEAD_DIM), F32)
    return pl.pallas_call(
        functools.partial(_rwkv_kernel, rows=rows, chunk=chunk, vres=vres is not None, emit_v=emit_v),
        grid=(batch, nr),
        in_specs=in_specs,
        out_specs=out_specs,
        out_shape=out_shape,
        scratch_shapes=[state] + [pltpu.VMEM((rows, aw), F32)] * 9,
        compiler_params=_cparams(("parallel", "arbitrary")),
        name="rwkv",
    )(*args)


def _scan_tile(refs, h_ref, y_ref, chunk, group=16):
    c = chunk
    hd = HEAD_DIM
    assert c == hd, "the [power | inverse] lane layout below needs chunk == head width"
    pw = 2 * hd
    npair = A_HEADS // 2
    nchunk = refs[0].shape[0] // c
    pairs = [slice(p * pw, (p + 1) * pw) for p in range(npair)]
    every = [(ci, p) for ci in range(nchunk) for p in range(npair)]

    ri = lax.broadcasted_iota(jnp.int32, (c, c), 0)
    ci_ = lax.broadcasted_iota(jnp.int32, (c, c), 1)
    tri_ones = jnp.where(ri >= ci_, 1.0, 0.0).astype(BF16)
    row = lax.broadcasted_iota(jnp.int32, (c, pw), 0)
    lane = lax.broadcasted_iota(jnp.int32, (c, pw), 1)
    left = lane < hd
    col = jnp.where(left, lane, lane - hd)
    strict = row > col
    incl = row >= col
    eye_right = jnp.where((row == col) & (lane >= hd), 1.0, 0.0)
    left2 = lax.broadcasted_iota(jnp.int32, (2 * c, pw), 1) < hd
    kr = lax.broadcasted_iota(jnp.int32, (pw, pw), 0)
    kc = lax.broadcasted_iota(jnp.int32, (pw, pw), 1)
    same_head = (kr < hd) == (kc < hd)
    zeros_cv = jnp.zeros((c, pw), BF16)

    ar, bk, bke, vv, p_all = {}, {}, {}, {}, []
    for ci in range(nchunk):
        rs = pl.ds(ci * c, c)
        r, lw, k, v, kk, b = (ref[rs, :] for ref in refs)
        cum = _split_dot(lw, tri_ones, left=True)
        p_inv = jnp.exp(-cum)
        tail = cum[c - 1:c, :]
        p_tail = jnp.exp(tail - cum)
        p_all.append(jnp.exp(tail))
        a_t = (-kk * jnp.exp(cum - lw)).astype(BF16)
        r_t = (r * jnp.exp(cum)).astype(BF16)
        b_t = (b * p_inv).astype(BF16)
        k_t = (k * p_inv).astype(BF16)
        b_e = (b * p_tail).astype(BF16)
        k_e = (k * p_tail).astype(BF16)
        vb = v.astype(BF16)
        for p, s in enumerate(pairs):
            ar[ci, p] = jnp.concatenate([a_t[:, s], r_t[:, s]], axis=0)
            bk[ci, p] = jnp.concatenate([b_t[:, s], k_t[:, s]], axis=0)
            bke[ci, p] = jnp.concatenate([b_e[:, s], k_e[:, s]], axis=0)
            vv[ci, p] = vb[:, s]

    mb, lv, inv = {}, {}, {}
    for g in range(0, len(every), group):
        grp = every[g:g + group]
        sc = {i: _dot_nt(jnp.concatenate([jnp.where(left2, ar[i], 0), jnp.where(left2, 0, ar[i])], axis=0),
                         bk[i]) for i in grp}
        lt = {i: [jnp.where(strict, sc[i][q * 2 * c:q * 2 * c + c], 0.0) for q in range(2)] for i in grp}
        for i in grp:
            mb[i] = jnp.concatenate([jnp.where(incl, sc[i][q * 2 * c + c:(q + 1) * 2 * c], 0.0).astype(BF16)
                                     for q in range(2)], axis=0)
        lvf = {i: _dot(jnp.concatenate(lt[i], axis=0).astype(BF16),
                       jnp.concatenate([zeros_cv, vv[i]], axis=0)) for i in grp}
        for i in grp:
            lv[i] = jnp.where(left, lvf[i][:c], lvf[i][c:])
        x = {i: [jnp.where(left, lt[i][q], eye_right) for q in range(2)] for i in grp}
        for _ in range(int(math.log2(c))):
            z = {i: [_dot(t[:, :hd].astype(BF16), t.astype(BF16)) for t in x[i]] for i in grp}
            x = {i: [jnp.where(left, u, t + u) for t, u in zip(x[i], z[i])] for i in grp}
        for i in grp:
            inv[i] = jnp.where(left, pltpu.roll(x[i][0], hd, axis=1), x[i][1]).astype(BF16)

    h = [h_ref[p] for p in range(npair)]
    for ci in range(nchunk):
        here = [(ci, p) for p in range(npair)]
        ah = [_dot(ar[i], h[p].astype(BF16)) for p, i in enumerate(here)]
        rhs = [ah[p][:c] + lv[i] for p, i in enumerate(here)]
        rhs = [jnp.concatenate([jnp.where(left, t, 0.0), jnp.where(left, 0.0, t)], axis=0).astype(BF16)
               for t in rhs]
        uv = [jnp.concatenate([_dot(inv[i], rhs[p]).astype(BF16), vv[i]], axis=0)
              for p, i in enumerate(here)]
        yy = [_dot(mb[i], uv[p]) for p, i in enumerate(here)]
        for p in range(npair):
            y_ref[pl.ds(ci * c, c), pairs[p]] = ah[p][c:] + jnp.where(left, yy[p][:c], yy[p][c:])
        hn = [_dot_tn(bke[i], uv[p]) for p, i in enumerate(here)]
        h = [jnp.transpose(jnp.broadcast_to(p_all[ci][:, pairs[p]], (pw, pw))) * h[p]
             + jnp.where(same_head, hn[p], 0.0) for p in range(npair)]
    for p in range(npair):
        h_ref[p] = h[p]


def _swa_kernel(sink_ref, q_ref, kc_ref, kp_ref, vc_ref, vp_ref, o_ref):
    blk = WINDOW
    first_key = jnp.where(pl.program_id(1) > 0, 0, blk)
    kj = lax.broadcasted_iota(jnp.int32, (2 * blk, blk), 0)
    qi = lax.broadcasted_iota(jnp.int32, (2 * blk, blk), 1)
    dist = qi + blk - kj
    valid = (dist >= 0) & (dist < WINDOW) & (kj >= first_key)
    neg_dist = jnp.where(valid, -dist.astype(F32), -1e32)
    scale = HEAD_DIM ** -0.5
    assert math.frexp(scale)[0] == 0.5, "a power-of-two scale commutes with the bf16 rounding of q"
    kband = jnp.concatenate([kp_ref[...], kc_ref[...]], axis=0).astype(BF16)
    vband = jnp.concatenate([vp_ref[...], vc_ref[...]], axis=0).astype(BF16)
    q = (q_ref[...] * scale).astype(BF16)
    heads = range(B_HEADS)
    head_cols = lambda t, i: t[:, i * HEAD_DIM:(i + 1) * HEAD_DIM]
    qk = [_dot_nt(head_cols(kband, h // B_GROUP), head_cols(q, h)) for h in heads]
    s = [qk[h] + 2.0 ** (-8.0 * (h + 1) / B_HEADS) * neg_dist for h in heads]
    m = [jnp.maximum(jnp.max(s[h], axis=0, keepdims=True), sink_ref[h]) for h in heads]
    p = [jnp.exp(s[h] - m[h]) for h in heads]
    den = [jnp.sum(p[h], axis=0, keepdims=True) + jnp.exp(sink_ref[h] - m[h]) for h in heads]
    pv = [_dot_tn(head_cols(vband, h // B_GROUP), p[h].astype(BF16)) for h in heads]
    for h in heads:
        o_ref[h * HEAD_DIM:(h + 1) * HEAD_DIM, :] = (pv[h] / den[h]).astype(o_ref.dtype)


def _swa(z, sinks, batch, seq_len):
    n = z.shape[0]
    nb = seq_len // WINDOW
    qw = B_HEADS * HEAD_DIM
    kvw = B_KV_HEADS * HEAD_DIM
    q_blk = 3 * A_WIDTH // qw
    k_blk = (3 * A_WIDTH + qw) // kvw
    cur = lambda col: (lambda bi, j: (bi * nb + j, col))
    prev = lambda col: (lambda bi, j: (bi * nb + jnp.maximum(j - 1, 0), col))
    return pl.pallas_call(
        _swa_kernel,
        grid=(batch, nb),
        in_specs=[
            pl.BlockSpec(memory_space=pltpu.SMEM),
            pl.BlockSpec((WINDOW, qw), cur(q_blk)),
            pl.BlockSpec((WINDOW, kvw), cur(k_blk)),
            pl.BlockSpec((WINDOW, kvw), prev(k_blk)),
            pl.BlockSpec((WINDOW, kvw), cur(k_blk + 1)),
            pl.BlockSpec((WINDOW, kvw), prev(k_blk + 1)),
        ],
        out_specs=pl.BlockSpec((qw, WINDOW), lambda bi, j: (0, bi * nb + j)),
        out_shape=jax.ShapeDtypeStruct((qw, n), BF16),
        compiler_params=_cparams(("parallel", "arbitrary")),
        name="swa",
    )(sinks, z, z, z, z, z)


def _mix_out_kernel(x_ref, oa_ref, obt_ref, wa_ref, wb_ref, o_ref):
    o_ref[...] = x_ref[...] + _dot(oa_ref[...], wa_ref[...]) + _dot_tn(obt_ref[...], wb_ref[...])


def _mix_out(x, oa, obt, wa, wb, tm=512):
    n, d = x.shape
    aw = oa.shape[1]
    bw = obt.shape[0]
    rows = lambda w: pl.BlockSpec((tm, w), lambda i: (i, 0))
    return pl.pallas_call(
        _mix_out_kernel,
        grid=(n // tm,),
        in_specs=[rows(d), rows(aw), pl.BlockSpec((bw, tm), lambda i: (0, i)),
                  pl.BlockSpec((aw, d), lambda i: (0, 0)), pl.BlockSpec((bw, d), lambda i: (0, 0))],
        out_specs=rows(d),
        out_shape=jax.ShapeDtypeStruct((n, d), F32),
        compiler_params=_cparams(("parallel",)),
        name="mix_out",
    )(x, oa, obt, wa, wb)


def _sgu_kernel(x_ref, u_ref, v_ref, vg_ref, ws_ref, bs_ref, wo_ref, o_ref, gate_ref, *, tm):
    cw = C_CHUNK
    vn = _rms(v_ref[...], vg_ref[...]).astype(BF16)
    for ch in range(tm // C_CHUNK):
        rows = slice(ch * C_CHUNK, (ch + 1) * C_CHUNK)
        for grp in range(C_GROUPS):
            cols = slice(grp * cw, (grp + 1) * cw)
            vm = _dot(ws_ref[grp], vn[rows, cols]) + bs_ref[grp]
            gate_ref[rows, cols] = (u_ref[rows, cols] * vm).astype(BF16)
    o_ref[...] = x_ref[...] + _dot(gate_ref[...], wo_ref[...])


def _sgu(x, z, vn_g, ws, bs, wo, tm=512):
    n, d = x.shape
    cwid = wo.shape[0]
    return pl.pallas_call(
        functools.partial(_sgu_kernel, tm=tm),
        grid=(n // tm,),
        in_specs=[
            pl.BlockSpec((tm, d), lambda i: (i, 0)),
            pl.BlockSpec((tm, cwid), lambda i: (i, 0)),
            pl.BlockSpec((tm, cwid), lambda i: (i, 1)),
            pl.BlockSpec((1, cwid), lambda i: (0, 0)),
            pl.BlockSpec(ws.shape, lambda i: (0, 0, 0)),
            pl.BlockSpec(bs.shape, lambda i: (0, 0, 0)),
            pl.BlockSpec((cwid, d), lambda i: (0, 0), pipeline_mode=pl.Buffered(1)),
        ],
        out_specs=pl.BlockSpec((tm, d), lambda i: (i, 0)),
        out_shape=jax.ShapeDtypeStruct((n, d), F32),
        scratch_shapes=[pltpu.VMEM((tm, cwid), BF16)],
        compiler_params=_cparams(("parallel",)),
        name="sgu",
    )(x, z, z, vn_g.reshape(1, cwid), ws, bs, wo)


def _pad_rows(w, lo, total):
    return jnp.pad(w, ((lo, total - lo - w.shape[0]), (0, 0)))


def _even_layer(x, batch, seq_len, g, w_in, b_qkv, mu, w0, w2, a0, a2, g2, k_k, k_a, r_k,
                gn_g, gn_b, sinks, w_out, v_first, vres):
    aw = A_WIDTH
    a_cols = 3 * aw + DECAY_RANK + ICLR_RANK + GATE_RANK
    n_low = a_cols - 3 * aw
    b_cols = w_in.shape[1] - a_cols
    pad_low = LOW_PAD - n_low
    tail = (-(3 * aw + b_cols + LOW_PAD)) % 1024
    w_cat = jnp.concatenate([w_in[:, :3 * aw], w_in[:, a_cols:], w_in[:, 3 * aw:a_cols],
                             jnp.zeros((w_in.shape[0], pad_low + tail), F32)], axis=1).astype(BF16)
    b_cat = jnp.concatenate([jnp.zeros((3 * aw,), F32), b_qkv, jnp.zeros((LOW_PAD + tail,), F32)])
    z = _proj(x, g, w_cat, b_cat, gelu=False)

    mu_l = jnp.pad(mu[3 * aw:], (0, pad_low))
    row = lambda t: t.reshape(1, -1)
    w2p = _pad_rows(w2, 0, LOW_PAD).astype(BF16)
    a2p = _pad_rows(a2, DECAY_RANK, LOW_PAD).astype(BF16)
    g2p = _pad_rows(g2, DECAY_RANK + ICLR_RANK, LOW_PAD).astype(BF16)
    if vres is not None:
        v0, v1, v2 = vres
        rank = v1.shape[1]
        vres_args = (v_first, row(v0), jnp.pad(v1, ((0, 0), (0, VRES_PAD - rank))).astype(BF16),
                     _pad_rows(v2, 0, VRES_PAD).astype(BF16))
    else:
        vres_args = None
    outs = _rwkv(z, batch, seq_len, row(mu[:aw]), row(mu[aw:2 * aw]), row(mu[2 * aw:3 * aw]), row(mu_l),
                 row(w0), w2p, row(a0), a2p, g2p, row(k_k), row(k_a), row(r_k.reshape(-1)),
                 row(gn_g), row(gn_b), vres_args, emit_v=vres is None)
    oa, v = outs if vres is None else (outs[0], None)
    ob = _swa(z, sinks, batch, seq_len)
    wo = w_out.astype(BF16)
    x = _mix_out(x, oa, ob, wo[:aw], wo[aw:])
    return x, v


def _odd_layer(x, g, w_in, vn_g, w_s, b_s, w_out):
    z = _proj(x, g, w_in.astype(BF16), jnp.zeros((w_in.shape[1],), F32), gelu=True)
    ws = jnp.tril(w_s).astype(BF16)
    return _sgu(x, z, vn_g, ws, b_s[:, :, None], w_out.astype(BF16))


def kernel(x, norm_g, ffn_wg, ffn_wu, ffn_wd, e_w_in, e_b_qkv, e_mu, e_w0, e_w2, e_a0, e_a2, e_g2, e_k_k, e_k_a, e_r_k, e_gn_g, e_gn_b, e_sinks, e_w_out, vres_v0, vres_v1, vres_v2, o_w_in, o_vn_g, o_w_s, o_b_s, o_w_out, final_g):
    batch, seq_len, d = x.shape
    depth = norm_g.shape[0]
    x = x.reshape(batch * seq_len, d)
    fdim = ffn_wg.shape[-1]
    wg = ffn_wg.astype(BF16).reshape(depth * 2, d, fdim)
    wu = ffn_wu.astype(BF16).reshape(depth * 2, d, fdim)
    wd = ffn_wd.astype(BF16).reshape(depth * 2, fdim, d)
    v_first = None
    for layer in range(depth):
        x = _ffn(x, norm_g[layer, 0], wg, wu, wd, 2 * layer)
        if layer % 2 == 0:
            i = layer // 2
            vres = None if i == 0 else (vres_v0[i - 1], vres_v1[i - 1], vres_v2[i - 1])
            x, v_a = _even_layer(x, batch, seq_len, norm_g[layer, 1], e_w_in[i], e_b_qkv[i], e_mu[i],
                                 e_w0[i], e_w2[i], e_a0[i], e_a2[i], e_g2[i], e_k_k[i], e_k_a[i],
                                 e_r_k[i], e_gn_g[i], e_gn_b[i], e_sinks[i], e_w_out[i], v_first, vres)
            if i == 0:
                v_first = v_a
        else:
            j = layer // 2
            x = _odd_layer(x, norm_g[layer, 1], o_w_in[j], o_vn_g[j], o_w_s[j], o_b_s[j], o_w_out[j])
        x = _ffn(x, norm_g[layer, 2], wg, wu, wd, 2 * layer + 1,
                 g_out=final_g if layer == depth - 1 else None)
    return x.reshape(batch, seq_len, d)
```

```python
import functools
import math

import jax
import jax.numpy as jnp
from jax import lax
from jax.experimental import pallas as pl
from jax.experimental.pallas import tpu as pltpu

F32 = jnp.float32
BF16 = jnp.bfloat16

NORM_EPS = 1e-6
GN_EPS = 64e-5
HEAD_DIM = 64
A_HEADS = 16
A_WIDTH = A_HEADS * HEAD_DIM
B_HEADS = 16
B_KV_HEADS = 4
B_GROUP = B_HEADS // B_KV_HEADS
WINDOW = 128
DECAY_RANK = 64
ICLR_RANK = 64
GATE_RANK = 160
LOW_PAD = 384
VRES_PAD = 128
C_GROUPS = 16
C_CHUNK = 128
SCAN_CHUNK = 64

VMEM_LIMIT = 60 * 1024 * 1024


def _cparams(sem):
    return pltpu.CompilerParams(dimension_semantics=sem, vmem_limit_bytes=VMEM_LIMIT)


def _dot(a, b):
    return jnp.dot(a, b, preferred_element_type=F32)


def _dot_nt(a, b):
    return lax.dot_general(a, b, (((1,), (1,)), ((), ())), preferred_element_type=F32)


def _dot_tn(a, b):
    return lax.dot_general(a, b, (((0,), (0,)), ((), ())), preferred_element_type=F32)


def _split_dot(x, ones_bf16, left=False):
    x1 = x.astype(BF16)
    r1 = x - x1.astype(F32)
    x2 = r1.astype(BF16)
    x3 = (r1 - x2.astype(F32)).astype(BF16)
    if left:
        return _dot(ones_bf16, x1) + _dot(ones_bf16, x2) + _dot(ones_bf16, x3)
    return _dot(x1, ones_bf16) + _dot(x2, ones_bf16) + _dot(x3, ones_bf16)


def _rms(x, g):
    return x * lax.rsqrt(jnp.mean(x * x, axis=-1, keepdims=True) + NORM_EPS) * g


def _ffn_kernel(*refs, final_norm, cast_next):
    x_ref, g_ref = refs[:2]
    refs = refs[2:]
    if final_norm:
        gout_ref = refs[0]
        refs = refs[1:]
    wg_ref, wu_ref, wd_ref = refs[:3]
    refs = refs[3:]
    if cast_next:
        next_f32 = refs[:3]
        refs = refs[3:]
    o_ref = refs[0]
    refs = refs[1:]
    if cast_next:
        next_bf16 = refs[:3]
        refs = refs[3:]
    (h_ref,) = refs
    f = pl.program_id(1)

    def half_swiglu(h):
        a = _dot(h, wg_ref[...])
        b = _dot(h, wu_ref[...])
        p = (0.5 * a * jax.nn.sigmoid(a) * b).astype(BF16)
        return _dot(p, wd_ref[...])

    def cast_slice():
        if cast_next:
            for src, dst in zip(next_f32, next_bf16):
                dst[...] = src[...].astype(BF16)

    @pl.when(f == 0)
    def _():
        cast_slice()
        half = x_ref.shape[0] // 2
        for s in range(2):
            rows = pl.ds(s * half, half)
            x = x_ref[rows, :]
            h = _rms(x, g_ref[...]).astype(BF16)
            h_ref[rows, :] = h
            o_ref[rows, :] = x + half_swiglu(h)

    @pl.when(f > 0)
    def _():
        cast_slice()
        o_ref[...] += half_swiglu(h_ref[...])

    if final_norm:
        @pl.when(f == pl.num_programs(1) - 1)
        def _():
            o_ref[...] = _rms(o_ref[...], gout_ref[...])


def _ffn(x, g, wg, wu, wd, g_out=None, cast_next=None, tm=1024, tf=512):
    n, d = x.shape
    fdim = wg.shape[1]
    ni, nf = n // tm, fdim // tf
    vec = pl.BlockSpec((1, d), lambda i, f: (0, 0))
    in_specs = [pl.BlockSpec((tm, d), lambda i, f: (i, 0)), vec]
    args = [x, g.reshape(1, d)]
    if g_out is not None:
        in_specs.append(vec)
        args.append(g_out.reshape(1, d))
    in_specs += [
        pl.BlockSpec((d, tf), lambda i, f: (0, f)),
        pl.BlockSpec((d, tf), lambda i, f: (0, f)),
        pl.BlockSpec((tf, d), lambda i, f: (f, 0)),
    ]
    args += [wg, wu, wd]
    out_specs = [pl.BlockSpec((tm, d), lambda i, f: (i, 0))]
    out_shape = [jax.ShapeDtypeStruct((n, d), F32)]
    if cast_next is not None:
        wg4, wu4, wd4, l, j = cast_next
        up_rows, down_rows = d // ni, fdim // (ni * nf)
        assert up_rows * ni == d and down_rows * ni * nf == fdim and up_rows % 16 == 0 and down_rows % 16 == 0
        in_specs += [
            pl.BlockSpec((None, None, up_rows, tf), lambda i, f: (l, j, i, f)),
            pl.BlockSpec((None, None, up_rows, tf), lambda i, f: (l, j, i, f)),
            pl.BlockSpec((None, None, down_rows, d), lambda i, f: (l, j, i * nf + f, 0)),
        ]
        args += [wg4, wu4, wd4]
        out_specs += [
            pl.BlockSpec((up_rows, tf), lambda i, f: (i, f)),
            pl.BlockSpec((up_rows, tf), lambda i, f: (i, f)),
            pl.BlockSpec((down_rows, d), lambda i, f: (i * nf + f, 0)),
        ]
        out_shape += [jax.ShapeDtypeStruct((d, fdim), BF16), jax.ShapeDtypeStruct((d, fdim), BF16),
                      jax.ShapeDtypeStruct((fdim, d), BF16)]
    outs = pl.pallas_call(
        functools.partial(_ffn_kernel, final_norm=g_out is not None, cast_next=cast_next is not None),
        grid=(ni, nf),
        in_specs=in_specs,
        out_specs=out_specs,
        out_shape=out_shape,
        scratch_shapes=[pltpu.VMEM((tm, d), BF16)],
        compiler_params=_cparams(("parallel", "arbitrary")),
        name="ffn",
    )(*args)
    return outs[0], tuple(outs[1:])


def _proj_kernel(x_ref, g_ref, w_ref, b_ref, o_ref, h_ref, *, gelu, tn):
    j = pl.program_id(1)

    cols = pl.ds(pl.multiple_of(j * tn, tn), tn)

    def project(h):
        z = _dot(h, w_ref[:, cols]) + b_ref[:, cols]
        if gelu:
            z = 0.5 * z * (1.0 + lax.erf(z * (2.0 ** -0.5)))
        return z

    @pl.when(j == 0)
    def _():
        half = x_ref.shape[0] // 2
        for s in range(2):
            rows = pl.ds(s * half, half)
            h = _rms(x_ref[rows, :], g_ref[...]).astype(BF16)
            h_ref[rows, :] = h
            o_ref[rows, :] = project(h)

    @pl.when(j > 0)
    def _():
        o_ref[...] = project(h_ref[...])


def _proj(x, g, w, b, gelu, tm=1024, tn=1024):
    n, d = x.shape
    cols = w.shape[1]
    resident = lambda shape: pl.BlockSpec(shape, lambda i, j: (0, 0), pipeline_mode=pl.Buffered(1))
    return pl.pallas_call(
        functools.partial(_proj_kernel, gelu=gelu, tn=tn),
        grid=(n // tm, cols // tn),
        in_specs=[
            pl.BlockSpec((tm, d), lambda i, j: (i, 0)),
            pl.BlockSpec((1, d), lambda i, j: (0, 0)),
            resident((d, cols)),
            resident((1, cols)),
        ],
        out_specs=pl.BlockSpec((tm, tn), lambda i, j: (i, j)),
        out_shape=jax.ShapeDtypeStruct((n, cols), F32),
        scratch_shapes=[pltpu.VMEM((tm, d), BF16)],
        compiler_params=_cparams(("parallel", "arbitrary")),
        name="proj_gelu" if gelu else "proj",
    )(x, g.reshape(1, d), w, b.reshape(1, cols))


MXU_TILE = 256


def _head_sum(x):
    rh = lax.broadcasted_iota(jnp.int32, (MXU_TILE, MXU_TILE), 0) // HEAD_DIM
    ch = lax.broadcasted_iota(jnp.int32, (MXU_TILE, MXU_TILE), 1) // HEAD_DIM
    ones = jnp.where(rh == ch, 1.0, 0.0).astype(BF16)
    hi = x.astype(BF16)
    lo = (x - hi.astype(F32)).astype(BF16)
    blocks = [slice(s, s + MXU_TILE) for s in range(0, x.shape[1], MXU_TILE)]
    return jnp.concatenate([_dot(hi[:, s], ones) + _dot(lo[:, s], ones) for s in blocks], axis=1)


def _rwkv_kernel(*refs, rows, chunk, vres, emit_v):
    (zr_ref, zk_ref, zv_ref, zl_ref, pr_ref, pk_ref, pv_ref, plo_ref,
     mur_ref, muk_ref, muv_ref, mul_ref, w0_ref, w2_ref, a0_ref, a2_ref, g2_ref,
     kk_ref, ka_ref, rk_ref, gng_ref, gnb_ref) = refs[:22]
    refs = refs[22:]
    if vres:
        vf_ref, v0_ref, v1_ref, v2_ref = refs[:4]
        refs = refs[4:]
    oa_out = refs[0]
    refs = refs[1:]
    if emit_v:
        v_out = refs[0]
        refs = refs[1:]
    h_ref, r_sc, lw_sc, k_sc, v_sc, kk_sc, b_sc, g_sc, bonus_sc, y_sc = refs

    first = pl.program_id(1) == 0

    @pl.when(first)
    def _():
        h_ref[...] = jnp.zeros_like(h_ref)

    keep_prev = jnp.where(first, 0.0, 1.0)
    row = lax.broadcasted_iota(jnp.int32, (rows, 1), 0)

    def shifted(z_ref, p_ref, mu_ref):
        z = z_ref[...]
        prev_row = p_ref[7:8, :] * keep_prev
        prev = jnp.where(row == 0, prev_row, pltpu.roll(z, 1, axis=0))
        return z + (prev - z) * mu_ref[...]

    r = shifted(zr_ref, pr_ref, mur_ref)
    k = shifted(zk_ref, pk_ref, muk_ref)
    v = shifted(zv_ref, pv_ref, muv_ref)
    low = shifted(zl_ref, plo_ref, mul_ref)

    if vres:
        mix = _dot(_dot(v.astype(BF16), v1_ref[...]).astype(BF16), v2_ref[...])
        v = v + (vf_ref[...] - v) * jax.nn.sigmoid(v0_ref[...] + mix)

    dw = w0_ref[...] + _dot(jnp.tanh(low).astype(BF16), w2_ref[...])
    lw_sc[...] = -math.exp(-0.5) * jax.nn.sigmoid(dw)
    a = jax.nn.sigmoid(a0_ref[...] + _dot(low.astype(BF16), a2_ref[...]))
    g_sc[...] = _dot(jax.nn.sigmoid(low).astype(BF16), g2_ref[...])

    kk = k * kk_ref[...]
    kk = kk * jnp.minimum(lax.rsqrt(_head_sum(kk * kk)), 1e12)
    k = k * (1.0 + (a - 1.0) * ka_ref[...])
    r_sc[...] = r
    k_sc[...] = k
    v_sc[...] = v
    if emit_v:
        v_out[...] = v
    kk_sc[...] = kk
    b_sc[...] = kk * a
    bonus_sc[...] = _head_sum(r * k * rk_ref[...]) * v

    _scan_tile((r_sc, lw_sc, k_sc, v_sc, kk_sc, b_sc), h_ref, y_sc, chunk)

    y = y_sc[...]
    yc = y - _head_sum(y) * (1.0 / HEAD_DIM)
    var = _head_sum(yc * yc) * (1.0 / HEAD_DIM)
    yn = yc * lax.rsqrt(var + GN_EPS) * gng_ref[...] + gnb_ref[...]
    oa_out[...] = ((yn + bonus_sc[...]) * g_sc[...]).astype(oa_out.dtype)


def _rwkv(z, batch, seq_len, mu_r, mu_k, mu_v, mu_l, w0, w2p, a0, a2p, g2p, k_k, k_a, r_k,
          gn_g, gn_b, vres, emit_v, rows=256, chunk=SCAN_CHUNK):
    n = z.shape[0]
    aw = A_WIDTH
    low_blk = (3 * aw + B_HEADS * HEAD_DIM + 2 * B_KV_HEADS * HEAD_DIM) // LOW_PAD
    nr = seq_len // rows
    pstep = rows // 8

    def cur(col):
        return lambda bi, j: (bi * nr + j, col)

    def prev_map(col):
        return lambda bi, j: (jnp.maximum((bi * nr + j) * pstep - 1, 0), col)

    row_vec = lambda w: pl.BlockSpec((1, w), lambda bi, j: (0, 0))
    full = lambda a: pl.BlockSpec(a.shape, lambda bi, j: (0, 0))
    in_specs = [
        pl.BlockSpec((rows, aw), cur(0)),
        pl.BlockSpec((rows, aw), cur(1)),
        pl.BlockSpec((rows, aw), cur(2)),
        pl.BlockSpec((rows, LOW_PAD), cur(low_blk)),
        pl.BlockSpec((8, aw), prev_map(0)),
        pl.BlockSpec((8, aw), prev_map(1)),
        pl.BlockSpec((8, aw), prev_map(2)),
        pl.BlockSpec((8, LOW_PAD), prev_map(low_blk)),
        row_vec(aw), row_vec(aw), row_vec(aw), row_vec(LOW_PAD),
        row_vec(aw), full(w2p), row_vec(aw), full(a2p), full(g2p),
        row_vec(aw), row_vec(aw), row_vec(aw), row_vec(aw), row_vec(aw),
    ]
    args = [z, z, z, z, z, z, z, z, mu_r, mu_k, mu_v, mu_l, w0, w2p, a0, a2p, g2p, k_k, k_a, r_k,
            gn_g, gn_b]
    if vres is not None:
        v_first, v0, v1p, v2p = vres
        in_specs += [pl.BlockSpec((rows, aw), cur(0)), row_vec(aw), full(v1p), full(v2p)]
        args += [v_first, v0, v1p, v2p]
    out_specs = [pl.BlockSpec((rows, aw), cur(0))]
    out_shape = [jax.ShapeDtypeStruct((n, aw), BF16)]
    if emit_v:
        out_specs.append(pl.BlockSpec((rows, aw), cur(0)))
        out_shape.append(jax.ShapeDtypeStruct((n, aw), F32))
    state = pltpu.VMEM((A_HEADS // 2, 2 * HEAD_DIM, 2 * HEAD_DIM), F32)
    return pl.pallas_call(
        functools.partial(_rwkv_kernel, rows=rows, chunk=chunk, vres=vres is not None, emit_v=emit_v),
        grid=(batch, nr),
        in_specs=in_specs,
        out_specs=out_specs,
        out_shape=out_shape,
        scratch_shapes=[state] + [pltpu.VMEM((rows, aw), F32)] * 9,
        compiler_params=_cparams(("parallel", "arbitrary")),
        name="rwkv",
    )(*args)


def _scan_tile(refs, h_ref, y_ref, chunk, group=16):
    c = chunk
    hd = HEAD_DIM
    assert c == hd, "the [power | inverse] lane layout below needs chunk == head width"
    pw = 2 * hd
    npair = A_HEADS // 2
    nchunk = refs[0].shape[0] // c
    pairs = [slice(p * pw, (p + 1) * pw) for p in range(npair)]
    every = [(ci, p) for ci in range(nchunk) for p in range(npair)]

    ri = lax.broadcasted_iota(jnp.int32, (c, c), 0)
    ci_ = lax.broadcasted_iota(jnp.int32, (c, c), 1)
    tri_ones = jnp.where(ri >= ci_, 1.0, 0.0).astype(BF16)
    row = lax.broadcasted_iota(jnp.int32, (c, pw), 0)
    lane = lax.broadcasted_iota(jnp.int32, (c, pw), 1)
    left = lane < hd
    col = jnp.where(left, lane, lane - hd)
    strict = row > col
    incl = row >= col
    eye_right = jnp.where((row == col) & (lane >= hd), 1.0, 0.0)
    left2 = lax.broadcasted_iota(jnp.int32, (2 * c, pw), 1) < hd
    kr = lax.broadcasted_iota(jnp.int32, (pw, pw), 0)
    kc = lax.broadcasted_iota(jnp.int32, (pw, pw), 1)
    same_head = (kr < hd) == (kc < hd)
    zeros_cv = jnp.zeros((c, pw), BF16)

    ar, bk, bke, vv, p_all = {}, {}, {}, {}, []
    for ci in range(nchunk):
        rs = pl.ds(ci * c, c)
        r, lw, k, v, kk, b = (ref[rs, :] for ref in refs)
        cum = _split_dot(lw, tri_ones, left=True)
        p_inv = jnp.exp(-cum)
        tail = cum[c - 1:c, :]
        p_tail = jnp.exp(tail - cum)
        p_all.append(jnp.exp(tail))
        a_t = (-kk * jnp.exp(cum - lw)).astype(BF16)
        r_t = (r * jnp.exp(cum)).astype(BF16)
        b_t = (b * p_inv).astype(BF16)
        k_t = (k * p_inv).astype(BF16)
        b_e = (b * p_tail).astype(BF16)
        k_e = (k * p_tail).astype(BF16)
        vb = v.astype(BF16)
        for p, s in enumerate(pairs):
            ar[ci, p] = jnp.concatenate([a_t[:, s], r_t[:, s]], axis=0)
            bk[ci, p] = jnp.concatenate([b_t[:, s], k_t[:, s]], axis=0)
            bke[ci, p] = jnp.concatenate([b_e[:, s], k_e[:, s]], axis=0)
            vv[ci, p] = vb[:, s]

    mb, lv, inv = {}, {}, {}
    for g in range(0, len(every), group):
        grp = every[g:g + group]
        sc = {i: _dot_nt(jnp.concatenate([jnp.where(left2, ar[i], 0), jnp.where(left2, 0, ar[i])], axis=0),
                         bk[i]) for i in grp}
        lt = {i: [jnp.where(strict, sc[i][q * 2 * c:q * 2 * c + c], 0.0) for q in range(2)] for i in grp}
        for i in grp:
            mb[i] = jnp.concatenate([jnp.where(incl, sc[i][q * 2 * c + c:(q + 1) * 2 * c], 0.0).astype(BF16)
                                     for q in range(2)], axis=0)
        lvf = {i: _dot(jnp.concatenate(lt[i], axis=0).astype(BF16),
                       jnp.concatenate([zeros_cv, vv[i]], axis=0)) for i in grp}
        for i in grp:
            lv[i] = jnp.where(left, lvf[i][:c], lvf[i][c:])
        x = {i: [jnp.where(left, lt[i][q], eye_right) for q in range(2)] for i in grp}
        for _ in range(int(math.log2(c))):
            z = {i: [_dot(t[:, :hd].astype(BF16), t.astype(BF16)) for t in x[i]] for i in grp}
            x = {i: [jnp.where(left, u, t + u) for t, u in zip(x[i], z[i])] for i in grp}
        for i in grp:
            inv[i] = jnp.where(left, pltpu.roll(x[i][0], hd, axis=1), x[i][1]).astype(BF16)

    h = [h_ref[p] for p in range(npair)]
    for ci in range(nchunk):
        here = [(ci, p) for p in range(npair)]
        ah = [_dot(ar[i], h[p].astype(BF16)) for p, i in enumerate(here)]
        rhs = [ah[p][:c] + lv[i] for p, i in enumerate(here)]
        rhs = [jnp.concatenate([jnp.where(left, t, 0.0), jnp.where(left, 0.0, t)], axis=0).astype(BF16)
               for t in rhs]
        uv = [jnp.concatenate([_dot(inv[i], rhs[p]).astype(BF16), vv[i]], axis=0)
              for p, i in enumerate(here)]
        yy = [_dot(mb[i], uv[p]) for p, i in enumerate(here)]
        for p in range(npair):
            y_ref[pl.ds(ci * c, c), pairs[p]] = ah[p][c:] + jnp.where(left, yy[p][:c], yy[p][c:])
        hn = [_dot_tn(bke[i], uv[p]) for p, i in enumerate(here)]
        h = [jnp.transpose(jnp.broadcast_to(p_all[ci][:, pairs[p]], (pw, pw))) * h[p]
             + jnp.where(same_head, hn[p], 0.0) for p in range(npair)]
    for p in range(npair):
        h_ref[p] = h[p]


def _swa_kernel(sink_ref, q_ref, kc_ref, kp_ref, vc_ref, vp_ref, o_ref):
    blk = WINDOW
    first_key = jnp.where(pl.program_id(1) > 0, 0, blk)
    kj = lax.broadcasted_iota(jnp.int32, (2 * blk, blk), 0)
    qi = lax.broadcasted_iota(jnp.int32, (2 * blk, blk), 1)
    dist = qi + blk - kj
    valid = (dist >= 0) & (dist < WINDOW) & (kj >= first_key)
    neg_dist = jnp.where(valid, -dist.astype(F32), -1e32)
    scale = HEAD_DIM ** -0.5
    assert math.frexp(scale)[0] == 0.5, "a power-of-two scale commutes with the bf16 rounding of q"
    kband = jnp.concatenate([kp_ref[...], kc_ref[...]], axis=0).astype(BF16)
    vband = jnp.concatenate([vp_ref[...], vc_ref[...]], axis=0).astype(BF16)
    q = (q_ref[...] * scale).astype(BF16)
    heads = range(B_HEADS)
    head_cols = lambda t, i: t[:, i * HEAD_DIM:(i + 1) * HEAD_DIM]
    qk = [_dot_nt(head_cols(kband, h // B_GROUP), head_cols(q, h)) for h in heads]
    s = [qk[h] + 2.0 ** (-8.0 * (h + 1) / B_HEADS) * neg_dist for h in heads]
    m = [jnp.maximum(jnp.max(s[h], axis=0, keepdims=True), sink_ref[h]) for h in heads]
    p = [jnp.exp(s[h] - m[h]) for h in heads]
    den = [jnp.sum(p[h], axis=0, keepdims=True) + jnp.exp(sink_ref[h] - m[h]) for h in heads]
    pv = [_dot_tn(head_cols(vband, h // B_GROUP), p[h].astype(BF16)) for h in heads]
    for h in heads:
        o_ref[h * HEAD_DIM:(h + 1) * HEAD_DIM, :] = (pv[h] / den[h]).astype(o_ref.dtype)


def _swa(z, sinks, batch, seq_len):
    n = z.shape[0]
    nb = seq_len // WINDOW
    qw = B_HEADS * HEAD_DIM
    kvw = B_KV_HEADS * HEAD_DIM
    q_blk = 3 * A_WIDTH // qw
    k_blk = (3 * A_WIDTH + qw) // kvw
    cur = lambda col: (lambda bi, j: (bi * nb + j, col))
    prev = lambda col: (lambda bi, j: (bi * nb + jnp.maximum(j - 1, 0), col))
    return pl.pallas_call(
        _swa_kernel,
        grid=(batch, nb),
        in_specs=[
            pl.BlockSpec(memory_space=pltpu.SMEM),
            pl.BlockSpec((WINDOW, qw), cur(q_blk)),
            pl.BlockSpec((WINDOW, kvw), cur(k_blk)),
            pl.BlockSpec((WINDOW, kvw), prev(k_blk)),
            pl.BlockSpec((WINDOW, kvw), cur(k_blk + 1)),
            pl.BlockSpec((WINDOW, kvw), prev(k_blk + 1)),
        ],
        out_specs=pl.BlockSpec((qw, WINDOW), lambda bi, j: (0, bi * nb + j)),
        out_shape=jax.ShapeDtypeStruct((qw, n), BF16),
        compiler_params=_cparams(("parallel", "arbitrary")),
        name="swa",
    )(sinks, z, z, z, z, z)


def _mix_out_kernel(x_ref, oa_ref, obt_ref, wa_ref, wb_ref, o_ref):
    o_ref[...] = x_ref[...] + _dot(oa_ref[...], wa_ref[...]) + _dot_tn(obt_ref[...], wb_ref[...])


def _mix_out(x, oa, obt, wa, wb, tm=512):
    n, d = x.shape
    aw = oa.shape[1]
    bw = obt.shape[0]
    rows = lambda w: pl.BlockSpec((tm, w), lambda i: (i, 0))
    return pl.pallas_call(
        _mix_out_kernel,
        grid=(n // tm,),
        in_specs=[rows(d), rows(aw), pl.BlockSpec((bw, tm), lambda i: (0, i)),
                  pl.BlockSpec((aw, d), lambda i: (0, 0)), pl.BlockSpec((bw, d), lambda i: (0, 0))],
        out_specs=rows(d),
        out_shape=jax.ShapeDtypeStruct((n, d), F32),
        compiler_params=_cparams(("parallel",)),
        name="mix_out",
    )(x, oa, obt, wa, wb)


def _sgu_kernel(x_ref, u_ref, v_ref, vg_ref, ws_ref, bs_ref, wo_ref, o_ref, gate_ref, *, tm):
    cw = C_CHUNK
    vn = _rms(v_ref[...], vg_ref[...]).astype(BF16)
    for ch in range(tm // C_CHUNK):
        rows = slice(ch * C_CHUNK, (ch + 1) * C_CHUNK)
        for grp in range(C_GROUPS):
            cols = slice(grp * cw, (grp + 1) * cw)
            vm = _dot(ws_ref[grp], vn[rows, cols]) + bs_ref[grp]
            gate_ref[rows, cols] = (u_ref[rows, cols] * vm).astype(BF16)
    o_ref[...] = x_ref[...] + _dot(gate_ref[...], wo_ref[...])


def _sgu(x, z, vn_g, ws, bs, wo, tm=512):
    n, d = x.shape
    cwid = wo.shape[0]
    return pl.pallas_call(
        functools.partial(_sgu_kernel, tm=tm),
        grid=(n // tm,),
        in_specs=[
            pl.BlockSpec((tm, d), lambda i: (i, 0)),
            pl.BlockSpec((tm, cwid), lambda i: (i, 0)),
            pl.BlockSpec((tm, cwid), lambda i: (i, 1)),
            pl.BlockSpec((1, cwid), lambda i: (0, 0)),
            pl.BlockSpec(ws.shape, lambda i: (0, 0, 0)),
            pl.BlockSpec(bs.shape, lambda i: (0, 0, 0)),
            pl.BlockSpec((cwid, d), lambda i: (0, 0), pipeline_mode=pl.Buffered(1)),
        ],
        out_specs=pl.BlockSpec((tm, d), lambda i: (i, 0)),
        out_shape=jax.ShapeDtypeStruct((n, d), F32),
        scratch_shapes=[pltpu.VMEM((tm, cwid), BF16)],
        compiler_params=_cparams(("parallel",)),
        name="sgu",
    )(x, z, z, vn_g.reshape(1, cwid), ws, bs, wo)


def _pad_rows(w, lo, total):
    return jnp.pad(w, ((lo, total - lo - w.shape[0]), (0, 0)))


def _even_layer(x, batch, seq_len, g, w_in, b_qkv, mu, w0, w2, a0, a2, g2, k_k, k_a, r_k,
                gn_g, gn_b, sinks, w_out, v_first, vres):
    aw = A_WIDTH
    a_cols = 3 * aw + DECAY_RANK + ICLR_RANK + GATE_RANK
    n_low = a_cols - 3 * aw
    b_cols = w_in.shape[1] - a_cols
    pad_low = LOW_PAD - n_low
    tail = (-(3 * aw + b_cols + LOW_PAD)) % 1024
    w_cat = jnp.concatenate([w_in[:, :3 * aw], w_in[:, a_cols:], w_in[:, 3 * aw:a_cols],
                             jnp.zeros((w_in.shape[0], pad_low + tail), F32)], axis=1).astype(BF16)
    b_cat = jnp.concatenate([jnp.zeros((3 * aw,), F32), b_qkv, jnp.zeros((LOW_PAD + tail,), F32)])
    z = _proj(x, g, w_cat, b_cat, gelu=False)

    mu_l = jnp.pad(mu[3 * aw:], (0, pad_low))
    row = lambda t: t.reshape(1, -1)
    w2p = _pad_rows(w2, 0, LOW_PAD).astype(BF16)
    a2p = _pad_rows(a2, DECAY_RANK, LOW_PAD).astype(BF16)
    g2p = _pad_rows(g2, DECAY_RANK + ICLR_RANK, LOW_PAD).astype(BF16)
    if vres is not None:
        v0, v1, v2 = vres
        rank = v1.shape[1]
        vres_args = (v_first, row(v0), jnp.pad(v1, ((0, 0), (0, VRES_PAD - rank))).astype(BF16),
                     _pad_rows(v2, 0, VRES_PAD).astype(BF16))
    else:
        vres_args = None
    outs = _rwkv(z, batch, seq_len, row(mu[:aw]), row(mu[aw:2 * aw]), row(mu[2 * aw:3 * aw]), row(mu_l),
                 row(w0), w2p, row(a0), a2p, g2p, row(k_k), row(k_a), row(r_k.reshape(-1)),
                 row(gn_g), row(gn_b), vres_args, emit_v=vres is None)
    oa, v = outs if vres is None else (outs[0], None)
    ob = _swa(z, sinks, batch, seq_len)
    wo = w_out.astype(BF16)
    x = _mix_out(x, oa, ob, wo[:aw], wo[aw:])
    return x, v


def _odd_layer(x, g, w_in, vn_g, w_s, b_s, w_out):
    z = _proj(x, g, w_in.astype(BF16), jnp.zeros((w_in.shape[1],), F32), gelu=True)
    ws = jnp.tril(w_s).astype(BF16)
    return _sgu(x, z, vn_g, ws, b_s[:, :, None], w_out.astype(BF16))


def kernel(x, norm_g, ffn_wg, ffn_wu, ffn_wd, e_w_in, e_b_qkv, e_mu, e_w0, e_w2, e_a0, e_a2, e_g2, e_k_k, e_k_a, e_r_k, e_gn_g, e_gn_b, e_sinks, e_w_out, vres_v0, vres_v1, vres_v2, o_w_in, o_vn_g, o_w_s, o_b_s, o_w_out, final_g):
    batch, seq_len, d = x.shape
    depth = norm_g.shape[0]
    x = x.reshape(batch * seq_len, d)
    n_ffn = 2 * depth
    ffn_norm = lambda k: norm_g[k // 2, 2 * (k % 2)]
    weights = tuple(w[0, 0].astype(BF16) for w in (ffn_wg, ffn_wu, ffn_wd))

    def ffn(x, k, weights):
        nxt = (ffn_wg, ffn_wu, ffn_wd, (k + 1) // 2, (k + 1) % 2) if k + 1 < n_ffn else None
        x, cast = _ffn(x, ffn_norm(k), *weights, g_out=final_g if k + 1 == n_ffn else None, cast_next=nxt)
        return x, cast

    v_first = None
    for layer in range(depth):
        x, weights = ffn(x, 2 * layer, weights)
        if layer % 2 == 0:
            i = layer // 2
            vres = None if i == 0 else (vres_v0[i - 1], vres_v1[i - 1], vres_v2[i - 1])
            x, v_a = _even_layer(x, batch, seq_len, norm_g[layer, 1], e_w_in[i], e_b_qkv[i], e_mu[i],
                                 e_w0[i], e_w2[i], e_a0[i], e_a2[i], e_g2[i], e_k_k[i], e_k_a[i],
                                 e_r_k[i], e_gn_g[i], e_gn_b[i], e_sinks[i], e_w_out[i], v_first, vres)
            if i == 0:
                v_first = v_a
        else:
            j = layer // 2
            x = _odd_layer(x, norm_g[layer, 1], o_w_in[j], o_vn_g[j], o_w_s[j], o_b_s[j], o_w_out[j])
        x, weights = ffn(x, 2 * layer + 1, weights)
    return x.reshape(batch, seq_len, d)
```

```python
import functools
import math

import jax
import jax.numpy as jnp
from jax import lax
from jax.experimental import pallas as pl
from jax.experimental.pallas import tpu as pltpu

F32 = jnp.float32
BF16 = jnp.bfloat16

NORM_EPS = 1e-6
GN_EPS = 64e-5
HEAD_DIM = 64
A_HEADS = 16
A_WIDTH = A_HEADS * HEAD_DIM
B_HEADS = 16
B_KV_HEADS = 4
B_GROUP = B_HEADS // B_KV_HEADS
WINDOW = 128
DECAY_RANK = 64
ICLR_RANK = 64
GATE_RANK = 160
LOW_PAD = 384
VRES_PAD = 128
C_GROUPS = 16
C_CHUNK = 128
SCAN_CHUNK = 64

VMEM_LIMIT = 60 * 1024 * 1024


def _cparams(sem):
    return pltpu.CompilerParams(dimension_semantics=sem, vmem_limit_bytes=VMEM_LIMIT)


def _dot(a, b):
    return jnp.dot(a, b, preferred_element_type=F32)


def _dot_nt(a, b):
    return lax.dot_general(a, b, (((1,), (1,)), ((), ())), preferred_element_type=F32)


def _dot_tn(a, b):
    return lax.dot_general(a, b, (((0,), (0,)), ((), ())), preferred_element_type=F32)


def _split_dot(x, ones_bf16, left=False):
    x1 = x.astype(BF16)
    r1 = x - x1.astype(F32)
    x2 = r1.astype(BF16)
    x3 = (r1 - x2.astype(F32)).astype(BF16)
    if left:
        return _dot(ones_bf16, x1) + _dot(ones_bf16, x2) + _dot(ones_bf16, x3)
    return _dot(x1, ones_bf16) + _dot(x2, ones_bf16) + _dot(x3, ones_bf16)


def _rms(x, g):
    return x * lax.rsqrt(jnp.mean(x * x, axis=-1, keepdims=True) + NORM_EPS) * g


def _ffn_kernel(*refs, final_norm, cast_next):
    x_ref, g_ref = refs[:2]
    refs = refs[2:]
    if final_norm:
        gout_ref = refs[0]
        refs = refs[1:]
    wg_ref, wu_ref, wd_ref = refs[:3]
    refs = refs[3:]
    if cast_next:
        next_f32 = refs[:3]
        refs = refs[3:]
    o_ref = refs[0]
    refs = refs[1:]
    if cast_next:
        next_bf16 = refs[:3]
        refs = refs[3:]
    (h_ref,) = refs
    f = pl.program_id(1)

    def half_swiglu(h):
        a = _dot(h, wg_ref[...])
        b = _dot(h, wu_ref[...])
        p = (0.5 * a * jax.nn.sigmoid(a) * b).astype(BF16)
        return _dot(p, wd_ref[...])

    def cast_slice():
        if cast_next:
            for src, dst in zip(next_f32, next_bf16):
                dst[...] = src[...].astype(BF16)

    @pl.when(f == 0)
    def _():
        cast_slice()
        half = x_ref.shape[0] // 2
        for s in range(2):
            rows = pl.ds(s * half, half)
            x = x_ref[rows, :]
            h = _rms(x, g_ref[...]).astype(BF16)
            h_ref[rows, :] = h
            o_ref[rows, :] = x + half_swiglu(h)

    @pl.when(f > 0)
    def _():
        cast_slice()
        o_ref[...] += half_swiglu(h_ref[...])

    if final_norm:
        @pl.when(f == pl.num_programs(1) - 1)
        def _():
            o_ref[...] = _rms(o_ref[...], gout_ref[...])


def _ffn(x, g, wg, wu, wd, g_out=None, cast_next=None, tm=1024, tf=512):
    n, d = x.shape
    fdim = wg.shape[1]
    ni, nf = n // tm, fdim // tf
    vec = pl.BlockSpec((1, d), lambda i, f: (0, 0))
    in_specs = [pl.BlockSpec((tm, d), lambda i, f: (i, 0)), vec]
    args = [x, g.reshape(1, d)]
    if g_out is not None:
        in_specs.append(vec)
        args.append(g_out.reshape(1, d))
    in_specs += [
        pl.BlockSpec((d, tf), lambda i, f: (0, f)),
        pl.BlockSpec((d, tf), lambda i, f: (0, f)),
        pl.BlockSpec((tf, d), lambda i, f: (f, 0)),
    ]
    args += [wg, wu, wd]
    out_specs = [pl.BlockSpec((tm, d), lambda i, f: (i, 0))]
    out_shape = [jax.ShapeDtypeStruct((n, d), F32)]
    if cast_next is not None:
        wg4, wu4, wd4, l, j = cast_next
        up_rows, down_rows = d // ni, fdim // (ni * nf)
        assert up_rows * ni == d and down_rows * ni * nf == fdim and up_rows % 16 == 0 and down_rows % 16 == 0
        in_specs += [
            pl.BlockSpec((None, None, up_rows, tf), lambda i, f: (l, j, i, f)),
            pl.BlockSpec((None, None, up_rows, tf), lambda i, f: (l, j, i, f)),
            pl.BlockSpec((None, None, down_rows, d), lambda i, f: (l, j, i * nf + f, 0)),
        ]
        args += [wg4, wu4, wd4]
        out_specs += [
            pl.BlockSpec((up_rows, tf), lambda i, f: (i, f)),
            pl.BlockSpec((up_rows, tf), lambda i, f: (i, f)),
            pl.BlockSpec((down_rows, d), lambda i, f: (i * nf + f, 0)),
        ]
        out_shape += [jax.ShapeDtypeStruct((d, fdim), BF16), jax.ShapeDtypeStruct((d, fdim), BF16),
                      jax.ShapeDtypeStruct((fdim, d), BF16)]
    outs = pl.pallas_call(
        functools.partial(_ffn_kernel, final_norm=g_out is not None, cast_next=cast_next is not None),
        grid=(ni, nf),
        in_specs=in_specs,
        out_specs=out_specs,
        out_shape=out_shape,
        scratch_shapes=[pltpu.VMEM((tm, d), BF16)],
        compiler_params=_cparams(("parallel", "arbitrary")),
        name="ffn",
    )(*args)
    return outs[0], tuple(outs[1:])


def _proj_kernel(x_ref, g_ref, w_ref, b_ref, o_ref, h_ref, *, gelu, tn):
    j = pl.program_id(1)

    cols = pl.ds(pl.multiple_of(j * tn, tn), tn)

    def project(h):
        z = _dot(h, w_ref[:, cols]) + b_ref[:, cols]
        if gelu:
            z = 0.5 * z * (1.0 + lax.erf(z * (2.0 ** -0.5)))
        return z

    @pl.when(j == 0)
    def _():
        half = x_ref.shape[0] // 2
        for s in range(2):
            rows = pl.ds(s * half, half)
            h = _rms(x_ref[rows, :], g_ref[...]).astype(BF16)
            h_ref[rows, :] = h
            o_ref[rows, :] = project(h)

    @pl.when(j > 0)
    def _():
        o_ref[...] = project(h_ref[...])


def _proj(x, g, w, b, gelu, tm=1024, tn=1024):
    n, d = x.shape
    cols = w.shape[1]
    resident = lambda shape: pl.BlockSpec(shape, lambda i, j: (0, 0), pipeline_mode=pl.Buffered(1))
    return pl.pallas_call(
        functools.partial(_proj_kernel, gelu=gelu, tn=tn),
        grid=(n // tm, cols // tn),
        in_specs=[
            pl.BlockSpec((tm, d), lambda i, j: (i, 0)),
            pl.BlockSpec((1, d), lambda i, j: (0, 0)),
            resident((d, cols)),
            resident((1, cols)),
        ],
        out_specs=pl.BlockSpec((tm, tn), lambda i, j: (i, j)),
        out_shape=jax.ShapeDtypeStruct((n, cols), F32),
        scratch_shapes=[pltpu.VMEM((tm, d), BF16)],
        compiler_params=_cparams(("parallel", "arbitrary")),
        name="proj_gelu" if gelu else "proj",
    )(x, g.reshape(1, d), w, b.reshape(1, cols))


MXU_TILE = 256


def _head_sum(x):
    rh = lax.broadcasted_iota(jnp.int32, (MXU_TILE, MXU_TILE), 0) // HEAD_DIM
    ch = lax.broadcasted_iota(jnp.int32, (MXU_TILE, MXU_TILE), 1) // HEAD_DIM
    ones = jnp.where(rh == ch, 1.0, 0.0).astype(BF16)
    hi = x.astype(BF16)
    lo = (x - hi.astype(F32)).astype(BF16)
    blocks = [slice(s, s + MXU_TILE) for s in range(0, x.shape[1], MXU_TILE)]
    return jnp.concatenate([_dot(hi[:, s], ones) + _dot(lo[:, s], ones) for s in blocks], axis=1)


def _rwkv_kernel(*refs, rows, chunk, vres, emit_v):
    (zr_ref, zk_ref, zv_ref, zl_ref, pr_ref, pk_ref, pv_ref, plo_ref,
     mur_ref, muk_ref, muv_ref, mul_ref, w0_ref, w2_ref, a0_ref, a2_ref, g2_ref,
     kk_ref, ka_ref, rk_ref, gng_ref, gnb_ref) = refs[:22]
    refs = refs[22:]
    if vres:
        vf_ref, v0_ref, v1_ref, v2_ref = refs[:4]
        refs = refs[4:]
    oa_out = refs[0]
    refs = refs[1:]
    if emit_v:
        v_out = refs[0]
        refs = refs[1:]
    h_ref, r_sc, lw_sc, k_sc, v_sc, kk_sc, b_sc, g_sc, bonus_sc, y_sc = refs

    first = pl.program_id(1) == 0

    @pl.when(first)
    def _():
        h_ref[...] = jnp.zeros_like(h_ref)

    keep_prev = jnp.where(first, 0.0, 1.0)
    row = lax.broadcasted_iota(jnp.int32, (rows, 1), 0)

    def shifted(z_ref, p_ref, mu_ref):
        z = z_ref[...]
        prev_row = p_ref[7:8, :] * keep_prev
        prev = jnp.where(row == 0, prev_row, pltpu.roll(z, 1, axis=0))
        return z + (prev - z) * mu_ref[...]

    r = shifted(zr_ref, pr_ref, mur_ref)
    k = shifted(zk_ref, pk_ref, muk_ref)
    v = shifted(zv_ref, pv_ref, muv_ref)
    low = shifted(zl_ref, plo_ref, mul_ref)

    if vres:
        mix = _dot(_dot(v.astype(BF16), v1_ref[...]).astype(BF16), v2_ref[...])
        v = v + (vf_ref[...] - v) * jax.nn.sigmoid(v0_ref[...] + mix)

    dw = w0_ref[...] + _dot(jnp.tanh(low).astype(BF16), w2_ref[...])
    lw_sc[...] = -math.exp(-0.5) * jax.nn.sigmoid(dw)
    a = jax.nn.sigmoid(a0_ref[...] + _dot(low.astype(BF16), a2_ref[...]))
    g_sc[...] = _dot(jax.nn.sigmoid(low).astype(BF16), g2_ref[...])

    kk = k * kk_ref[...]
    kk = kk * jnp.minimum(lax.rsqrt(_head_sum(kk * kk)), 1e12)
    k = k * (1.0 + (a - 1.0) * ka_ref[...])
    r_sc[...] = r
    k_sc[...] = k
    v_sc[...] = v
    if emit_v:
        v_out[...] = v
    kk_sc[...] = kk
    b_sc[...] = kk * a
    bonus_sc[...] = _head_sum(r * k * rk_ref[...]) * v

    _scan_tile((r_sc, lw_sc, k_sc, v_sc, kk_sc, b_sc), h_ref, y_sc, chunk)

    y = y_sc[...]
    yc = y - _head_sum(y) * (1.0 / HEAD_DIM)
    var = _head_sum(yc * yc) * (1.0 / HEAD_DIM)
    yn = yc * lax.rsqrt(var + GN_EPS) * gng_ref[...] + gnb_ref[...]
    oa_out[...] = ((yn + bonus_sc[...]) * g_sc[...]).astype(oa_out.dtype)


def _rwkv(z, batch, seq_len, mu_r, mu_k, mu_v, mu_l, w0, w2p, a0, a2p, g2p, k_k, k_a, r_k,
          gn_g, gn_b, vres, emit_v, rows=256, chunk=SCAN_CHUNK):
    n = z.shape[0]
    aw = A_WIDTH
    low_blk = (3 * aw + B_HEADS * HEAD_DIM + 2 * B_KV_HEADS * HEAD_DIM) // LOW_PAD
    nr = seq_len // rows
    pstep = rows // 8

    def cur(col):
        return lambda bi, j: (bi * nr + j, col)

    def prev_map(col):
        return lambda bi, j: (jnp.maximum((bi * nr + j) * pstep - 1, 0), col)

    row_vec = lambda w: pl.BlockSpec((1, w), lambda bi, j: (0, 0))
    full = lambda a: pl.BlockSpec(a.shape, lambda bi, j: (0, 0))
    in_specs = [
        pl.BlockSpec((rows, aw), cur(0)),
        pl.BlockSpec((rows, aw), cur(1)),
        pl.BlockSpec((rows, aw), cur(2)),
        pl.BlockSpec((rows, LOW_PAD), cur(low_blk)),
        pl.BlockSpec((8, aw), prev_map(0)),
        pl.BlockSpec((8, aw), prev_map(1)),
        pl.BlockSpec((8, aw), prev_map(2)),
        pl.BlockSpec((8, LOW_PAD), prev_map(low_blk)),
        row_vec(aw), row_vec(aw), row_vec(aw), row_vec(LOW_PAD),
        row_vec(aw), full(w2p), row_vec(aw), full(a2p), full(g2p),
        row_vec(aw), row_vec(aw), row_vec(aw), row_vec(aw), row_vec(aw),
    ]
    args = [z, z, z, z, z, z, z, z, mu_r, mu_k, mu_v, mu_l, w0, w2p, a0, a2p, g2p, k_k, k_a, r_k,
            gn_g, gn_b]
    if vres is not None:
        v_first, v0, v1p, v2p = vres
        in_specs += [pl.BlockSpec((rows, aw), cur(0)), row_vec(aw), full(v1p), full(v2p)]
        args += [v_first, v0, v1p, v2p]
    out_specs = [pl.BlockSpec((rows, aw), cur(0))]
    out_shape = [jax.ShapeDtypeStruct((n, aw), BF16)]
    if emit_v:
        out_specs.append(pl.BlockSpec((rows, aw), cur(0)))
        out_shape.append(jax.ShapeDtypeStruct((n, aw), F32))
    state = pltpu.VMEM((A_HEADS // 2, 2 * HEAD_DIM, 2 * HEAD_DIM), F32)
    return pl.pallas_call(
        functools.partial(_rwkv_kernel, rows=rows, chunk=chunk, vres=vres is not None, emit_v=emit_v),
        grid=(batch, nr),
        in_specs=in_specs,
        out_specs=out_specs,
        out_shape=out_shape,
        scratch_shapes=[state] + [pltpu.VMEM((rows, aw), F32)] * 9,
        compiler_params=_cparams(("parallel", "arbitrary")),
        name="rwkv",
    )(*args)


def _scan_tile(refs, h_ref, y_ref, chunk, group=16):
    c = chunk
    hd = HEAD_DIM
    assert c == hd, "the [power | inverse] lane layout below needs chunk == head width"
    pw = 2 * hd
    npair = A_HEADS // 2
    nchunk = refs[0].shape[0] // c
    pairs = [slice(p * pw, (p + 1) * pw) for p in range(npair)]
    every = [(ci, p) for ci in range(nchunk) for p in range(npair)]

    ri = lax.broadcasted_iota(jnp.int32, (c, c), 0)
    ci_ = lax.broadcasted_iota(jnp.int32, (c, c), 1)
    tri_ones = jnp.where(ri >= ci_, 1.0, 0.0).astype(BF16)
    row = lax.broadcasted_iota(jnp.int32, (c, pw), 0)
    lane = lax.broadcasted_iota(jnp.int32, (c, pw), 1)
    left = lane < hd
    col = jnp.where(left, lane, lane - hd)
    strict = row > col
    incl = row >= col
    eye_pair = jnp.where(row == col, 1.0, 0.0)
    couples = {s: (row // (2 * s) == col // (2 * s)) & (row // s != col // s) & strict
               for s in [2 ** e for e in range(int(math.log2(c)))]}

    def pair_diag(t):
        return jnp.concatenate([jnp.where(left, t, 0.0), jnp.where(left, 0.0, t)], axis=0).astype(BF16)

    left2 = lax.broadcasted_iota(jnp.int32, (2 * c, pw), 1) < hd
    kr = lax.broadcasted_iota(jnp.int32, (pw, pw), 0)
    kc = lax.broadcasted_iota(jnp.int32, (pw, pw), 1)
    same_head = (kr < hd) == (kc < hd)
    zeros_cv = jnp.zeros((c, pw), BF16)

    ar, bk, bke, vv, p_all = {}, {}, {}, {}, []
    for ci in range(nchunk):
        rs = pl.ds(ci * c, c)
        r, lw, k, v, kk, b = (ref[rs, :] for ref in refs)
        cum = _split_dot(lw, tri_ones, left=True)
        p_inv = jnp.exp(-cum)
        tail = cum[c - 1:c, :]
        p_tail = jnp.exp(tail - cum)
        p_all.append(jnp.exp(tail))
        a_t = (-kk * jnp.exp(cum - lw)).astype(BF16)
        r_t = (r * jnp.exp(cum)).astype(BF16)
        b_t = (b * p_inv).astype(BF16)
        k_t = (k * p_inv).astype(BF16)
        b_e = (b * p_tail).astype(BF16)
        k_e = (k * p_tail).astype(BF16)
        vb = v.astype(BF16)
        for p, s in enumerate(pairs):
            ar[ci, p] = jnp.concatenate([a_t[:, s], r_t[:, s]], axis=0)
            bk[ci, p] = jnp.concatenate([b_t[:, s], k_t[:, s]], axis=0)
            bke[ci, p] = jnp.concatenate([b_e[:, s], k_e[:, s]], axis=0)
            vv[ci, p] = vb[:, s]

    mb, lv, inv = {}, {}, {}
    for g in range(0, len(every), group):
        grp = every[g:g + group]
        sc = {i: _dot_nt(jnp.concatenate([jnp.where(left2, ar[i], 0), jnp.where(left2, 0, ar[i])], axis=0),
                         bk[i]) for i in grp}
        lt = {i: [jnp.where(strict, sc[i][q * 2 * c:q * 2 * c + c], 0.0) for q in range(2)] for i in grp}
        for i in grp:
            mb[i] = jnp.concatenate([jnp.where(incl, sc[i][q * 2 * c + c:(q + 1) * 2 * c], 0.0).astype(BF16)
                                     for q in range(2)], axis=0)
        lvf = {i: _dot(jnp.concatenate(lt[i], axis=0).astype(BF16),
                       jnp.concatenate([zeros_cv, vv[i]], axis=0)) for i in grp}
        for i in grp:
            lv[i] = jnp.where(left, lvf[i][:c], lvf[i][c:])
        lp = {i: jnp.where(left, lt[i][0], pltpu.roll(lt[i][1], hd, axis=1)) for i in grp}
        t = {i: jnp.where(couples[1], lp[i], eye_pair) for i in grp}
        size = 2
        while size < c:
            g = {i: _dot(t[i].astype(BF16), pair_diag(jnp.where(couples[size], lp[i], 0.0))) for i in grp}
            t = {i: t[i] + _dot(g[i].astype(BF16), pair_diag(t[i])) for i in grp}
            size *= 2
        for i in grp:
            inv[i] = t[i].astype(BF16)

    h = [h_ref[p] for p in range(npair)]
    for ci in range(nchunk):
        here = [(ci, p) for p in range(npair)]
        ah = [_dot(ar[i], h[p].astype(BF16)) for p, i in enumerate(here)]
        rhs = [ah[p][:c] + lv[i] for p, i in enumerate(here)]
        rhs = [pair_diag(t) for t in rhs]
        uv = [jnp.concatenate([_dot(inv[i], rhs[p]).astype(BF16), vv[i]], axis=0)
              for p, i in enumerate(here)]
        yy = [_dot(mb[i], uv[p]) for p, i in enumerate(here)]
        for p in range(npair):
            y_ref[pl.ds(ci * c, c), pairs[p]] = ah[p][c:] + jnp.where(left, yy[p][:c], yy[p][c:])
        hn = [_dot_tn(bke[i], uv[p]) for p, i in enumerate(here)]
        h = [jnp.transpose(jnp.broadcast_to(p_all[ci][:, pairs[p]], (pw, pw))) * h[p]
             + jnp.where(same_head, hn[p], 0.0) for p in range(npair)]
    for p in range(npair):
        h_ref[p] = h[p]


def _swa_kernel(sink_ref, q_ref, kc_ref, kp_ref, vc_ref, vp_ref, o_ref):
    blk = WINDOW
    first_key = jnp.where(pl.program_id(1) > 0, 0, blk)
    kj = lax.broadcasted_iota(jnp.int32, (2 * blk, blk), 0)
    qi = lax.broadcasted_iota(jnp.int32, (2 * blk, blk), 1)
    dist = qi + blk - kj
    valid = (dist >= 0) & (dist < WINDOW) & (kj >= first_key)
    neg_dist = jnp.where(valid, -dist.astype(F32), -1e32)
    scale = HEAD_DIM ** -0.5
    assert math.frexp(scale)[0] == 0.5, "a power-of-two scale commutes with the bf16 rounding of q"
    kband = jnp.concatenate([kp_ref[...], kc_ref[...]], axis=0).astype(BF16)
    vband = jnp.concatenate([vp_ref[...], vc_ref[...]], axis=0).astype(BF16)
    q = (q_ref[...] * scale).astype(BF16)
    heads = range(B_HEADS)
    head_cols = lambda t, i: t[:, i * HEAD_DIM:(i + 1) * HEAD_DIM]
    qk = [_dot_nt(head_cols(kband, h // B_GROUP), head_cols(q, h)) for h in heads]
    s = [qk[h] + 2.0 ** (-8.0 * (h + 1) / B_HEADS) * neg_dist for h in heads]
    m = [jnp.maximum(jnp.max(s[h], axis=0, keepdims=True), sink_ref[h]) for h in heads]
    p = [jnp.exp(s[h] - m[h]) for h in heads]
    den = [jnp.sum(p[h], axis=0, keepdims=True) + jnp.exp(sink_ref[h] - m[h]) for h in heads]
    pv = [_dot_tn(head_cols(vband, h // B_GROUP), p[h].astype(BF16)) for h in heads]
    for h in heads:
        o_ref[h * HEAD_DIM:(h + 1) * HEAD_DIM, :] = (pv[h] / den[h]).astype(o_ref.dtype)


def _swa(z, sinks, batch, seq_len):
    n = z.shape[0]
    nb = seq_len // WINDOW
    qw = B_HEADS * HEAD_DIM
    kvw = B_KV_HEADS * HEAD_DIM
    q_blk = 3 * A_WIDTH // qw
    k_blk = (3 * A_WIDTH + qw) // kvw
    cur = lambda col: (lambda bi, j: (bi * nb + j, col))
    prev = lambda col: (lambda bi, j: (bi * nb + jnp.maximum(j - 1, 0), col))
    return pl.pallas_call(
        _swa_kernel,
        grid=(batch, nb),
        in_specs=[
            pl.BlockSpec(memory_space=pltpu.SMEM),
            pl.BlockSpec((WINDOW, qw), cur(q_blk)),
            pl.BlockSpec((WINDOW, kvw), cur(k_blk)),
            pl.BlockSpec((WINDOW, kvw), prev(k_blk)),
            pl.BlockSpec((WINDOW, kvw), cur(k_blk + 1)),
            pl.BlockSpec((WINDOW, kvw), prev(k_blk + 1)),
        ],
        out_specs=pl.BlockSpec((qw, WINDOW), lambda bi, j: (0, bi * nb + j)),
        out_shape=jax.ShapeDtypeStruct((qw, n), BF16),
        compiler_params=_cparams(("parallel", "arbitrary")),
        name="swa",
    )(sinks, z, z, z, z, z)


def _mix_out_kernel(x_ref, oa_ref, obt_ref, wa_ref, wb_ref, o_ref):
    o_ref[...] = x_ref[...] + _dot(oa_ref[...], wa_ref[...]) + _dot_tn(obt_ref[...], wb_ref[...])


def _mix_out(x, oa, obt, wa, wb, tm=512):
    n, d = x.shape
    aw = oa.shape[1]
    bw = obt.shape[0]
    rows = lambda w: pl.BlockSpec((tm, w), lambda i: (i, 0))
    return pl.pallas_call(
        _mix_out_kernel,
        grid=(n // tm,),
        in_specs=[rows(d), rows(aw), pl.BlockSpec((bw, tm), lambda i: (0, i)),
                  pl.BlockSpec((aw, d), lambda i: (0, 0)), pl.BlockSpec((bw, d), lambda i: (0, 0))],
        out_specs=rows(d),
        out_shape=jax.ShapeDtypeStruct((n, d), F32),
        compiler_params=_cparams(("parallel",)),
        name="mix_out",
    )(x, oa, obt, wa, wb)


def _sgu_kernel(x_ref, u_ref, v_ref, vg_ref, ws_ref, bs_ref, wo_ref, o_ref, gate_ref, *, tm):
    cw = C_CHUNK
    vn = _rms(v_ref[...], vg_ref[...]).astype(BF16)
    for ch in range(tm // C_CHUNK):
        rows = slice(ch * C_CHUNK, (ch + 1) * C_CHUNK)
        for grp in range(C_GROUPS):
            cols = slice(grp * cw, (grp + 1) * cw)
            vm = _dot(ws_ref[grp], vn[rows, cols]) + bs_ref[grp]
            gate_ref[rows, cols] = (u_ref[rows, cols] * vm).astype(BF16)
    o_ref[...] = x_ref[...] + _dot(gate_ref[...], wo_ref[...])


def _sgu(x, z, vn_g, ws, bs, wo, tm=512):
    n, d = x.shape
    cwid = wo.shape[0]
    return pl.pallas_call(
        functools.partial(_sgu_kernel, tm=tm),
        grid=(n // tm,),
        in_specs=[
            pl.BlockSpec((tm, d), lambda i: (i, 0)),
            pl.BlockSpec((tm, cwid), lambda i: (i, 0)),
            pl.BlockSpec((tm, cwid), lambda i: (i, 1)),
            pl.BlockSpec((1, cwid), lambda i: (0, 0)),
            pl.BlockSpec(ws.shape, lambda i: (0, 0, 0)),
            pl.BlockSpec(bs.shape, lambda i: (0, 0, 0)),
            pl.BlockSpec((cwid, d), lambda i: (0, 0), pipeline_mode=pl.Buffered(1)),
        ],
        out_specs=pl.BlockSpec((tm, d), lambda i: (i, 0)),
        out_shape=jax.ShapeDtypeStruct((n, d), F32),
        scratch_shapes=[pltpu.VMEM((tm, cwid), BF16)],
        compiler_params=_cparams(("parallel",)),
        name="sgu",
    )(x, z, z, vn_g.reshape(1, cwid), ws, bs, wo)


def _pad_rows(w, lo, total):
    return jnp.pad(w, ((lo, total - lo - w.shape[0]), (0, 0)))


def _even_layer(x, batch, seq_len, g, w_in, b_qkv, mu, w0, w2, a0, a2, g2, k_k, k_a, r_k,
                gn_g, gn_b, sinks, w_out, v_first, vres):
    aw = A_WIDTH
    a_cols = 3 * aw + DECAY_RANK + ICLR_RANK + GATE_RANK
    n_low = a_cols - 3 * aw
    b_cols = w_in.shape[1] - a_cols
    pad_low = LOW_PAD - n_low
    tail = (-(3 * aw + b_cols + LOW_PAD)) % 1024
    w_cat = jnp.concatenate([w_in[:, :3 * aw], w_in[:, a_cols:], w_in[:, 3 * aw:a_cols],
                             jnp.zeros((w_in.shape[0], pad_low + tail), F32)], axis=1).astype(BF16)
    b_cat = jnp.concatenate([jnp.zeros((3 * aw,), F32), b_qkv, jnp.zeros((LOW_PAD + tail,), F32)])
    z = _proj(x, g, w_cat, b_cat, gelu=False)

    mu_l = jnp.pad(mu[3 * aw:], (0, pad_low))
    row = lambda t: t.reshape(1, -1)
    w2p = _pad_rows(w2, 0, LOW_PAD).astype(BF16)
    a2p = _pad_rows(a2, DECAY_RANK, LOW_PAD).astype(BF16)
    g2p = _pad_rows(g2, DECAY_RANK + ICLR_RANK, LOW_PAD).astype(BF16)
    if vres is not None:
        v0, v1, v2 = vres
        rank = v1.shape[1]
        vres_args = (v_first, row(v0), jnp.pad(v1, ((0, 0), (0, VRES_PAD - rank))).astype(BF16),
                     _pad_rows(v2, 0, VRES_PAD).astype(BF16))
    else:
        vres_args = None
    outs = _rwkv(z, batch, seq_len, row(mu[:aw]), row(mu[aw:2 * aw]), row(mu[2 * aw:3 * aw]), row(mu_l),
                 row(w0), w2p, row(a0), a2p, g2p, row(k_k), row(k_a), row(r_k.reshape(-1)),
                 row(gn_g), row(gn_b), vres_args, emit_v=vres is None)
    oa, v = outs if vres is None else (outs[0], None)
    ob = _swa(z, sinks, batch, seq_len)
    wo = w_out.astype(BF16)
    x = _mix_out(x, oa, ob, wo[:aw], wo[aw:])
    return x, v


def _odd_layer(x, g, w_in, vn_g, w_s, b_s, w_out):
    z = _proj(x, g, w_in.astype(BF16), jnp.zeros((w_in.shape[1],), F32), gelu=True)
    ws = jnp.tril(w_s).astype(BF16)
    return _sgu(x, z, vn_g, ws, b_s[:, :, None], w_out.astype(BF16))


def kernel(x, norm_g, ffn_wg, ffn_wu, ffn_wd, e_w_in, e_b_qkv, e_mu, e_w0, e_w2, e_a0, e_a2, e_g2, e_k_k, e_k_a, e_r_k, e_gn_g, e_gn_b, e_sinks, e_w_out, vres_v0, vres_v1, vres_v2, o_w_in, o_vn_g, o_w_s, o_b_s, o_w_out, final_g):
    batch, seq_len, d = x.shape
    depth = norm_g.shape[0]
    x = x.reshape(batch * seq_len, d)
    n_ffn = 2 * depth
    ffn_norm = lambda k: norm_g[k // 2, 2 * (k % 2)]
    weights = tuple(w[0, 0].astype(BF16) for w in (ffn_wg, ffn_wu, ffn_wd))

    def ffn(x, k, weights):
        nxt = (ffn_wg, ffn_wu, ffn_wd, (k + 1) // 2, (k + 1) % 2) if k + 1 < n_ffn else None
        x, cast = _ffn(x, ffn_norm(k), *weights, g_out=final_g if k + 1 == n_ffn else None, cast_next=nxt)
        return x, cast

    v_first = None
    for layer in range(depth):
        x, weights = ffn(x, 2 * layer, weights)
        if layer % 2 == 0:
            i = layer // 2
            vres = None if i == 0 else (vres_v0[i - 1], vres_v1[i - 1], vres_v2[i - 1])
            x, v_a = _even_layer(x, batch, seq_len, norm_g[layer, 1], e_w_in[i], e_b_qkv[i], e_mu[i],
                                 e_w0[i], e_w2[i], e_a0[i], e_a2[i], e_g2[i], e_k_k[i], e_k_a[i],
                                 e_r_k[i], e_gn_g[i], e_gn_b[i], e_sinks[i], e_w_out[i], v_first, vres)
            if i == 0:
                v_first = v_a
        else:
            j = layer // 2
            x = _odd_layer(x, norm_g[layer, 1], o_w_in[j], o_vn_g[j], o_w_s[j], o_b_s[j], o_w_out[j])
        x, weights = ffn(x, 2 * layer + 1, weights)
    return x.reshape(batch, seq_len, d)
```

```python
import functools
import math

import jax
import jax.numpy as jnp
from jax import lax
from jax.experimental import pallas as pl
from jax.experimental.pallas import tpu as pltpu

F32 = jnp.float32
BF16 = jnp.bfloat16

NORM_EPS = 1e-6
GN_EPS = 64e-5
HEAD_DIM = 64
A_HEADS = 16
A_WIDTH = A_HEADS * HEAD_DIM
B_HEADS = 16
B_KV_HEADS = 4
B_GROUP = B_HEADS // B_KV_HEADS
WINDOW = 128
DECAY_RANK = 64
ICLR_RANK = 64
GATE_RANK = 160
LOW_PAD = 384
VRES_PAD = 128
C_GROUPS = 16
C_CHUNK = 128
SCAN_CHUNK = 64
BF16_TILE_ROWS = 16
PROJ_TN = 1024

VMEM_LIMIT = 60 * 1024 * 1024


def _cparams(sem):
    return pltpu.CompilerParams(dimension_semantics=sem, vmem_limit_bytes=VMEM_LIMIT)


def _dot(a, b):
    return jnp.dot(a, b, preferred_element_type=F32)


def _dot_nt(a, b):
    return lax.dot_general(a, b, (((1,), (1,)), ((), ())), preferred_element_type=F32)


def _dot_tn(a, b):
    return lax.dot_general(a, b, (((0,), (0,)), ((), ())), preferred_element_type=F32)


def _split_dot(x, ones_bf16, left=False):
    x1 = x.astype(BF16)
    r1 = x - x1.astype(F32)
    x2 = r1.astype(BF16)
    x3 = (r1 - x2.astype(F32)).astype(BF16)
    if left:
        return _dot(ones_bf16, x1) + _dot(ones_bf16, x2) + _dot(ones_bf16, x3)
    return _dot(x1, ones_bf16) + _dot(x2, ones_bf16) + _dot(x3, ones_bf16)


def _rms(x, g):
    return x * lax.rsqrt(jnp.mean(x * x, axis=-1, keepdims=True) + NORM_EPS) * g


def _ffn_kernel(*refs, final_norm, cast_next):
    x_ref, g_ref = refs[:2]
    refs = refs[2:]
    if final_norm:
        gout_ref = refs[0]
        refs = refs[1:]
    wg_ref, wu_ref, wd_ref = refs[:3]
    refs = refs[3:]
    if cast_next:
        next_f32 = refs[:3]
        refs = refs[3:]
    o_ref = refs[0]
    refs = refs[1:]
    if cast_next:
        next_bf16 = refs[:3]
        refs = refs[3:]
    (h_ref,) = refs
    f = pl.program_id(1)

    def half_swiglu(h):
        a = _dot(h, wg_ref[...])
        b = _dot(h, wu_ref[...])
        p = (0.5 * a * jax.nn.sigmoid(a) * b).astype(BF16)
        return _dot(p, wd_ref[...])

    def cast_slice():
        if cast_next:
            for src, dst in zip(next_f32, next_bf16):
                dst[...] = src[...].astype(BF16)

    @pl.when(f == 0)
    def _():
        cast_slice()
        half = x_ref.shape[0] // 2
        for s in range(2):
            rows = pl.ds(s * half, half)
            x = x_ref[rows, :]
            h = _rms(x, g_ref[...]).astype(BF16)
            h_ref[rows, :] = h
            o_ref[rows, :] = x + half_swiglu(h)

    @pl.when(f > 0)
    def _():
        cast_slice()
        o_ref[...] += half_swiglu(h_ref[...])

    if final_norm:
        @pl.when(f == pl.num_programs(1) - 1)
        def _():
            o_ref[...] = _rms(o_ref[...], gout_ref[...])


def _ffn(x, g, wg, wu, wd, g_out=None, cast_next=None, tm=1024, tf=512):
    n, d = x.shape
    fdim = wg.shape[1]
    ni, nf = n // tm, fdim // tf
    vec = pl.BlockSpec((1, d), lambda i, f: (0, 0))
    in_specs = [pl.BlockSpec((tm, d), lambda i, f: (i, 0)), vec]
    args = [x, g.reshape(1, d)]
    if g_out is not None:
        in_specs.append(vec)
        args.append(g_out.reshape(1, d))
    in_specs += [
        pl.BlockSpec((d, tf), lambda i, f: (0, f)),
        pl.BlockSpec((d, tf), lambda i, f: (0, f)),
        pl.BlockSpec((tf, d), lambda i, f: (f, 0)),
    ]
    args += [wg, wu, wd]
    out_specs = [pl.BlockSpec((tm, d), lambda i, f: (i, 0))]
    out_shape = [jax.ShapeDtypeStruct((n, d), F32)]
    if cast_next is not None:
        wg4, wu4, wd4, l, j = cast_next
        up_rows, down_rows = d // ni, fdim // (ni * nf)
        assert up_rows * ni == d and down_rows * ni * nf == fdim
        assert up_rows % BF16_TILE_ROWS == 0 and down_rows % BF16_TILE_ROWS == 0
        in_specs += [
            pl.BlockSpec((None, None, up_rows, tf), lambda i, f: (l, j, i, f)),
            pl.BlockSpec((None, None, up_rows, tf), lambda i, f: (l, j, i, f)),
            pl.BlockSpec((None, None, down_rows, d), lambda i, f: (l, j, i * nf + f, 0)),
        ]
        args += [wg4, wu4, wd4]
        out_specs += [
            pl.BlockSpec((up_rows, tf), lambda i, f: (i, f)),
            pl.BlockSpec((up_rows, tf), lambda i, f: (i, f)),
            pl.BlockSpec((down_rows, d), lambda i, f: (i * nf + f, 0)),
        ]
        out_shape += [jax.ShapeDtypeStruct((d, fdim), BF16), jax.ShapeDtypeStruct((d, fdim), BF16),
                      jax.ShapeDtypeStruct((fdim, d), BF16)]
    outs = pl.pallas_call(
        functools.partial(_ffn_kernel, final_norm=g_out is not None, cast_next=cast_next is not None),
        grid=(ni, nf),
        in_specs=in_specs,
        out_specs=out_specs,
        out_shape=out_shape,
        scratch_shapes=[pltpu.VMEM((tm, d), BF16)],
        compiler_params=_cparams(("parallel", "arbitrary")),
        name="ffn",
    )(*args)
    return outs[0], tuple(outs[1:])


def _proj_kernel(x_ref, g_ref, w_ref, b_ref, o_ref, h_ref, *, gelu, tn):
    j = pl.program_id(1)

    cols = pl.ds(pl.multiple_of(j * tn, tn), tn)

    def project(h):
        z = _dot(h, w_ref[:, cols]) + b_ref[:, cols]
        if gelu:
            z = 0.5 * z * (1.0 + lax.erf(z * (2.0 ** -0.5)))
        return z

    @pl.when(j == 0)
    def _():
        half = x_ref.shape[0] // 2
        for s in range(2):
            rows = pl.ds(s * half, half)
            h = _rms(x_ref[rows, :], g_ref[...]).astype(BF16)
            h_ref[rows, :] = h
            o_ref[rows, :] = project(h)

    @pl.when(j > 0)
    def _():
        o_ref[...] = project(h_ref[...])


def _proj(x, g, w, b, gelu, tm=1024, tn=PROJ_TN):
    n, d = x.shape
    cols = w.shape[1]
    resident = lambda shape: pl.BlockSpec(shape, lambda i, j: (0, 0), pipeline_mode=pl.Buffered(1))
    return pl.pallas_call(
        functools.partial(_proj_kernel, gelu=gelu, tn=tn),
        grid=(n // tm, cols // tn),
        in_specs=[
            pl.BlockSpec((tm, d), lambda i, j: (i, 0)),
            pl.BlockSpec((1, d), lambda i, j: (0, 0)),
            resident((d, cols)),
            resident((1, cols)),
        ],
        out_specs=pl.BlockSpec((tm, tn), lambda i, j: (i, j)),
        out_shape=jax.ShapeDtypeStruct((n, cols), F32),
        scratch_shapes=[pltpu.VMEM((tm, d), BF16)],
        compiler_params=_cparams(("parallel", "arbitrary")),
        name="proj_gelu" if gelu else "proj",
    )(x, g.reshape(1, d), w, b.reshape(1, cols))


MXU_TILE = 256


def _head_sum(x):
    rh = lax.broadcasted_iota(jnp.int32, (MXU_TILE, MXU_TILE), 0) // HEAD_DIM
    ch = lax.broadcasted_iota(jnp.int32, (MXU_TILE, MXU_TILE), 1) // HEAD_DIM
    ones = jnp.where(rh == ch, 1.0, 0.0).astype(BF16)
    hi = x.astype(BF16)
    lo = (x - hi.astype(F32)).astype(BF16)
    blocks = [slice(s, s + MXU_TILE) for s in range(0, x.shape[1], MXU_TILE)]
    return jnp.concatenate([_dot(hi[:, s], ones) + _dot(lo[:, s], ones) for s in blocks], axis=1)


def _rwkv_kernel(*refs, rows, chunk, vres, emit_v):
    (zr_ref, zk_ref, zv_ref, zl_ref, pr_ref, pk_ref, pv_ref, plo_ref,
     mur_ref, muk_ref, muv_ref, mul_ref, w0_ref, w2_ref, a0_ref, a2_ref, g2_ref,
     kk_ref, ka_ref, rk_ref, gng_ref, gnb_ref) = refs[:22]
    refs = refs[22:]
    if vres:
        vf_ref, v0_ref, v1_ref, v2_ref = refs[:4]
        refs = refs[4:]
    oa_out = refs[0]
    refs = refs[1:]
    if emit_v:
        v_out = refs[0]
        refs = refs[1:]
    h_ref, r_sc, lw_sc, k_sc, v_sc, kk_sc, b_sc, g_sc, bonus_sc, y_sc = refs

    first = pl.program_id(1) == 0

    @pl.when(first)
    def _():
        h_ref[...] = jnp.zeros_like(h_ref)

    keep_prev = jnp.where(first, 0.0, 1.0)
    row = lax.broadcasted_iota(jnp.int32, (rows, 1), 0)

    def shifted(z_ref, p_ref, mu_ref):
        z = z_ref[...]
        prev_row = p_ref[7:8, :] * keep_prev
        prev = jnp.where(row == 0, prev_row, pltpu.roll(z, 1, axis=0))
        return z + (prev - z) * mu_ref[...]

    r = shifted(zr_ref, pr_ref, mur_ref)
    k = shifted(zk_ref, pk_ref, muk_ref)
    v = shifted(zv_ref, pv_ref, muv_ref)
    low = shifted(zl_ref, plo_ref, mul_ref)

    if vres:
        mix = _dot(_dot(v.astype(BF16), v1_ref[...]).astype(BF16), v2_ref[...])
        v = v + (vf_ref[...] - v) * jax.nn.sigmoid(v0_ref[...] + mix)

    dw = w0_ref[...] + _dot(jnp.tanh(low).astype(BF16), w2_ref[...])
    lw_sc[...] = -math.exp(-0.5) * jax.nn.sigmoid(dw)
    a = jax.nn.sigmoid(a0_ref[...] + _dot(low.astype(BF16), a2_ref[...]))
    g_sc[...] = _dot(jax.nn.sigmoid(low).astype(BF16), g2_ref[...])

    kk = k * kk_ref[...]
    kk = kk * jnp.minimum(lax.rsqrt(_head_sum(kk * kk)), 1e12)
    k = k * (1.0 + (a - 1.0) * ka_ref[...])
    r_sc[...] = r
    k_sc[...] = k
    v_sc[...] = v
    if emit_v:
        v_out[...] = v
    kk_sc[...] = kk
    b_sc[...] = kk * a
    bonus_sc[...] = _head_sum(r * k * rk_ref[...]) * v

    _scan_tile((r_sc, lw_sc, k_sc, v_sc, kk_sc, b_sc), h_ref, y_sc, chunk)

    y = y_sc[...]
    yc = y - _head_sum(y) * (1.0 / HEAD_DIM)
    var = _head_sum(yc * yc) * (1.0 / HEAD_DIM)
    yn = yc * lax.rsqrt(var + GN_EPS) * gng_ref[...] + gnb_ref[...]
    oa_out[...] = ((yn + bonus_sc[...]) * g_sc[...]).astype(oa_out.dtype)


def _rwkv(z, batch, seq_len, mu_r, mu_k, mu_v, mu_l, w0, w2p, a0, a2p, g2p, k_k, k_a, r_k,
          gn_g, gn_b, vres, emit_v, rows=256, chunk=SCAN_CHUNK):
    n = z.shape[0]
    aw = A_WIDTH
    low_blk = (3 * aw + B_HEADS * HEAD_DIM + 2 * B_KV_HEADS * HEAD_DIM) // LOW_PAD
    nr = seq_len // rows
    pstep = rows // 8

    def cur(col):
        return lambda bi, j: (bi * nr + j, col)

    def prev_map(col):
        return lambda bi, j: (jnp.maximum((bi * nr + j) * pstep - 1, 0), col)

    row_vec = lambda w: pl.BlockSpec((1, w), lambda bi, j: (0, 0))
    full = lambda a: pl.BlockSpec(a.shape, lambda bi, j: (0, 0))
    in_specs = [
        pl.BlockSpec((rows, aw), cur(0)),
        pl.BlockSpec((rows, aw), cur(1)),
        pl.BlockSpec((rows, aw), cur(2)),
        pl.BlockSpec((rows, LOW_PAD), cur(low_blk)),
        pl.BlockSpec((8, aw), prev_map(0)),
        pl.BlockSpec((8, aw), prev_map(1)),
        pl.BlockSpec((8, aw), prev_map(2)),
        pl.BlockSpec((8, LOW_PAD), prev_map(low_blk)),
        row_vec(aw), row_vec(aw), row_vec(aw), row_vec(LOW_PAD),
        row_vec(aw), full(w2p), row_vec(aw), full(a2p), full(g2p),
        row_vec(aw), row_vec(aw), row_vec(aw), row_vec(aw), row_vec(aw),
    ]
    args = [z, z, z, z, z, z, z, z, mu_r, mu_k, mu_v, mu_l, w0, w2p, a0, a2p, g2p, k_k, k_a, r_k,
            gn_g, gn_b]
    if vres is not None:
        v_first, v0, v1p, v2p = vres
        in_specs += [pl.BlockSpec((rows, aw), cur(0)), row_vec(aw), full(v1p), full(v2p)]
        args += [v_first, v0, v1p, v2p]
    out_specs = [pl.BlockSpec((rows, aw), cur(0))]
    out_shape = [jax.ShapeDtypeStruct((n, aw), BF16)]
    if emit_v:
        out_specs.append(pl.BlockSpec((rows, aw), cur(0)))
        out_shape.append(jax.ShapeDtypeStruct((n, aw), F32))
    state = pltpu.VMEM((A_HEADS // 2, 2 * HEAD_DIM, 2 * HEAD_DIM), F32)
    return pl.pallas_call(
        functools.partial(_rwkv_kernel, rows=rows, chunk=chunk, vres=vres is not None, emit_v=emit_v),
        grid=(batch, nr),
        in_specs=in_specs,
        out_specs=out_specs,
        out_shape=out_shape,
        scratch_shapes=[state] + [pltpu.VMEM((rows, aw), F32)] * 9,
        compiler_params=_cparams(("parallel", "arbitrary")),
        name="rwkv",
    )(*args)


def _scan_tile(refs, h_ref, y_ref, chunk, group=16):
    c = chunk
    hd = HEAD_DIM
    assert c == hd, "the [power | inverse] lane layout below needs chunk == head width"
    pw = 2 * hd
    npair = A_HEADS // 2
    nchunk = refs[0].shape[0] // c
    pairs = [slice(p * pw, (p + 1) * pw) for p in range(npair)]
    every = [(ci, p) for ci in range(nchunk) for p in range(npair)]

    ri = lax.broadcasted_iota(jnp.int32, (c, c), 0)
    ci_ = lax.broadcasted_iota(jnp.int32, (c, c), 1)
    tri_ones = jnp.where(ri >= ci_, 1.0, 0.0).astype(BF16)
    row = lax.broadcasted_iota(jnp.int32, (c, pw), 0)
    lane = lax.broadcasted_iota(jnp.int32, (c, pw), 1)
    left = lane < hd
    col = jnp.where(left, lane, lane - hd)
    strict = row > col
    incl = row >= col
    eye_pair = jnp.where(row == col, 1.0, 0.0)
    couples = {s: (row // (2 * s) == col // (2 * s)) & (row // s != col // s) & strict
               for s in [2 ** e for e in range(int(math.log2(c)))]}

    def pair_diag(t):
        return jnp.concatenate([jnp.where(left, t, 0.0), jnp.where(left, 0.0, t)], axis=0).astype(BF16)

    left2 = lax.broadcasted_iota(jnp.int32, (2 * c, pw), 1) < hd
    kr = lax.broadcasted_iota(jnp.int32, (pw, pw), 0)
    kc = lax.broadcasted_iota(jnp.int32, (pw, pw), 1)
    same_head = (kr < hd) == (kc < hd)
    zeros_cv = jnp.zeros((c, pw), BF16)

    ar, bk, bke, vv, p_all = {}, {}, {}, {}, []
    for ci in range(nchunk):
        rs = pl.ds(ci * c, c)
        r, lw, k, v, kk, b = (ref[rs, :] for ref in refs)
        cum = _split_dot(lw, tri_ones, left=True)
        p_inv = jnp.exp(-cum)
        tail = cum[c - 1:c, :]
        p_tail = jnp.exp(tail - cum)
        p_all.append(jnp.exp(tail))
        a_t = (-kk * jnp.exp(cum - lw)).astype(BF16)
        r_t = (r * jnp.exp(cum)).astype(BF16)
        b_t = (b * p_inv).astype(BF16)
        k_t = (k * p_inv).astype(BF16)
        b_e = (b * p_tail).astype(BF16)
        k_e = (k * p_tail).astype(BF16)
        vb = v.astype(BF16)
        for p, s in enumerate(pairs):
            ar[ci, p] = jnp.concatenate([a_t[:, s], r_t[:, s]], axis=0)
            bk[ci, p] = jnp.concatenate([b_t[:, s], k_t[:, s]], axis=0)
            bke[ci, p] = jnp.concatenate([b_e[:, s], k_e[:, s]], axis=0)
            vv[ci, p] = vb[:, s]

    mb, lv, inv = {}, {}, {}
    for g in range(0, len(every), group):
        grp = every[g:g + group]
        sc = {i: _dot_nt(jnp.concatenate([jnp.where(left2, ar[i], 0), jnp.where(left2, 0, ar[i])], axis=0),
                         bk[i]) for i in grp}
        lt = {i: [jnp.where(strict, sc[i][q * 2 * c:q * 2 * c + c], 0.0) for q in range(2)] for i in grp}
        for i in grp:
            mb[i] = jnp.concatenate([jnp.where(incl, sc[i][q * 2 * c + c:(q + 1) * 2 * c], 0.0).astype(BF16)
                                     for q in range(2)], axis=0)
        lvf = {i: _dot(jnp.concatenate(lt[i], axis=0).astype(BF16),
                       jnp.concatenate([zeros_cv, vv[i]], axis=0)) for i in grp}
        for i in grp:
            lv[i] = jnp.where(left, lvf[i][:c], lvf[i][c:])
        lp = {i: jnp.where(left, lt[i][0], pltpu.roll(lt[i][1], hd, axis=1)) for i in grp}
        t = {i: jnp.where(couples[1], lp[i], eye_pair) for i in grp}
        size = 2
        while size < c:
            g = {i: _dot(t[i].astype(BF16), pair_diag(jnp.where(couples[size], lp[i], 0.0))) for i in grp}
            t = {i: t[i] + _dot(g[i].astype(BF16), pair_diag(t[i])) for i in grp}
            size *= 2
        for i in grp:
            inv[i] = t[i].astype(BF16)

    h = [h_ref[p] for p in range(npair)]
    for ci in range(nchunk):
        here = [(ci, p) for p in range(npair)]
        ah = [_dot(ar[i], h[p].astype(BF16)) for p, i in enumerate(here)]
        rhs = [ah[p][:c] + lv[i] for p, i in enumerate(here)]
        rhs = [pair_diag(t) for t in rhs]
        uv = [jnp.concatenate([_dot(inv[i], rhs[p]).astype(BF16), vv[i]], axis=0)
              for p, i in enumerate(here)]
        yy = [_dot(mb[i], uv[p]) for p, i in enumerate(here)]
        for p in range(npair):
            y_ref[pl.ds(ci * c, c), pairs[p]] = ah[p][c:] + jnp.where(left, yy[p][:c], yy[p][c:])
        hn = [_dot_tn(bke[i], uv[p]) for p, i in enumerate(here)]
        h = [jnp.transpose(jnp.broadcast_to(p_all[ci][:, pairs[p]], (pw, pw))) * h[p]
             + jnp.where(same_head, hn[p], 0.0) for p in range(npair)]
    for p in range(npair):
        h_ref[p] = h[p]


def _swa_kernel(sink_ref, q_ref, kc_ref, kp_ref, vc_ref, vp_ref, o_ref):
    blk = WINDOW
    r = lax.broadcasted_iota(jnp.int32, (blk, blk), 0)
    qi = lax.broadcasted_iota(jnp.int32, (blk, blk), 1)
    from_prev = r > qi
    no_prev = jnp.where(pl.program_id(1) > 0, 0.0, -1e32)
    neg_dist = jnp.where(from_prev, (r - qi - blk).astype(F32) + no_prev, (r - qi).astype(F32))
    scale = HEAD_DIM ** -0.5
    assert math.frexp(scale)[0] == 0.5, "a power-of-two scale commutes with the bf16 rounding of q"
    kband = jnp.concatenate([kp_ref[...], kc_ref[...]], axis=0).astype(BF16)
    vband = jnp.concatenate([vp_ref[...], vc_ref[...]], axis=0).astype(BF16)
    q = (q_ref[...] * scale).astype(BF16)
    heads = range(B_HEADS)
    head_cols = lambda t, i: t[:, i * HEAD_DIM:(i + 1) * HEAD_DIM]
    qk = [_dot_nt(head_cols(kband, h // B_GROUP), head_cols(q, h)) for h in heads]
    qk = [jnp.where(from_prev, t[:blk], t[blk:]) for t in qk]
    s = [qk[h] + 2.0 ** (-8.0 * (h + 1) / B_HEADS) * neg_dist for h in heads]
    m = [jnp.maximum(jnp.max(s[h], axis=0, keepdims=True), sink_ref[h]) for h in heads]
    p = [jnp.exp(s[h] - m[h]) for h in heads]
    den = [jnp.sum(p[h], axis=0, keepdims=True) + jnp.exp(sink_ref[h] - m[h]) for h in heads]
    p = [jnp.concatenate([jnp.where(from_prev, t, 0.0), jnp.where(from_prev, 0.0, t)], axis=0).astype(BF16)
         for t in p]
    pv = [_dot_tn(head_cols(vband, h // B_GROUP), p[h]) for h in heads]
    for h in heads:
        o_ref[h * HEAD_DIM:(h + 1) * HEAD_DIM, :] = (pv[h] / den[h]).astype(o_ref.dtype)


def _swa(z, sinks, batch, seq_len):
    n = z.shape[0]
    nb = seq_len // WINDOW
    qw = B_HEADS * HEAD_DIM
    kvw = B_KV_HEADS * HEAD_DIM
    q_blk = 3 * A_WIDTH // qw
    k_blk = (3 * A_WIDTH + qw) // kvw
    cur = lambda col: (lambda bi, j: (bi * nb + j, col))
    prev = lambda col: (lambda bi, j: (bi * nb + jnp.maximum(j - 1, 0), col))
    return pl.pallas_call(
        _swa_kernel,
        grid=(batch, nb),
        in_specs=[
            pl.BlockSpec(memory_space=pltpu.SMEM),
            pl.BlockSpec((WINDOW, qw), cur(q_blk)),
            pl.BlockSpec((WINDOW, kvw), cur(k_blk)),
            pl.BlockSpec((WINDOW, kvw), prev(k_blk)),
            pl.BlockSpec((WINDOW, kvw), cur(k_blk + 1)),
            pl.BlockSpec((WINDOW, kvw), prev(k_blk + 1)),
        ],
        out_specs=pl.BlockSpec((qw, WINDOW), lambda bi, j: (0, bi * nb + j)),
        out_shape=jax.ShapeDtypeStruct((qw, n), BF16),
        compiler_params=_cparams(("parallel", "arbitrary")),
        name="swa",
    )(sinks, z, z, z, z, z)


def _mix_out_kernel(x_ref, oa_ref, obt_ref, wa_ref, wb_ref, o_ref):
    o_ref[...] = x_ref[...] + _dot(oa_ref[...], wa_ref[...]) + _dot_tn(obt_ref[...], wb_ref[...])


def _mix_out(x, oa, obt, wa, wb, tm=512):
    n, d = x.shape
    aw = oa.shape[1]
    bw = obt.shape[0]
    rows = lambda w: pl.BlockSpec((tm, w), lambda i: (i, 0))
    return pl.pallas_call(
        _mix_out_kernel,
        grid=(n // tm,),
        in_specs=[rows(d), rows(aw), pl.BlockSpec((bw, tm), lambda i: (0, i)),
                  pl.BlockSpec((aw, d), lambda i: (0, 0)), pl.BlockSpec((bw, d), lambda i: (0, 0))],
        out_specs=rows(d),
        out_shape=jax.ShapeDtypeStruct((n, d), F32),
        compiler_params=_cparams(("parallel",)),
        name="mix_out",
    )(x, oa, obt, wa, wb)


def _sgu_kernel(x_ref, u_ref, v_ref, vg_ref, ws_ref, bs_ref, wo_ref, o_ref, gate_ref, *, tm):
    cw = C_CHUNK
    vn = _rms(v_ref[...], vg_ref[...]).astype(BF16)
    for ch in range(tm // C_CHUNK):
        rows = slice(ch * C_CHUNK, (ch + 1) * C_CHUNK)
        for grp in range(C_GROUPS):
            cols = slice(grp * cw, (grp + 1) * cw)
            vm = _dot(ws_ref[grp], vn[rows, cols]) + bs_ref[grp]
            gate_ref[rows, cols] = (u_ref[rows, cols] * vm).astype(BF16)
    o_ref[...] = x_ref[...] + _dot(gate_ref[...], wo_ref[...])


def _sgu(x, z, vn_g, ws, bs, wo, tm=512):
    n, d = x.shape
    cwid = wo.shape[0]
    return pl.pallas_call(
        functools.partial(_sgu_kernel, tm=tm),
        grid=(n // tm,),
        in_specs=[
            pl.BlockSpec((tm, d), lambda i: (i, 0)),
            pl.BlockSpec((tm, cwid), lambda i: (i, 0)),
            pl.BlockSpec((tm, cwid), lambda i: (i, 1)),
            pl.BlockSpec((1, cwid), lambda i: (0, 0)),
            pl.BlockSpec(ws.shape, lambda i: (0, 0, 0)),
            pl.BlockSpec(bs.shape, lambda i: (0, 0, 0)),
            pl.BlockSpec((cwid, d), lambda i: (0, 0), pipeline_mode=pl.Buffered(1)),
        ],
        out_specs=pl.BlockSpec((tm, d), lambda i: (i, 0)),
        out_shape=jax.ShapeDtypeStruct((n, d), F32),
        scratch_shapes=[pltpu.VMEM((tm, cwid), BF16)],
        compiler_params=_cparams(("parallel",)),
        name="sgu",
    )(x, z, z, vn_g.reshape(1, cwid), ws, bs, wo)


def _pad_rows(w, lo, total):
    return jnp.pad(w, ((lo, total - lo - w.shape[0]), (0, 0)))


def _even_layer(x, batch, seq_len, g, w_in, b_qkv, mu, w0, w2, a0, a2, g2, k_k, k_a, r_k,
                gn_g, gn_b, sinks, w_out, v_first, vres):
    aw = A_WIDTH
    a_cols = 3 * aw + DECAY_RANK + ICLR_RANK + GATE_RANK
    n_low = a_cols - 3 * aw
    b_cols = w_in.shape[1] - a_cols
    pad_low = LOW_PAD - n_low
    tail = (-(3 * aw + b_cols + LOW_PAD)) % PROJ_TN
    w_cat = jnp.concatenate([w_in[:, :3 * aw], w_in[:, a_cols:], w_in[:, 3 * aw:a_cols],
                             jnp.zeros((w_in.shape[0], pad_low + tail), F32)], axis=1).astype(BF16)
    b_cat = jnp.concatenate([jnp.zeros((3 * aw,), F32), b_qkv, jnp.zeros((LOW_PAD + tail,), F32)])
    z = _proj(x, g, w_cat, b_cat, gelu=False)

    mu_l = jnp.pad(mu[3 * aw:], (0, pad_low))
    row = lambda t: t.reshape(1, -1)
    w2p = _pad_rows(w2, 0, LOW_PAD).astype(BF16)
    a2p = _pad_rows(a2, DECAY_RANK, LOW_PAD).astype(BF16)
    g2p = _pad_rows(g2, DECAY_RANK + ICLR_RANK, LOW_PAD).astype(BF16)
    if vres is not None:
        v0, v1, v2 = vres
        rank = v1.shape[1]
        vres_args = (v_first, row(v0), jnp.pad(v1, ((0, 0), (0, VRES_PAD - rank))).astype(BF16),
                     _pad_rows(v2, 0, VRES_PAD).astype(BF16))
    else:
        vres_args = None
    outs = _rwkv(z, batch, seq_len, row(mu[:aw]), row(mu[aw:2 * aw]), row(mu[2 * aw:3 * aw]), row(mu_l),
                 row(w0), w2p, row(a0), a2p, g2p, row(k_k), row(k_a), row(r_k.reshape(-1)),
                 row(gn_g), row(gn_b), vres_args, emit_v=vres is None)
    oa, v = outs if vres is None else (outs[0], None)
    ob = _swa(z, sinks, batch, seq_len)
    wo = w_out.astype(BF16)
    x = _mix_out(x, oa, ob, wo[:aw], wo[aw:])
    return x, v


def _odd_layer(x, g, w_in, vn_g, w_s, b_s, w_out):
    z = _proj(x, g, w_in.astype(BF16), jnp.zeros((w_in.shape[1],), F32), gelu=True)
    ws = jnp.tril(w_s).astype(BF16)
    return _sgu(x, z, vn_g, ws, b_s[:, :, None], w_out.astype(BF16))


def kernel(x, norm_g, ffn_wg, ffn_wu, ffn_wd, e_w_in, e_b_qkv, e_mu, e_w0, e_w2, e_a0, e_a2, e_g2, e_k_k, e_k_a, e_r_k, e_gn_g, e_gn_b, e_sinks, e_w_out, vres_v0, vres_v1, vres_v2, o_w_in, o_vn_g, o_w_s, o_b_s, o_w_out, final_g):
    batch, seq_len, d = x.shape
    depth = norm_g.shape[0]
    x = x.reshape(batch * seq_len, d)
    n_ffn = 2 * depth
    ffn_norm = lambda k: norm_g[k // 2, 2 * (k % 2)]
    weights = tuple(w[0, 0].astype(BF16) for w in (ffn_wg, ffn_wu, ffn_wd))

    def ffn(x, k, weights):
        nxt = (ffn_wg, ffn_wu, ffn_wd, (k + 1) // 2, (k + 1) % 2) if k + 1 < n_ffn else None
        x, cast = _ffn(x, ffn_norm(k), *weights, g_out=final_g if k + 1 == n_ffn else None, cast_next=nxt)
        return x, cast

    v_first = None
    for layer in range(depth):
        x, weights = ffn(x, 2 * layer, weights)
        if layer % 2 == 0:
            i = layer // 2
            vres = None if i == 0 else (vres_v0[i - 1], vres_v1[i - 1], vres_v2[i - 1])
            x, v_a = _even_layer(x, batch, seq_len, norm_g[layer, 1], e_w_in[i], e_b_qkv[i], e_mu[i],
                                 e_w0[i], e_w2[i], e_a0[i], e_a2[i], e_g2[i], e_k_k[i], e_k_a[i],
                                 e_r_k[i], e_gn_g[i], e_gn_b[i], e_sinks[i], e_w_out[i], v_first, vres)
            if i == 0:
                v_first = v_a
        else:
            j = layer // 2
            x = _odd_layer(x, norm_g[layer, 1], o_w_in[j], o_vn_g[j], o_w_s[j], o_b_s[j], o_w_out[j])
        x, weights = ffn(x, 2 * layer + 1, weights)
    return x.reshape(batch, seq_len, d)
```

```python
import functools
import math

import jax
import jax.numpy as jnp
from jax import lax
from jax.experimental import pallas as pl
from jax.experimental.pallas import tpu as pltpu

F32 = jnp.float32
BF16 = jnp.bfloat16

NORM_EPS = 1e-6
GN_EPS = 64e-5
HEAD_DIM = 64
A_HEADS = 16
A_WIDTH = A_HEADS * HEAD_DIM
B_HEADS = 16
B_KV_HEADS = 4
B_GROUP = B_HEADS // B_KV_HEADS
WINDOW = 128
DECAY_RANK = 64
ICLR_RANK = 64
GATE_RANK = 160
LOW_PAD = 384
VRES_PAD = 128
C_GROUPS = 16
C_CHUNK = 128
SCAN_CHUNK = 64
BF16_TILE_ROWS = 16
PROJ_TN = 1024

VMEM_LIMIT = 60 * 1024 * 1024


def _cparams(sem):
    return pltpu.CompilerParams(dimension_semantics=sem, vmem_limit_bytes=VMEM_LIMIT)


def _dot(a, b):
    return jnp.dot(a, b, preferred_element_type=F32)


def _dot_nt(a, b):
    return lax.dot_general(a, b, (((1,), (1,)), ((), ())), preferred_element_type=F32)


def _dot_tn(a, b):
    return lax.dot_general(a, b, (((0,), (0,)), ((), ())), preferred_element_type=F32)


def _split_dot(x, ones_bf16, left=False):
    x1 = x.astype(BF16)
    r1 = x - x1.astype(F32)
    x2 = r1.astype(BF16)
    x3 = (r1 - x2.astype(F32)).astype(BF16)
    if left:
        return _dot(ones_bf16, x1) + _dot(ones_bf16, x2) + _dot(ones_bf16, x3)
    return _dot(x1, ones_bf16) + _dot(x2, ones_bf16) + _dot(x3, ones_bf16)


def _rms(x, g):
    return x * lax.rsqrt(jnp.mean(x * x, axis=-1, keepdims=True) + NORM_EPS) * g


def _ffn_kernel(*refs, final_norm, cast_next):
    x_ref, g_ref = refs[:2]
    refs = refs[2:]
    if final_norm:
        gout_ref = refs[0]
        refs = refs[1:]
    wg_ref, wu_ref, wd_ref = refs[:3]
    refs = refs[3:]
    if cast_next:
        next_f32 = refs[:3]
        refs = refs[3:]
    o_ref = refs[0]
    refs = refs[1:]
    if cast_next:
        next_bf16 = refs[:3]
        refs = refs[3:]
    (h_ref,) = refs
    f = pl.program_id(1)

    def half_swiglu(h):
        a = _dot(h, wg_ref[...])
        b = _dot(h, wu_ref[...])
        p = (0.5 * a * jax.nn.sigmoid(a) * b).astype(BF16)
        return _dot(p, wd_ref[...])

    def cast_slice():
        if cast_next:
            for src, dst in zip(next_f32, next_bf16):
                dst[...] = src[...].astype(BF16)

    @pl.when(f == 0)
    def _():
        cast_slice()
        half = x_ref.shape[0] // 2
        for s in range(2):
            rows = pl.ds(s * half, half)
            x = x_ref[rows, :]
            h = _rms(x, g_ref[...]).astype(BF16)
            h_ref[rows, :] = h
            o_ref[rows, :] = x + half_swiglu(h)

    @pl.when(f > 0)
    def _():
        cast_slice()
        o_ref[...] += half_swiglu(h_ref[...])

    if final_norm:
        @pl.when(f == pl.num_programs(1) - 1)
        def _():
            o_ref[...] = _rms(o_ref[...], gout_ref[...])


def _ffn(x, g, wg, wu, wd, g_out=None, cast_next=None, tm=1024, tf=512):
    n, d = x.shape
    fdim = wg.shape[1]
    ni, nf = n // tm, fdim // tf
    vec = pl.BlockSpec((1, d), lambda i, f: (0, 0))
    in_specs = [pl.BlockSpec((tm, d), lambda i, f: (i, 0)), vec]
    args = [x, g.reshape(1, d)]
    if g_out is not None:
        in_specs.append(vec)
        args.append(g_out.reshape(1, d))
    in_specs += [
        pl.BlockSpec((d, tf), lambda i, f: (0, f)),
        pl.BlockSpec((d, tf), lambda i, f: (0, f)),
        pl.BlockSpec((tf, d), lambda i, f: (f, 0)),
    ]
    args += [wg, wu, wd]
    out_specs = [pl.BlockSpec((tm, d), lambda i, f: (i, 0))]
    out_shape = [jax.ShapeDtypeStruct((n, d), F32)]
    if cast_next is not None:
        wg4, wu4, wd4, l, j = cast_next
        up_rows, down_rows = d // ni, fdim // (ni * nf)
        assert up_rows * ni == d and down_rows * ni * nf == fdim
        assert up_rows % BF16_TILE_ROWS == 0 and down_rows % BF16_TILE_ROWS == 0
        in_specs += [
            pl.BlockSpec((None, None, up_rows, tf), lambda i, f: (l, j, i, f)),
            pl.BlockSpec((None, None, up_rows, tf), lambda i, f: (l, j, i, f)),
            pl.BlockSpec((None, None, down_rows, d), lambda i, f: (l, j, i * nf + f, 0)),
        ]
        args += [wg4, wu4, wd4]
        out_specs += [
            pl.BlockSpec((up_rows, tf), lambda i, f: (i, f)),
            pl.BlockSpec((up_rows, tf), lambda i, f: (i, f)),
            pl.BlockSpec((down_rows, d), lambda i, f: (i * nf + f, 0)),
        ]
        out_shape += [jax.ShapeDtypeStruct((d, fdim), BF16), jax.ShapeDtypeStruct((d, fdim), BF16),
                      jax.ShapeDtypeStruct((fdim, d), BF16)]
    outs = pl.pallas_call(
        functools.partial(_ffn_kernel, final_norm=g_out is not None, cast_next=cast_next is not None),
        grid=(ni, nf),
        in_specs=in_specs,
        out_specs=out_specs,
        out_shape=out_shape,
        scratch_shapes=[pltpu.VMEM((tm, d), BF16)],
        compiler_params=_cparams(("parallel", "arbitrary")),
        name="ffn",
    )(*args)
    return outs[0], tuple(outs[1:])


def _proj_kernel(x_ref, g_ref, w_ref, b_ref, o_ref, h_ref, *, tn):
    j = pl.program_id(1)

    cols = pl.ds(pl.multiple_of(j * tn, tn), tn)

    def project(h):
        return _dot(h, w_ref[:, cols]) + b_ref[:, cols]

    @pl.when(j == 0)
    def _():
        half = x_ref.shape[0] // 2
        for s in range(2):
            rows = pl.ds(s * half, half)
            h = _rms(x_ref[rows, :], g_ref[...]).astype(BF16)
            h_ref[rows, :] = h
            o_ref[rows, :] = project(h)

    @pl.when(j > 0)
    def _():
        o_ref[...] = project(h_ref[...])


def _proj(x, g, w, b, tm=1024, tn=PROJ_TN):
    n, d = x.shape
    cols = w.shape[1]
    resident = lambda shape: pl.BlockSpec(shape, lambda i, j: (0, 0), pipeline_mode=pl.Buffered(1))
    return pl.pallas_call(
        functools.partial(_proj_kernel, tn=tn),
        grid=(n // tm, cols // tn),
        in_specs=[
            pl.BlockSpec((tm, d), lambda i, j: (i, 0)),
            pl.BlockSpec((1, d), lambda i, j: (0, 0)),
            resident((d, cols)),
            resident((1, cols)),
        ],
        out_specs=pl.BlockSpec((tm, tn), lambda i, j: (i, j)),
        out_shape=jax.ShapeDtypeStruct((n, cols), F32),
        scratch_shapes=[pltpu.VMEM((tm, d), BF16)],
        compiler_params=_cparams(("parallel", "arbitrary")),
        name="proj",
    )(x, g.reshape(1, d), w, b.reshape(1, cols))


MXU_TILE = 256


def _head_sum(x):
    rh = lax.broadcasted_iota(jnp.int32, (MXU_TILE, MXU_TILE), 0) // HEAD_DIM
    ch = lax.broadcasted_iota(jnp.int32, (MXU_TILE, MXU_TILE), 1) // HEAD_DIM
    ones = jnp.where(rh == ch, 1.0, 0.0).astype(BF16)
    hi = x.astype(BF16)
    lo = (x - hi.astype(F32)).astype(BF16)
    blocks = [slice(s, s + MXU_TILE) for s in range(0, x.shape[1], MXU_TILE)]
    return jnp.concatenate([_dot(hi[:, s], ones) + _dot(lo[:, s], ones) for s in blocks], axis=1)


def _rwkv_kernel(*refs, rows, chunk, vres, emit_v):
    (zr_ref, zk_ref, zv_ref, zl_ref, pr_ref, pk_ref, pv_ref, plo_ref,
     mur_ref, muk_ref, muv_ref, mul_ref, w0_ref, w2_ref, a0_ref, a2_ref, g2_ref,
     kk_ref, ka_ref, rk_ref, gng_ref, gnb_ref) = refs[:22]
    refs = refs[22:]
    if vres:
        vf_ref, v0_ref, v1_ref, v2_ref = refs[:4]
        refs = refs[4:]
    oa_out = refs[0]
    refs = refs[1:]
    if emit_v:
        v_out = refs[0]
        refs = refs[1:]
    h_ref, r_sc, lw_sc, k_sc, v_sc, kk_sc, b_sc, g_sc, bonus_sc, y_sc = refs

    first = pl.program_id(1) == 0

    @pl.when(first)
    def _():
        h_ref[...] = jnp.zeros_like(h_ref)

    keep_prev = jnp.where(first, 0.0, 1.0)
    row = lax.broadcasted_iota(jnp.int32, (rows, 1), 0)

    def shifted(z_ref, p_ref, mu_ref):
        z = z_ref[...]
        prev_row = p_ref[7:8, :] * keep_prev
        prev = jnp.where(row == 0, prev_row, pltpu.roll(z, 1, axis=0))
        return z + (prev - z) * mu_ref[...]

    r = shifted(zr_ref, pr_ref, mur_ref)
    k = shifted(zk_ref, pk_ref, muk_ref)
    v = shifted(zv_ref, pv_ref, muv_ref)
    low = shifted(zl_ref, plo_ref, mul_ref)

    if vres:
        mix = _dot(_dot(v.astype(BF16), v1_ref[...]).astype(BF16), v2_ref[...])
        v = v + (vf_ref[...] - v) * jax.nn.sigmoid(v0_ref[...] + mix)

    dw = w0_ref[...] + _dot(jnp.tanh(low).astype(BF16), w2_ref[...])
    lw_sc[...] = -math.exp(-0.5) * jax.nn.sigmoid(dw)
    a = jax.nn.sigmoid(a0_ref[...] + _dot(low.astype(BF16), a2_ref[...]))
    g_sc[...] = _dot(jax.nn.sigmoid(low).astype(BF16), g2_ref[...])

    kk = k * kk_ref[...]
    kk = kk * jnp.minimum(lax.rsqrt(_head_sum(kk * kk)), 1e12)
    k = k * (1.0 + (a - 1.0) * ka_ref[...])
    r_sc[...] = r
    k_sc[...] = k
    v_sc[...] = v
    if emit_v:
        v_out[...] = v
    kk_sc[...] = kk
    b_sc[...] = kk * a
    bonus_sc[...] = _head_sum(r * k * rk_ref[...]) * v

    _scan_tile((r_sc, lw_sc, k_sc, v_sc, kk_sc, b_sc), h_ref, y_sc, chunk)

    y = y_sc[...]
    yc = y - _head_sum(y) * (1.0 / HEAD_DIM)
    var = _head_sum(yc * yc) * (1.0 / HEAD_DIM)
    yn = yc * lax.rsqrt(var + GN_EPS) * gng_ref[...] + gnb_ref[...]
    oa_out[...] = ((yn + bonus_sc[...]) * g_sc[...]).astype(oa_out.dtype)


def _rwkv(z, batch, seq_len, mu_r, mu_k, mu_v, mu_l, w0, w2p, a0, a2p, g2p, k_k, k_a, r_k,
          gn_g, gn_b, vres, emit_v, rows=256, chunk=SCAN_CHUNK):
    n = z.shape[0]
    aw = A_WIDTH
    low_blk = (3 * aw + B_HEADS * HEAD_DIM + 2 * B_KV_HEADS * HEAD_DIM) // LOW_PAD
    nr = seq_len // rows
    pstep = rows // 8

    def cur(col):
        return lambda bi, j: (bi * nr + j, col)

    def prev_map(col):
        return lambda bi, j: (jnp.maximum((bi * nr + j) * pstep - 1, 0), col)

    row_vec = lambda w: pl.BlockSpec((1, w), lambda bi, j: (0, 0))
    full = lambda a: pl.BlockSpec(a.shape, lambda bi, j: (0, 0))
    in_specs = [
        pl.BlockSpec((rows, aw), cur(0)),
        pl.BlockSpec((rows, aw), cur(1)),
        pl.BlockSpec((rows, aw), cur(2)),
        pl.BlockSpec((rows, LOW_PAD), cur(low_blk)),
        pl.BlockSpec((8, aw), prev_map(0)),
        pl.BlockSpec((8, aw), prev_map(1)),
        pl.BlockSpec((8, aw), prev_map(2)),
        pl.BlockSpec((8, LOW_PAD), prev_map(low_blk)),
        row_vec(aw), row_vec(aw), row_vec(aw), row_vec(LOW_PAD),
        row_vec(aw), full(w2p), row_vec(aw), full(a2p), full(g2p),
        row_vec(aw), row_vec(aw), row_vec(aw), row_vec(aw), row_vec(aw),
    ]
    args = [z, z, z, z, z, z, z, z, mu_r, mu_k, mu_v, mu_l, w0, w2p, a0, a2p, g2p, k_k, k_a, r_k,
            gn_g, gn_b]
    if vres is not None:
        v_first, v0, v1p, v2p = vres
        in_specs += [pl.BlockSpec((rows, aw), cur(0)), row_vec(aw), full(v1p), full(v2p)]
        args += [v_first, v0, v1p, v2p]
    out_specs = [pl.BlockSpec((rows, aw), cur(0))]
    out_shape = [jax.ShapeDtypeStruct((n, aw), BF16)]
    if emit_v:
        out_specs.append(pl.BlockSpec((rows, aw), cur(0)))
        out_shape.append(jax.ShapeDtypeStruct((n, aw), F32))
    state = pltpu.VMEM((A_HEADS // 2, 2 * HEAD_DIM, 2 * HEAD_DIM), F32)
    return pl.pallas_call(
        functools.partial(_rwkv_kernel, rows=rows, chunk=chunk, vres=vres is not None, emit_v=emit_v),
        grid=(batch, nr),
        in_specs=in_specs,
        out_specs=out_specs,
        out_shape=out_shape,
        scratch_shapes=[state] + [pltpu.VMEM((rows, aw), F32)] * 9,
        compiler_params=_cparams(("parallel", "arbitrary")),
        name="rwkv",
    )(*args)


def _scan_tile(refs, h_ref, y_ref, chunk, group=16):
    c = chunk
    hd = HEAD_DIM
    assert c == hd, "the [power | inverse] lane layout below needs chunk == head width"
    pw = 2 * hd
    npair = A_HEADS // 2
    nchunk = refs[0].shape[0] // c
    pairs = [slice(p * pw, (p + 1) * pw) for p in range(npair)]
    every = [(ci, p) for ci in range(nchunk) for p in range(npair)]

    ri = lax.broadcasted_iota(jnp.int32, (c, c), 0)
    ci_ = lax.broadcasted_iota(jnp.int32, (c, c), 1)
    tri_ones = jnp.where(ri >= ci_, 1.0, 0.0).astype(BF16)
    row = lax.broadcasted_iota(jnp.int32, (c, pw), 0)
    lane = lax.broadcasted_iota(jnp.int32, (c, pw), 1)
    left = lane < hd
    col = jnp.where(left, lane, lane - hd)
    strict = row > col
    incl = row >= col
    eye_pair = jnp.where(row == col, 1.0, 0.0)
    couples = {s: (row // (2 * s) == col // (2 * s)) & (row // s != col // s) & strict
               for s in [2 ** e for e in range(int(math.log2(c)))]}

    def pair_diag(t):
        return jnp.concatenate([jnp.where(left, t, 0.0), jnp.where(left, 0.0, t)], axis=0).astype(BF16)

    left2 = lax.broadcasted_iota(jnp.int32, (2 * c, pw), 1) < hd
    kr = lax.broadcasted_iota(jnp.int32, (pw, pw), 0)
    kc = lax.broadcasted_iota(jnp.int32, (pw, pw), 1)
    same_head = (kr < hd) == (kc < hd)
    zeros_cv = jnp.zeros((c, pw), BF16)

    ar, bk, bke, vv, p_all = {}, {}, {}, {}, []
    for ci in range(nchunk):
        rs = pl.ds(ci * c, c)
        r, lw, k, v, kk, b = (ref[rs, :] for ref in refs)
        cum = _split_dot(lw, tri_ones, left=True)
        p_inv = jnp.exp(-cum)
        tail = cum[c - 1:c, :]
        p_tail = jnp.exp(tail - cum)
        p_all.append(jnp.exp(tail))
        a_t = (-kk * jnp.exp(cum - lw)).astype(BF16)
        r_t = (r * jnp.exp(cum)).astype(BF16)
        b_t = (b * p_inv).astype(BF16)
        k_t = (k * p_inv).astype(BF16)
        b_e = (b * p_tail).astype(BF16)
        k_e = (k * p_tail).astype(BF16)
        vb = v.astype(BF16)
        for p, s in enumerate(pairs):
            ar[ci, p] = jnp.concatenate([a_t[:, s], r_t[:, s]], axis=0)
            bk[ci, p] = jnp.concatenate([b_t[:, s], k_t[:, s]], axis=0)
            bke[ci, p] = jnp.concatenate([b_e[:, s], k_e[:, s]], axis=0)
            vv[ci, p] = vb[:, s]

    mb, lv, inv = {}, {}, {}
    for g in range(0, len(every), group):
        grp = every[g:g + group]
        sc = {i: _dot_nt(jnp.concatenate([jnp.where(left2, ar[i], 0), jnp.where(left2, 0, ar[i])], axis=0),
                         bk[i]) for i in grp}
        lt = {i: [jnp.where(strict, sc[i][q * 2 * c:q * 2 * c + c], 0.0) for q in range(2)] for i in grp}
        for i in grp:
            mb[i] = jnp.concatenate([jnp.where(incl, sc[i][q * 2 * c + c:(q + 1) * 2 * c], 0.0).astype(BF16)
                                     for q in range(2)], axis=0)
        lvf = {i: _dot(jnp.concatenate(lt[i], axis=0).astype(BF16),
                       jnp.concatenate([zeros_cv, vv[i]], axis=0)) for i in grp}
        for i in grp:
            lv[i] = jnp.where(left, lvf[i][:c], lvf[i][c:])
        lp = {i: jnp.where(left, lt[i][0], pltpu.roll(lt[i][1], hd, axis=1)) for i in grp}
        t = {i: jnp.where(couples[1], lp[i], eye_pair) for i in grp}
        size = 2
        while size < c:
            g = {i: _dot(t[i].astype(BF16), pair_diag(jnp.where(couples[size], lp[i], 0.0))) for i in grp}
            t = {i: t[i] + _dot(g[i].astype(BF16), pair_diag(t[i])) for i in grp}
            size *= 2
        for i in grp:
            inv[i] = t[i].astype(BF16)

    h = [h_ref[p] for p in range(npair)]
    for ci in range(nchunk):
        here = [(ci, p) for p in range(npair)]
        ah = [_dot(ar[i], h[p].astype(BF16)) for p, i in enumerate(here)]
        rhs = [ah[p][:c] + lv[i] for p, i in enumerate(here)]
        rhs = [pair_diag(t) for t in rhs]
        uv = [jnp.concatenate([_dot(inv[i], rhs[p]).astype(BF16), vv[i]], axis=0)
              for p, i in enumerate(here)]
        yy = [_dot(mb[i], uv[p]) for p, i in enumerate(here)]
        for p in range(npair):
            y_ref[pl.ds(ci * c, c), pairs[p]] = ah[p][c:] + jnp.where(left, yy[p][:c], yy[p][c:])
        hn = [_dot_tn(bke[i], uv[p]) for p, i in enumerate(here)]
        h = [jnp.transpose(jnp.broadcast_to(p_all[ci][:, pairs[p]], (pw, pw))) * h[p]
             + jnp.where(same_head, hn[p], 0.0) for p in range(npair)]
    for p in range(npair):
        h_ref[p] = h[p]


def _swa_kernel(sink_ref, q_ref, kc_ref, kp_ref, vc_ref, vp_ref, o_ref):
    blk = WINDOW
    r = lax.broadcasted_iota(jnp.int32, (blk, blk), 0)
    qi = lax.broadcasted_iota(jnp.int32, (blk, blk), 1)
    from_prev = r > qi
    no_prev = jnp.where(pl.program_id(1) > 0, 0.0, -1e32)
    neg_dist = jnp.where(from_prev, (r - qi - blk).astype(F32) + no_prev, (r - qi).astype(F32))
    scale = HEAD_DIM ** -0.5
    assert math.frexp(scale)[0] == 0.5, "a power-of-two scale commutes with the bf16 rounding of q"
    kband = jnp.concatenate([kp_ref[...], kc_ref[...]], axis=0).astype(BF16)
    vband = jnp.concatenate([vp_ref[...], vc_ref[...]], axis=0).astype(BF16)
    q = (q_ref[...] * scale).astype(BF16)
    heads = range(B_HEADS)
    head_cols = lambda t, i: t[:, i * HEAD_DIM:(i + 1) * HEAD_DIM]
    qk = [_dot_nt(head_cols(kband, h // B_GROUP), head_cols(q, h)) for h in heads]
    qk = [jnp.where(from_prev, t[:blk], t[blk:]) for t in qk]
    s = [qk[h] + 2.0 ** (-8.0 * (h + 1) / B_HEADS) * neg_dist for h in heads]
    m = [jnp.maximum(jnp.max(s[h], axis=0, keepdims=True), sink_ref[h]) for h in heads]
    p = [jnp.exp(s[h] - m[h]) for h in heads]
    den = [jnp.sum(p[h], axis=0, keepdims=True) + jnp.exp(sink_ref[h] - m[h]) for h in heads]
    p = [jnp.concatenate([jnp.where(from_prev, t, 0.0), jnp.where(from_prev, 0.0, t)], axis=0).astype(BF16)
         for t in p]
    pv = [_dot_tn(head_cols(vband, h // B_GROUP), p[h]) for h in heads]
    for h in heads:
        o_ref[h * HEAD_DIM:(h + 1) * HEAD_DIM, :] = (pv[h] / den[h]).astype(o_ref.dtype)


def _swa(z, sinks, batch, seq_len):
    n = z.shape[0]
    nb = seq_len // WINDOW
    qw = B_HEADS * HEAD_DIM
    kvw = B_KV_HEADS * HEAD_DIM
    q_blk = 3 * A_WIDTH // qw
    k_blk = (3 * A_WIDTH + qw) // kvw
    cur = lambda col: (lambda bi, j: (bi * nb + j, col))
    prev = lambda col: (lambda bi, j: (bi * nb + jnp.maximum(j - 1, 0), col))
    return pl.pallas_call(
        _swa_kernel,
        grid=(batch, nb),
        in_specs=[
            pl.BlockSpec(memory_space=pltpu.SMEM),
            pl.BlockSpec((WINDOW, qw), cur(q_blk)),
            pl.BlockSpec((WINDOW, kvw), cur(k_blk)),
            pl.BlockSpec((WINDOW, kvw), prev(k_blk)),
            pl.BlockSpec((WINDOW, kvw), cur(k_blk + 1)),
            pl.BlockSpec((WINDOW, kvw), prev(k_blk + 1)),
        ],
        out_specs=pl.BlockSpec((qw, WINDOW), lambda bi, j: (0, bi * nb + j)),
        out_shape=jax.ShapeDtypeStruct((qw, n), BF16),
        compiler_params=_cparams(("parallel", "arbitrary")),
        name="swa",
    )(sinks, z, z, z, z, z)


def _mix_out_kernel(x_ref, oa_ref, obt_ref, wa_ref, wb_ref, o_ref):
    o_ref[...] = x_ref[...] + _dot(oa_ref[...], wa_ref[...]) + _dot_tn(obt_ref[...], wb_ref[...])


def _mix_out(x, oa, obt, wa, wb, tm=512):
    n, d = x.shape
    aw = oa.shape[1]
    bw = obt.shape[0]
    rows = lambda w: pl.BlockSpec((tm, w), lambda i: (i, 0))
    return pl.pallas_call(
        _mix_out_kernel,
        grid=(n // tm,),
        in_specs=[rows(d), rows(aw), pl.BlockSpec((bw, tm), lambda i: (0, i)),
                  pl.BlockSpec((aw, d), lambda i: (0, 0)), pl.BlockSpec((bw, d), lambda i: (0, 0))],
        out_specs=rows(d),
        out_shape=jax.ShapeDtypeStruct((n, d), F32),
        compiler_params=_cparams(("parallel",)),
        name="mix_out",
    )(x, oa, obt, wa, wb)


def _gmlp_kernel(x_ref, g_ref, win_ref, vg_ref, ws_ref, bs_ref, wo_ref, o_ref, u_ref, v_ref, gate_ref,
                 *, tm, tn):
    cwid = wo_ref.shape[0]
    h = _rms(x_ref[...], g_ref[...]).astype(BF16)
    for c0 in range(0, 2 * cwid, tn):
        z = _dot(h, win_ref[:, c0:c0 + tn])
        z = 0.5 * z * (1.0 + lax.erf(z * (2.0 ** -0.5)))
        if c0 < cwid:
            u_ref[:, c0:c0 + tn] = z
        else:
            v_ref[:, c0 - cwid:c0 - cwid + tn] = z
    vn = _rms(v_ref[...], vg_ref[...]).astype(BF16)
    cw = C_CHUNK
    for ch in range(tm // C_CHUNK):
        rows = slice(ch * C_CHUNK, (ch + 1) * C_CHUNK)
        for grp in range(C_GROUPS):
            cols = slice(grp * cw, (grp + 1) * cw)
            vm = _dot(ws_ref[grp], vn[rows, cols]) + bs_ref[grp]
            gate_ref[rows, cols] = (u_ref[rows, cols] * vm).astype(BF16)
    o_ref[...] = x_ref[...] + _dot(gate_ref[...], wo_ref[...])


def _gmlp(x, g, w_in, vn_g, ws, bs, wo, tm=512, tn=PROJ_TN):
    n, d = x.shape
    cwid = wo.shape[0]
    resident = lambda a: pl.BlockSpec(a.shape, lambda i: (0,) * a.ndim, pipeline_mode=pl.Buffered(1))
    vec = lambda w: pl.BlockSpec((1, w), lambda i: (0, 0))
    return pl.pallas_call(
        functools.partial(_gmlp_kernel, tm=tm, tn=tn),
        grid=(n // tm,),
        in_specs=[pl.BlockSpec((tm, d), lambda i: (i, 0)), vec(d), resident(w_in), vec(cwid),
                  resident(ws), resident(bs), resident(wo)],
        out_specs=pl.BlockSpec((tm, d), lambda i: (i, 0)),
        out_shape=jax.ShapeDtypeStruct((n, d), F32),
        scratch_shapes=[pltpu.VMEM((tm, cwid), F32), pltpu.VMEM((tm, cwid), F32), pltpu.VMEM((tm, cwid), BF16)],
        compiler_params=_cparams(("parallel",)),
        name="gmlp",
    )(x, g.reshape(1, d), w_in, vn_g.reshape(1, cwid), ws, bs, wo)


def _pad_rows(w, lo, total):
    return jnp.pad(w, ((lo, total - lo - w.shape[0]), (0, 0)))


def _even_layer(x, batch, seq_len, g, w_in, b_qkv, mu, w0, w2, a0, a2, g2, k_k, k_a, r_k,
                gn_g, gn_b, sinks, w_out, v_first, vres):
    aw = A_WIDTH
    a_cols = 3 * aw + DECAY_RANK + ICLR_RANK + GATE_RANK
    n_low = a_cols - 3 * aw
    b_cols = w_in.shape[1] - a_cols
    pad_low = LOW_PAD - n_low
    tail = (-(3 * aw + b_cols + LOW_PAD)) % PROJ_TN
    w_cat = jnp.concatenate([w_in[:, :3 * aw], w_in[:, a_cols:], w_in[:, 3 * aw:a_cols],
                             jnp.zeros((w_in.shape[0], pad_low + tail), F32)], axis=1).astype(BF16)
    b_cat = jnp.concatenate([jnp.zeros((3 * aw,), F32), b_qkv, jnp.zeros((LOW_PAD + tail,), F32)])
    z = _proj(x, g, w_cat, b_cat)

    mu_l = jnp.pad(mu[3 * aw:], (0, pad_low))
    row = lambda t: t.reshape(1, -1)
    w2p = _pad_rows(w2, 0, LOW_PAD).astype(BF16)
    a2p = _pad_rows(a2, DECAY_RANK, LOW_PAD).astype(BF16)
    g2p = _pad_rows(g2, DECAY_RANK + ICLR_RANK, LOW_PAD).astype(BF16)
    if vres is not None:
        v0, v1, v2 = vres
        rank = v1.shape[1]
        vres_args = (v_first, row(v0), jnp.pad(v1, ((0, 0), (0, VRES_PAD - rank))).astype(BF16),
                     _pad_rows(v2, 0, VRES_PAD).astype(BF16))
    else:
        vres_args = None
    outs = _rwkv(z, batch, seq_len, row(mu[:aw]), row(mu[aw:2 * aw]), row(mu[2 * aw:3 * aw]), row(mu_l),
                 row(w0), w2p, row(a0), a2p, g2p, row(k_k), row(k_a), row(r_k.reshape(-1)),
                 row(gn_g), row(gn_b), vres_args, emit_v=vres is None)
    oa, v = outs if vres is None else (outs[0], None)
    ob = _swa(z, sinks, batch, seq_len)
    wo = w_out.astype(BF16)
    x = _mix_out(x, oa, ob, wo[:aw], wo[aw:])
    return x, v


def _odd_layer(x, g, w_in, vn_g, w_s, b_s, w_out):
    ws = jnp.tril(w_s).astype(BF16)
    return _gmlp(x, g, w_in.astype(BF16), vn_g, ws, b_s[:, :, None], w_out.astype(BF16))


def kernel(x, norm_g, ffn_wg, ffn_wu, ffn_wd, e_w_in, e_b_qkv, e_mu, e_w0, e_w2, e_a0, e_a2, e_g2, e_k_k, e_k_a, e_r_k, e_gn_g, e_gn_b, e_sinks, e_w_out, vres_v0, vres_v1, vres_v2, o_w_in, o_vn_g, o_w_s, o_b_s, o_w_out, final_g):
    batch, seq_len, d = x.shape
    depth = norm_g.shape[0]
    x = x.reshape(batch * seq_len, d)
    n_ffn = 2 * depth
    ffn_norm = lambda k: norm_g[k // 2, 2 * (k % 2)]
    weights = tuple(w[0, 0].astype(BF16) for w in (ffn_wg, ffn_wu, ffn_wd))

    def ffn(x, k, weights):
        nxt = (ffn_wg, ffn_wu, ffn_wd, (k + 1) // 2, (k + 1) % 2) if k + 1 < n_ffn else None
        x, cast = _ffn(x, ffn_norm(k), *weights, g_out=final_g if k + 1 == n_ffn else None, cast_next=nxt)
        return x, cast

    v_first = None
    for layer in range(depth):
        x, weights = ffn(x, 2 * layer, weights)
        if layer % 2 == 0:
            i = layer // 2
            vres = None if i == 0 else (vres_v0[i - 1], vres_v1[i - 1], vres_v2[i - 1])
            x, v_a = _even_layer(x, batch, seq_len, norm_g[layer, 1], e_w_in[i], e_b_qkv[i], e_mu[i],
                                 e_w0[i], e_w2[i], e_a0[i], e_a2[i], e_g2[i], e_k_k[i], e_k_a[i],
                                 e_r_k[i], e_gn_g[i], e_gn_b[i], e_sinks[i], e_w_out[i], v_first, vres)
            if i == 0:
                v_first = v_a
        else:
            j = layer // 2
            x = _odd_layer(x, norm_g[layer, 1], o_w_in[j], o_vn_g[j], o_w_s[j], o_b_s[j], o_w_out[j])
        x, weights = ffn(x, 2 * layer + 1, weights)
    return x.reshape(batch, seq_len, d)
```

```python
import functools
import math

import jax
import jax.numpy as jnp
from jax import lax
from jax.experimental import pallas as pl
from jax.experimental.pallas import tpu as pltpu

F32 = jnp.float32
BF16 = jnp.bfloat16

NORM_EPS = 1e-6
GN_EPS = 64e-5
HEAD_DIM = 64
A_HEADS = 16
A_WIDTH = A_HEADS * HEAD_DIM
B_HEADS = 16
B_KV_HEADS = 4
B_GROUP = B_HEADS // B_KV_HEADS
WINDOW = 128
DECAY_RANK = 64
ICLR_RANK = 64
GATE_RANK = 160
LOW_PAD = 384
VRES_PAD = 128
C_GROUPS = 16
C_CHUNK = 128
SCAN_CHUNK = 64
BF16_TILE_ROWS = 16
PROJ_TN = 1024

VMEM_LIMIT = 60 * 1024 * 1024


def _cparams(sem):
    return pltpu.CompilerParams(dimension_semantics=sem, vmem_limit_bytes=VMEM_LIMIT)


def _dot(a, b):
    return jnp.dot(a, b, preferred_element_type=F32)


def _dot_nt(a, b):
    return lax.dot_general(a, b, (((1,), (1,)), ((), ())), preferred_element_type=F32)


def _dot_tn(a, b):
    return lax.dot_general(a, b, (((0,), (0,)), ((), ())), preferred_element_type=F32)


def _split_dot(x, ones_bf16, left=False):
    x1 = x.astype(BF16)
    r1 = x - x1.astype(F32)
    x2 = r1.astype(BF16)
    x3 = (r1 - x2.astype(F32)).astype(BF16)
    if left:
        return _dot(ones_bf16, x1) + _dot(ones_bf16, x2) + _dot(ones_bf16, x3)
    return _dot(x1, ones_bf16) + _dot(x2, ones_bf16) + _dot(x3, ones_bf16)


def _rms(x, g):
    return x * lax.rsqrt(jnp.mean(x * x, axis=-1, keepdims=True) + NORM_EPS) * g


def _ffn_kernel(*refs, final_norm, cast_next):
    x_ref, g_ref = refs[:2]
    refs = refs[2:]
    if final_norm:
        gout_ref = refs[0]
        refs = refs[1:]
    wg_ref, wu_ref, wd_ref = refs[:3]
    refs = refs[3:]
    if cast_next:
        next_f32 = refs[:3]
        refs = refs[3:]
    o_ref = refs[0]
    refs = refs[1:]
    if cast_next:
        next_bf16 = refs[:3]
        refs = refs[3:]
    (h_ref,) = refs
    f = pl.program_id(1)

    def half_swiglu(h):
        a = _dot(h, wg_ref[...])
        b = _dot(h, wu_ref[...])
        p = (0.5 * a * jax.nn.sigmoid(a) * b).astype(BF16)
        return _dot(p, wd_ref[...])

    def cast_slice():
        if cast_next:
            for src, dst in zip(next_f32, next_bf16):
                dst[...] = src[...].astype(BF16)

    @pl.when(f == 0)
    def _():
        cast_slice()
        half = x_ref.shape[0] // 2
        for s in range(2):
            rows = pl.ds(s * half, half)
            x = x_ref[rows, :]
            h = _rms(x, g_ref[...]).astype(BF16)
            h_ref[rows, :] = h
            o_ref[rows, :] = x + half_swiglu(h)

    @pl.when(f > 0)
    def _():
        cast_slice()
        o_ref[...] += half_swiglu(h_ref[...])

    if final_norm:
        @pl.when(f == pl.num_programs(1) - 1)
        def _():
            o_ref[...] = _rms(o_ref[...], gout_ref[...])


def _ffn(x, g, wg, wu, wd, g_out=None, cast_next=None, tm=1024, tf=512):
    n, d = x.shape
    fdim = wg.shape[1]
    ni, nf = n // tm, fdim // tf
    vec = pl.BlockSpec((1, d), lambda i, f: (0, 0))
    in_specs = [pl.BlockSpec((tm, d), lambda i, f: (i, 0)), vec]
    args = [x, g.reshape(1, d)]
    if g_out is not None:
        in_specs.append(vec)
        args.append(g_out.reshape(1, d))
    in_specs += [
        pl.BlockSpec((d, tf), lambda i, f: (0, f)),
        pl.BlockSpec((d, tf), lambda i, f: (0, f)),
        pl.BlockSpec((tf, d), lambda i, f: (f, 0)),
    ]
    args += [wg, wu, wd]
    out_specs = [pl.BlockSpec((tm, d), lambda i, f: (i, 0))]
    out_shape = [jax.ShapeDtypeStruct((n, d), F32)]
    if cast_next is not None:
        wg4, wu4, wd4, l, j = cast_next
        up_rows, down_rows = d // ni, fdim // (ni * nf)
        assert up_rows * ni == d and down_rows * ni * nf == fdim
        assert up_rows % BF16_TILE_ROWS == 0 and down_rows % BF16_TILE_ROWS == 0
        in_specs += [
            pl.BlockSpec((None, None, up_rows, tf), lambda i, f: (l, j, i, f)),
            pl.BlockSpec((None, None, up_rows, tf), lambda i, f: (l, j, i, f)),
            pl.BlockSpec((None, None, down_rows, d), lambda i, f: (l, j, i * nf + f, 0)),
        ]
        args += [wg4, wu4, wd4]
        out_specs += [
            pl.BlockSpec((up_rows, tf), lambda i, f: (i, f)),
            pl.BlockSpec((up_rows, tf), lambda i, f: (i, f)),
            pl.BlockSpec((down_rows, d), lambda i, f: (i * nf + f, 0)),
        ]
        out_shape += [jax.ShapeDtypeStruct((d, fdim), BF16), jax.ShapeDtypeStruct((d, fdim), BF16),
                      jax.ShapeDtypeStruct((fdim, d), BF16)]
    outs = pl.pallas_call(
        functools.partial(_ffn_kernel, final_norm=g_out is not None, cast_next=cast_next is not None),
        grid=(ni, nf),
        in_specs=in_specs,
        out_specs=out_specs,
        out_shape=out_shape,
        scratch_shapes=[pltpu.VMEM((tm, d), BF16)],
        compiler_params=_cparams(("parallel", "arbitrary")),
        name="ffn",
    )(*args)
    return outs[0], tuple(outs[1:])


def _proj_kernel(x_ref, g_ref, w_ref, b_ref, o_ref, h_ref, *, tn):
    j = pl.program_id(1)

    cols = pl.ds(pl.multiple_of(j * tn, tn), tn)

    def project(h):
        return _dot(h, w_ref[:, cols]) + b_ref[:, cols]

    @pl.when(j == 0)
    def _():
        half = x_ref.shape[0] // 2
        for s in range(2):
            rows = pl.ds(s * half, half)
            h = _rms(x_ref[rows, :], g_ref[...]).astype(BF16)
            h_ref[rows, :] = h
            o_ref[rows, :] = project(h)

    @pl.when(j > 0)
    def _():
        o_ref[...] = project(h_ref[...])


def _proj(x, g, w, b, tm=1024, tn=PROJ_TN):
    n, d = x.shape
    cols = w.shape[1]
    resident = lambda shape: pl.BlockSpec(shape, lambda i, j: (0, 0), pipeline_mode=pl.Buffered(1))
    return pl.pallas_call(
        functools.partial(_proj_kernel, tn=tn),
        grid=(n // tm, cols // tn),
        in_specs=[
            pl.BlockSpec((tm, d), lambda i, j: (i, 0)),
            pl.BlockSpec((1, d), lambda i, j: (0, 0)),
            resident((d, cols)),
            resident((1, cols)),
        ],
        out_specs=pl.BlockSpec((tm, tn), lambda i, j: (i, j)),
        out_shape=jax.ShapeDtypeStruct((n, cols), F32),
        scratch_shapes=[pltpu.VMEM((tm, d), BF16)],
        compiler_params=_cparams(("parallel", "arbitrary")),
        name="proj",
    )(x, g.reshape(1, d), w, b.reshape(1, cols))


MXU_TILE = 256


def _head_sum(x):
    rh = lax.broadcasted_iota(jnp.int32, (MXU_TILE, MXU_TILE), 0) // HEAD_DIM
    ch = lax.broadcasted_iota(jnp.int32, (MXU_TILE, MXU_TILE), 1) // HEAD_DIM
    ones = jnp.where(rh == ch, 1.0, 0.0).astype(BF16)
    hi = x.astype(BF16)
    lo = (x - hi.astype(F32)).astype(BF16)
    blocks = [slice(s, s + MXU_TILE) for s in range(0, x.shape[1], MXU_TILE)]
    return jnp.concatenate([_dot(hi[:, s], ones) + _dot(lo[:, s], ones) for s in blocks], axis=1)


def _rwkv_kernel(*refs, rows, chunk, vres, emit_v):
    (zr_ref, zk_ref, zv_ref, zl_ref, pr_ref, pk_ref, pv_ref, plo_ref,
     mur_ref, muk_ref, muv_ref, mul_ref, w0_ref, w2_ref, a0_ref, a2_ref, g2_ref,
     kk_ref, ka_ref, rk_ref, gng_ref, gnb_ref) = refs[:22]
    refs = refs[22:]
    if vres:
        vf_ref, v0_ref, v1_ref, v2_ref = refs[:4]
        refs = refs[4:]
    oa_out = refs[0]
    refs = refs[1:]
    if emit_v:
        v_out = refs[0]
        refs = refs[1:]
    h_ref, r_sc, lw_sc, k_sc, v_sc, kk_sc, b_sc, g_sc, bonus_sc, y_sc = refs

    first = pl.program_id(1) == 0

    @pl.when(first)
    def _():
        h_ref[...] = jnp.zeros_like(h_ref)

    keep_prev = jnp.where(first, 0.0, 1.0)
    row = lax.broadcasted_iota(jnp.int32, (rows, 1), 0)

    def shifted(z_ref, p_ref, mu_ref):
        z = z_ref[...]
        prev_row = p_ref[7:8, :] * keep_prev
        prev = jnp.where(row == 0, prev_row, pltpu.roll(z, 1, axis=0))
        return z + (prev - z) * mu_ref[...]

    r = shifted(zr_ref, pr_ref, mur_ref)
    k = shifted(zk_ref, pk_ref, muk_ref)
    v = shifted(zv_ref, pv_ref, muv_ref)
    low = shifted(zl_ref, plo_ref, mul_ref)

    if vres:
        mix = _dot(_dot(v.astype(BF16), v1_ref[...]).astype(BF16), v2_ref[...])
        v = v + (vf_ref[...] - v) * jax.nn.sigmoid(v0_ref[...] + mix)

    dw = w0_ref[...] + _dot(jnp.tanh(low).astype(BF16), w2_ref[...])
    lw_sc[...] = -math.exp(-0.5) * jax.nn.sigmoid(dw)
    a = jax.nn.sigmoid(a0_ref[...] + _dot(low.astype(BF16), a2_ref[...]))
    g_sc[...] = _dot(jax.nn.sigmoid(low).astype(BF16), g2_ref[...])

    kk = k * kk_ref[...]
    kk = kk * jnp.minimum(lax.rsqrt(_head_sum(kk * kk)), 1e12)
    k = k * (1.0 + (a - 1.0) * ka_ref[...])
    r_sc[...] = r
    k_sc[...] = k
    v_sc[...] = v
    if emit_v:
        v_out[...] = v
    kk_sc[...] = kk
    b_sc[...] = kk * a
    bonus_sc[...] = _head_sum(r * k * rk_ref[...]) * v

    _scan_tile((r_sc, lw_sc, k_sc, v_sc, kk_sc, b_sc), h_ref, y_sc, chunk)

    y = y_sc[...]
    yc = y - _head_sum(y) * (1.0 / HEAD_DIM)
    var = _head_sum(yc * yc) * (1.0 / HEAD_DIM)
    yn = yc * lax.rsqrt(var + GN_EPS) * gng_ref[...] + gnb_ref[...]
    oa_out[...] = ((yn + bonus_sc[...]) * g_sc[...]).astype(oa_out.dtype)


def _rwkv(z, batch, seq_len, mu_r, mu_k, mu_v, mu_l, w0, w2p, a0, a2p, g2p, k_k, k_a, r_k,
          gn_g, gn_b, vres, emit_v, rows=256, chunk=SCAN_CHUNK):
    n = z.shape[0]
    aw = A_WIDTH
    low_blk = (3 * aw + B_HEADS * HEAD_DIM + 2 * B_KV_HEADS * HEAD_DIM) // LOW_PAD
    nr = seq_len // rows
    pstep = rows // 8

    def cur(col):
        return lambda bi, j: (bi * nr + j, col)

    def prev_map(col):
        return lambda bi, j: (jnp.maximum((bi * nr + j) * pstep - 1, 0), col)

    row_vec = lambda w: pl.BlockSpec((1, w), lambda bi, j: (0, 0))
    full = lambda a: pl.BlockSpec(a.shape, lambda bi, j: (0, 0))
    in_specs = [
        pl.BlockSpec((rows, aw), cur(0)),
        pl.BlockSpec((rows, aw), cur(1)),
        pl.BlockSpec((rows, aw), cur(2)),
        pl.BlockSpec((rows, LOW_PAD), cur(low_blk)),
        pl.BlockSpec((8, aw), prev_map(0)),
        pl.BlockSpec((8, aw), prev_map(1)),
        pl.BlockSpec((8, aw), prev_map(2)),
        pl.BlockSpec((8, LOW_PAD), prev_map(low_blk)),
        row_vec(aw), row_vec(aw), row_vec(aw), row_vec(LOW_PAD),
        row_vec(aw), full(w2p), row_vec(aw), full(a2p), full(g2p),
        row_vec(aw), row_vec(aw), row_vec(aw), row_vec(aw), row_vec(aw),
    ]
    args = [z, z, z, z, z, z, z, z, mu_r, mu_k, mu_v, mu_l, w0, w2p, a0, a2p, g2p, k_k, k_a, r_k,
            gn_g, gn_b]
    if vres is not None:
        v_first, v0, v1p, v2p = vres
        in_specs += [pl.BlockSpec((rows, aw), cur(0)), row_vec(aw), full(v1p), full(v2p)]
        args += [v_first, v0, v1p, v2p]
    out_specs = [pl.BlockSpec((rows, aw), cur(0))]
    out_shape = [jax.ShapeDtypeStruct((n, aw), BF16)]
    if emit_v:
        out_specs.append(pl.BlockSpec((rows, aw), cur(0)))
        out_shape.append(jax.ShapeDtypeStruct((n, aw), F32))
    state = pltpu.VMEM((A_HEADS // 2, 2 * HEAD_DIM, 2 * HEAD_DIM), F32)
    return pl.pallas_call(
        functools.partial(_rwkv_kernel, rows=rows, chunk=chunk, vres=vres is not None, emit_v=emit_v),
        grid=(batch, nr),
        in_specs=in_specs,
        out_specs=out_specs,
        out_shape=out_shape,
        scratch_shapes=[state] + [pltpu.VMEM((rows, aw), F32)] * 9,
        compiler_params=_cparams(("parallel", "arbitrary")),
        name="rwkv",
    )(*args)


def _scan_tile(refs, h_ref, y_ref, chunk, group=16):
    c = chunk
    hd = HEAD_DIM
    assert c == hd, "the [power | inverse] lane layout below needs chunk == head width"
    pw = 2 * hd
    npair = A_HEADS // 2
    nchunk = refs[0].shape[0] // c
    pairs = [slice(p * pw, (p + 1) * pw) for p in range(npair)]
    every = [(ci, p) for ci in range(nchunk) for p in range(npair)]

    ri = lax.broadcasted_iota(jnp.int32, (c, c), 0)
    ci_ = lax.broadcasted_iota(jnp.int32, (c, c), 1)
    tri_ones = jnp.where(ri >= ci_, 1.0, 0.0).astype(BF16)
    row = lax.broadcasted_iota(jnp.int32, (c, pw), 0)
    lane = lax.broadcasted_iota(jnp.int32, (c, pw), 1)
    left = lane < hd
    col = jnp.where(left, lane, lane - hd)
    strict = row > col
    incl = row >= col
    eye_pair = jnp.where(row == col, 1.0, 0.0)
    couples = {s: (row // (2 * s) == col // (2 * s)) & (row // s != col // s) & strict
               for s in [2 ** e for e in range(int(math.log2(c)))]}

    def pair_diag(t):
        return jnp.concatenate([jnp.where(left, t, 0.0), jnp.where(left, 0.0, t)], axis=0).astype(BF16)

    left2 = lax.broadcasted_iota(jnp.int32, (2 * c, pw), 1) < hd
    kr = lax.broadcasted_iota(jnp.int32, (pw, pw), 0)
    kc = lax.broadcasted_iota(jnp.int32, (pw, pw), 1)
    same_head = (kr < hd) == (kc < hd)
    zeros_cv = jnp.zeros((c, pw), BF16)

    ar, bk, bke, vv, p_all = {}, {}, {}, {}, []
    for ci in range(nchunk):
        rs = pl.ds(ci * c, c)
        r, lw, k, v, kk, b = (ref[rs, :] for ref in refs)
        cum = _split_dot(lw, tri_ones, left=True)
        p_inv = jnp.exp(-cum)
        tail = cum[c - 1:c, :]
        p_tail = jnp.exp(tail - cum)
        p_all.append(jnp.exp(tail))
        a_t = (-kk * jnp.exp(cum - lw)).astype(BF16)
        r_t = (r * jnp.exp(cum)).astype(BF16)
        b_t = (b * p_inv).astype(BF16)
        k_t = (k * p_inv).astype(BF16)
        b_e = (b * p_tail).astype(BF16)
        k_e = (k * p_tail).astype(BF16)
        vb = v.astype(BF16)
        for p, s in enumerate(pairs):
            ar[ci, p] = jnp.concatenate([a_t[:, s], r_t[:, s]], axis=0)
            bk[ci, p] = jnp.concatenate([b_t[:, s], k_t[:, s]], axis=0)
            bke[ci, p] = jnp.concatenate([b_e[:, s], k_e[:, s]], axis=0)
            vv[ci, p] = vb[:, s]

    mb, lv, inv = {}, {}, {}
    for g in range(0, len(every), group):
        grp = every[g:g + group]
        sc = {i: _dot_nt(jnp.concatenate([jnp.where(left2, ar[i], 0), jnp.where(left2, 0, ar[i])], axis=0),
                         bk[i]) for i in grp}
        lt = {i: [jnp.where(strict, sc[i][q * 2 * c:q * 2 * c + c], 0.0) for q in range(2)] for i in grp}
        for i in grp:
            mb[i] = jnp.concatenate([jnp.where(incl, sc[i][q * 2 * c + c:(q + 1) * 2 * c], 0.0).astype(BF16)
                                     for q in range(2)], axis=0)
        lvf = {i: _dot(jnp.concatenate(lt[i], axis=0).astype(BF16),
                       jnp.concatenate([zeros_cv, vv[i]], axis=0)) for i in grp}
        for i in grp:
            lv[i] = jnp.where(left, lvf[i][:c], lvf[i][c:])
        lp = {i: jnp.where(left, lt[i][0], pltpu.roll(lt[i][1], hd, axis=1)) for i in grp}
        t = {i: jnp.where(couples[1], lp[i], eye_pair) for i in grp}
        size = 2
        while size < c:
            g = {i: _dot(t[i].astype(BF16), pair_diag(jnp.where(couples[size], lp[i], 0.0))) for i in grp}
            t = {i: t[i] + _dot(g[i].astype(BF16), pair_diag(t[i])) for i in grp}
            size *= 2
        for i in grp:
            inv[i] = t[i].astype(BF16)

    h = [h_ref[p] for p in range(npair)]
    for ci in range(nchunk):
        here = [(ci, p) for p in range(npair)]
        ah = [_dot(ar[i], h[p].astype(BF16)) for p, i in enumerate(here)]
        rhs = [ah[p][:c] + lv[i] for p, i in enumerate(here)]
        rhs = [pair_diag(t) for t in rhs]
        uv = [jnp.concatenate([_dot(inv[i], rhs[p]).astype(BF16), vv[i]], axis=0)
              for p, i in enumerate(here)]
        yy = [_dot(mb[i], uv[p]) for p, i in enumerate(here)]
        for p in range(npair):
            y_ref[pl.ds(ci * c, c), pairs[p]] = ah[p][c:] + jnp.where(left, yy[p][:c], yy[p][c:])
        hn = [_dot_tn(bke[i], uv[p]) for p, i in enumerate(here)]
        h = [jnp.transpose(jnp.broadcast_to(p_all[ci][:, pairs[p]], (pw, pw))) * h[p]
             + jnp.where(same_head, hn[p], 0.0) for p in range(npair)]
    for p in range(npair):
        h_ref[p] = h[p]


def _attn_block(sinks_ref, q, kprev, kcur, vprev, vcur, no_prev):
    blk = WINDOW
    r = lax.broadcasted_iota(jnp.int32, (blk, blk), 0)
    qi = lax.broadcasted_iota(jnp.int32, (blk, blk), 1)
    from_prev = r > qi
    neg_dist = jnp.where(from_prev, (r - qi - blk).astype(F32) + no_prev, (r - qi).astype(F32))
    scale = HEAD_DIM ** -0.5
    assert math.frexp(scale)[0] == 0.5, "a power-of-two scale commutes with the bf16 rounding of q"
    kband = jnp.concatenate([kprev, kcur], axis=0).astype(BF16)
    vband = jnp.concatenate([vprev, vcur], axis=0).astype(BF16)
    q = (q * scale).astype(BF16)
    heads = range(B_HEADS)
    head_cols = lambda t, i: t[:, i * HEAD_DIM:(i + 1) * HEAD_DIM]
    qk = [_dot_nt(head_cols(kband, h // B_GROUP), head_cols(q, h)) for h in heads]
    qk = [jnp.where(from_prev, t[:blk], t[blk:]) for t in qk]
    s = [qk[h] + 2.0 ** (-8.0 * (h + 1) / B_HEADS) * neg_dist for h in heads]
    m = [jnp.maximum(jnp.max(s[h], axis=0, keepdims=True), sinks_ref[h]) for h in heads]
    p = [jnp.exp(s[h] - m[h]) for h in heads]
    den = [jnp.sum(p[h], axis=0, keepdims=True) + jnp.exp(sinks_ref[h] - m[h]) for h in heads]
    p = [jnp.concatenate([jnp.where(from_prev, t, 0.0), jnp.where(from_prev, 0.0, t)], axis=0).astype(BF16)
         for t in p]
    pv = [_dot_tn(head_cols(vband, h // B_GROUP), p[h]) for h in heads]
    return [pv[h] / den[h] for h in heads]


def _attn_out_kernel(sinks_ref, x_ref, oa_ref, q_ref, kc_ref, kp_ref, vc_ref, vp_ref, wa_ref, wb_ref,
                     o_ref, obt_ref):
    blk = WINDOW
    acc = x_ref[...] + _dot(oa_ref[...], wa_ref[...])
    for j in range(x_ref.shape[0] // blk):
        rows = pl.ds(j * blk, blk)
        if j == 0:
            kprev, vprev = kp_ref[...], vp_ref[...]
            no_prev = jnp.where(pl.program_id(1) > 0, 0.0, -1e32)
        else:
            before = pl.ds((j - 1) * blk, blk)
            kprev, vprev = kc_ref[before, :], vc_ref[before, :]
            no_prev = 0.0
        outs = _attn_block(sinks_ref, q_ref[rows, :], kprev, kc_ref[rows, :], vprev, vc_ref[rows, :], no_prev)
        for h, o in enumerate(outs):
            obt_ref[h * HEAD_DIM:(h + 1) * HEAD_DIM, rows] = o.astype(obt_ref.dtype)
    o_ref[...] = acc + _dot_tn(obt_ref[...], wb_ref[...])


def _attn_out(x, z, oa, sinks, wa, wb, batch, seq_len, tm=512):
    n, d = x.shape
    aw = oa.shape[1]
    nt = seq_len // tm
    per = tm // WINDOW
    qw = B_HEADS * HEAD_DIM
    kvw = B_KV_HEADS * HEAD_DIM
    q_blk = 3 * A_WIDTH // qw
    k_blk = (3 * A_WIDTH + qw) // kvw
    tile = lambda w, col: pl.BlockSpec((tm, w), lambda bi, j: (bi * nt + j, col))
    prev = lambda col: pl.BlockSpec((WINDOW, kvw), lambda bi, j: (jnp.maximum((bi * nt + j) * per - 1, 0), col))
    resident = lambda a: pl.BlockSpec(a.shape, lambda bi, j: (0, 0), pipeline_mode=pl.Buffered(1))
    return pl.pallas_call(
        _attn_out_kernel,
        grid=(batch, nt),
        in_specs=[
            pl.BlockSpec(memory_space=pltpu.SMEM),
            tile(d, 0), tile(aw, 0), tile(qw, q_blk),
            tile(kvw, k_blk), prev(k_blk), tile(kvw, k_blk + 1), prev(k_blk + 1),
            resident(wa), resident(wb),
        ],
        out_specs=tile(d, 0),
        out_shape=jax.ShapeDtypeStruct((n, d), F32),
        scratch_shapes=[pltpu.VMEM((qw, tm), BF16)],
        compiler_params=_cparams(("parallel", "arbitrary")),
        name="attn_out",
    )(sinks, x, oa, z, z, z, z, z, wa, wb)


def _gmlp_kernel(x_ref, g_ref, win_ref, vg_ref, ws_ref, bs_ref, wo_ref, o_ref, u_ref, v_ref, gate_ref,
                 *, tm, tn):
    cwid = wo_ref.shape[0]
    h = _rms(x_ref[...], g_ref[...]).astype(BF16)
    for c0 in range(0, 2 * cwid, tn):
        z = _dot(h, win_ref[:, c0:c0 + tn])
        z = 0.5 * z * (1.0 + lax.erf(z * (2.0 ** -0.5)))
        if c0 < cwid:
            u_ref[:, c0:c0 + tn] = z
        else:
            v_ref[:, c0 - cwid:c0 - cwid + tn] = z
    vn = _rms(v_ref[...], vg_ref[...]).astype(BF16)
    cw = C_CHUNK
    for ch in range(tm // C_CHUNK):
        rows = slice(ch * C_CHUNK, (ch + 1) * C_CHUNK)
        for grp in range(C_GROUPS):
            cols = slice(grp * cw, (grp + 1) * cw)
            vm = _dot(ws_ref[grp], vn[rows, cols]) + bs_ref[grp]
            gate_ref[rows, cols] = (u_ref[rows, cols] * vm).astype(BF16)
    o_ref[...] = x_ref[...] + _dot(gate_ref[...], wo_ref[...])


def _gmlp(x, g, w_in, vn_g, ws, bs, wo, tm=512, tn=PROJ_TN):
    n, d = x.shape
    cwid = wo.shape[0]
    resident = lambda a: pl.BlockSpec(a.shape, lambda i: (0,) * a.ndim, pipeline_mode=pl.Buffered(1))
    vec = lambda w: pl.BlockSpec((1, w), lambda i: (0, 0))
    return pl.pallas_call(
        functools.partial(_gmlp_kernel, tm=tm, tn=tn),
        grid=(n // tm,),
        in_specs=[pl.BlockSpec((tm, d), lambda i: (i, 0)), vec(d), resident(w_in), vec(cwid),
                  resident(ws), resident(bs), resident(wo)],
        out_specs=pl.BlockSpec((tm, d), lambda i: (i, 0)),
        out_shape=jax.ShapeDtypeStruct((n, d), F32),
        scratch_shapes=[pltpu.VMEM((tm, cwid), F32), pltpu.VMEM((tm, cwid), F32), pltpu.VMEM((tm, cwid), BF16)],
        compiler_params=_cparams(("parallel",)),
        name="gmlp",
    )(x, g.reshape(1, d), w_in, vn_g.reshape(1, cwid), ws, bs, wo)


def _pad_rows(w, lo, total):
    return jnp.pad(w, ((lo, total - lo - w.shape[0]), (0, 0)))


def _even_layer(x, batch, seq_len, g, w_in, b_qkv, mu, w0, w2, a0, a2, g2, k_k, k_a, r_k,
                gn_g, gn_b, sinks, w_out, v_first, vres):
    aw = A_WIDTH
    a_cols = 3 * aw + DECAY_RANK + ICLR_RANK + GATE_RANK
    n_low = a_cols - 3 * aw
    b_cols = w_in.shape[1] - a_cols
    pad_low = LOW_PAD - n_low
    tail = (-(3 * aw + b_cols + LOW_PAD)) % PROJ_TN
    w_cat = jnp.concatenate([w_in[:, :3 * aw], w_in[:, a_cols:], w_in[:, 3 * aw:a_cols],
                             jnp.zeros((w_in.shape[0], pad_low + tail), F32)], axis=1).astype(BF16)
    b_cat = jnp.concatenate([jnp.zeros((3 * aw,), F32), b_qkv, jnp.zeros((LOW_PAD + tail,), F32)])
    z = _proj(x, g, w_cat, b_cat)

    mu_l = jnp.pad(mu[3 * aw:], (0, pad_low))
    row = lambda t: t.reshape(1, -1)
    w2p = _pad_rows(w2, 0, LOW_PAD).astype(BF16)
    a2p = _pad_rows(a2, DECAY_RANK, LOW_PAD).astype(BF16)
    g2p = _pad_rows(g2, DECAY_RANK + ICLR_RANK, LOW_PAD).astype(BF16)
    if vres is not None:
        v0, v1, v2 = vres
        rank = v1.shape[1]
        vres_args = (v_first, row(v0), jnp.pad(v1, ((0, 0), (0, VRES_PAD - rank))).astype(BF16),
                     _pad_rows(v2, 0, VRES_PAD).astype(BF16))
    else:
        vres_args = None
    outs = _rwkv(z, batch, seq_len, row(mu[:aw]), row(mu[aw:2 * aw]), row(mu[2 * aw:3 * aw]), row(mu_l),
                 row(w0), w2p, row(a0), a2p, g2p, row(k_k), row(k_a), row(r_k.reshape(-1)),
                 row(gn_g), row(gn_b), vres_args, emit_v=vres is None)
    oa, v = outs if vres is None else (outs[0], None)
    wo = w_out.astype(BF16)
    x = _attn_out(x, z, oa, sinks, wo[:aw], wo[aw:], batch, seq_len)
    return x, v


def _odd_layer(x, g, w_in, vn_g, w_s, b_s, w_out):
    ws = jnp.tril(w_s).astype(BF16)
    return _gmlp(x, g, w_in.astype(BF16), vn_g, ws, b_s[:, :, None], w_out.astype(BF16))


def kernel(x, norm_g, ffn_wg, ffn_wu, ffn_wd, e_w_in, e_b_qkv, e_mu, e_w0, e_w2, e_a0, e_a2, e_g2, e_k_k, e_k_a, e_r_k, e_gn_g, e_gn_b, e_sinks, e_w_out, vres_v0, vres_v1, vres_v2, o_w_in, o_vn_g, o_w_s, o_b_s, o_w_out, final_g):
    batch, seq_len, d = x.shape
    depth = norm_g.shape[0]
    x = x.reshape(batch * seq_len, d)
    n_ffn = 2 * depth
    ffn_norm = lambda k: norm_g[k // 2, 2 * (k % 2)]
    weights = tuple(w[0, 0].astype(BF16) for w in (ffn_wg, ffn_wu, ffn_wd))

    def ffn(x, k, weights):
        nxt = (ffn_wg, ffn_wu, ffn_wd, (k + 1) // 2, (k + 1) % 2) if k + 1 < n_ffn else None
        x, cast = _ffn(x, ffn_norm(k), *weights, g_out=final_g if k + 1 == n_ffn else None, cast_next=nxt)
        return x, cast

    v_first = None
    for layer in range(depth):
        x, weights = ffn(x, 2 * layer, weights)
        if layer % 2 == 0:
            i = layer // 2
            vres = None if i == 0 else (vres_v0[i - 1], vres_v1[i - 1], vres_v2[i - 1])
            x, v_a = _even_layer(x, batch, seq_len, norm_g[layer, 1], e_w_in[i], e_b_qkv[i], e_mu[i],
                                 e_w0[i], e_w2[i], e_a0[i], e_a2[i], e_g2[i], e_k_k[i], e_k_a[i],
                                 e_r_k[i], e_gn_g[i], e_gn_b[i], e_sinks[i], e_w_out[i], v_first, vres)
            if i == 0:
                v_first = v_a
        else:
            j = layer // 2
            x = _odd_layer(x, norm_g[layer, 1], o_w_in[j], o_vn_g[j], o_w_s[j], o_b_s[j], o_w_out[j])
        x, weights = ffn(x, 2 * layer + 1, weights)
    return x.reshape(batch, seq_len, d)
```

```python
import functools
import math

import jax
import jax.numpy as jnp
from jax import lax
from jax.experimental import pallas as pl
from jax.experimental.pallas import tpu as pltpu

F32 = jnp.float32
BF16 = jnp.bfloat16

NORM_EPS = 1e-6
GN_EPS = 64e-5
HEAD_DIM = 64
A_HEADS = 16
A_WIDTH = A_HEADS * HEAD_DIM
B_HEADS = 16
B_KV_HEADS = 4
B_GROUP = B_HEADS // B_KV_HEADS
WINDOW = 128
DECAY_RANK = 64
ICLR_RANK = 64
GATE_RANK = 160
LOW_PAD = 384
VRES_PAD = 128
C_GROUPS = 16
C_CHUNK = 128
SCAN_CHUNK = 64
BF16_TILE_ROWS = 16
LAST_ROWS = 8
GMLP_TN = 1024

VMEM_LIMIT = 60 * 1024 * 1024


def _cparams(sem):
    return pltpu.CompilerParams(dimension_semantics=sem, vmem_limit_bytes=VMEM_LIMIT)


def _dot(a, b):
    return jnp.dot(a, b, preferred_element_type=F32)


def _dot_nt(a, b):
    return lax.dot_general(a, b, (((1,), (1,)), ((), ())), preferred_element_type=F32)


def _dot_tn(a, b):
    return lax.dot_general(a, b, (((0,), (0,)), ((), ())), preferred_element_type=F32)


def _split_dot(x, ones_bf16, left=False):
    x1 = x.astype(BF16)
    r1 = x - x1.astype(F32)
    x2 = r1.astype(BF16)
    x3 = (r1 - x2.astype(F32)).astype(BF16)
    if left:
        return _dot(ones_bf16, x1) + _dot(ones_bf16, x2) + _dot(ones_bf16, x3)
    return _dot(x1, ones_bf16) + _dot(x2, ones_bf16) + _dot(x3, ones_bf16)


def _rms(x, g):
    return x * lax.rsqrt(jnp.mean(x * x, axis=-1, keepdims=True) + NORM_EPS) * g


def _ffn_kernel(*refs, final_norm, cast_next):
    x_ref, g_ref = refs[:2]
    refs = refs[2:]
    if final_norm:
        gout_ref = refs[0]
        refs = refs[1:]
    wg_ref, wu_ref, wd_ref = refs[:3]
    refs = refs[3:]
    if cast_next:
        next_f32 = refs[:3]
        refs = refs[3:]
    o_ref = refs[0]
    refs = refs[1:]
    if cast_next:
        next_bf16 = refs[:3]
        refs = refs[3:]
    (h_ref,) = refs
    f = pl.program_id(1)

    def half_swiglu(h):
        a = _dot(h, wg_ref[...])
        b = _dot(h, wu_ref[...])
        p = (0.5 * a * jax.nn.sigmoid(a) * b).astype(BF16)
        return _dot(p, wd_ref[...])

    def cast_slice():
        if cast_next:
            for src, dst in zip(next_f32, next_bf16):
                dst[...] = src[...].astype(BF16)

    @pl.when(f == 0)
    def _():
        cast_slice()
        half = x_ref.shape[0] // 2
        for s in range(2):
            rows = pl.ds(s * half, half)
            x = x_ref[rows, :]
            h = _rms(x, g_ref[...]).astype(BF16)
            h_ref[rows, :] = h
            o_ref[rows, :] = x + half_swiglu(h)

    @pl.when(f > 0)
    def _():
        cast_slice()
        o_ref[...] += half_swiglu(h_ref[...])

    if final_norm:
        @pl.when(f == pl.num_programs(1) - 1)
        def _():
            o_ref[...] = _rms(o_ref[...], gout_ref[...])


def _ffn(x, g, wg, wu, wd, g_out=None, cast_next=None, tm=1024, tf=512):
    n, d = x.shape
    fdim = wg.shape[1]
    ni, nf = n // tm, fdim // tf
    vec = pl.BlockSpec((1, d), lambda i, f: (0, 0))
    in_specs = [pl.BlockSpec((tm, d), lambda i, f: (i, 0)), vec]
    args = [x, g.reshape(1, d)]
    if g_out is not None:
        in_specs.append(vec)
        args.append(g_out.reshape(1, d))
    in_specs += [
        pl.BlockSpec((d, tf), lambda i, f: (0, f)),
        pl.BlockSpec((d, tf), lambda i, f: (0, f)),
        pl.BlockSpec((tf, d), lambda i, f: (f, 0)),
    ]
    args += [wg, wu, wd]
    out_specs = [pl.BlockSpec((tm, d), lambda i, f: (i, 0))]
    out_shape = [jax.ShapeDtypeStruct((n, d), F32)]
    if cast_next is not None:
        wg4, wu4, wd4, l, j = cast_next
        up_rows, down_rows = d // ni, fdim // (ni * nf)
        assert up_rows * ni == d and down_rows * ni * nf == fdim
        assert up_rows % BF16_TILE_ROWS == 0 and down_rows % BF16_TILE_ROWS == 0
        in_specs += [
            pl.BlockSpec((None, None, up_rows, tf), lambda i, f: (l, j, i, f)),
            pl.BlockSpec((None, None, up_rows, tf), lambda i, f: (l, j, i, f)),
            pl.BlockSpec((None, None, down_rows, d), lambda i, f: (l, j, i * nf + f, 0)),
        ]
        args += [wg4, wu4, wd4]
        out_specs += [
            pl.BlockSpec((up_rows, tf), lambda i, f: (i, f)),
            pl.BlockSpec((up_rows, tf), lambda i, f: (i, f)),
            pl.BlockSpec((down_rows, d), lambda i, f: (i * nf + f, 0)),
        ]
        out_shape += [jax.ShapeDtypeStruct((d, fdim), BF16), jax.ShapeDtypeStruct((d, fdim), BF16),
                      jax.ShapeDtypeStruct((fdim, d), BF16)]
    outs = pl.pallas_call(
        functools.partial(_ffn_kernel, final_norm=g_out is not None, cast_next=cast_next is not None),
        grid=(ni, nf),
        in_specs=in_specs,
        out_specs=out_specs,
        out_shape=out_shape,
        scratch_shapes=[pltpu.VMEM((tm, d), BF16)],
        compiler_params=_cparams(("parallel", "arbitrary")),
        name="ffn",
    )(*args)
    return outs[0], tuple(outs[1:])


MXU_TILE = 256


def _head_sum(x):
    rh = lax.broadcasted_iota(jnp.int32, (MXU_TILE, MXU_TILE), 0) // HEAD_DIM
    ch = lax.broadcasted_iota(jnp.int32, (MXU_TILE, MXU_TILE), 1) // HEAD_DIM
    ones = jnp.where(rh == ch, 1.0, 0.0).astype(BF16)
    hi = x.astype(BF16)
    lo = (x - hi.astype(F32)).astype(BF16)
    blocks = [slice(s, s + MXU_TILE) for s in range(0, x.shape[1], MXU_TILE)]
    return jnp.concatenate([_dot(hi[:, s], ones) + _dot(lo[:, s], ones) for s in blocks], axis=1)


def _rwkv_kernel(*refs, rows, chunk, vres, emit_v):
    (x_ref, gx_ref, wz_ref,
     mur_ref, muk_ref, muv_ref, mul_ref, w0_ref, w2_ref, a0_ref, a2_ref, g2_ref,
     kk_ref, ka_ref, rk_ref, gng_ref, gnb_ref) = refs[:17]
    refs = refs[17:]
    if vres:
        vf_ref, v0_ref, v1_ref, v2_ref = refs[:4]
        refs = refs[4:]
    oa_out = refs[0]
    refs = refs[1:]
    if emit_v:
        v_out = refs[0]
        refs = refs[1:]
    h_ref, last_sc, r_sc, lw_sc, k_sc, v_sc, kk_sc, b_sc, g_sc, bonus_sc, y_sc = refs

    first = pl.program_id(1) == 0

    @pl.when(first)
    def _():
        h_ref[...] = jnp.zeros_like(h_ref)
        last_sc[...] = jnp.zeros_like(last_sc)

    row = lax.broadcasted_iota(jnp.int32, (rows, 1), 0)
    hx = _rms(x_ref[...], gx_ref[...]).astype(BF16)

    def shifted(cols, mu_ref):
        z = _dot(hx, wz_ref[:, cols])
        prev = jnp.where(row == 0, last_sc[LAST_ROWS - 1:LAST_ROWS, cols], pltpu.roll(z, 1, axis=0))
        last_sc[:, cols] = z[rows - LAST_ROWS:, :]
        return z + (prev - z) * mu_ref[...]

    aw = A_WIDTH
    r = shifted(slice(0, aw), mur_ref)
    k = shifted(slice(aw, 2 * aw), muk_ref)
    v = shifted(slice(2 * aw, 3 * aw), muv_ref)
    low = shifted(slice(3 * aw, 3 * aw + LOW_PAD), mul_ref)

    if vres:
        mix = _dot(_dot(v.astype(BF16), v1_ref[...]).astype(BF16), v2_ref[...])
        v = v + (vf_ref[...] - v) * jax.nn.sigmoid(v0_ref[...] + mix)

    dw = w0_ref[...] + _dot(jnp.tanh(low).astype(BF16), w2_ref[...])
    lw_sc[...] = -math.exp(-0.5) * jax.nn.sigmoid(dw)
    a = jax.nn.sigmoid(a0_ref[...] + _dot(low.astype(BF16), a2_ref[...]))
    g_sc[...] = _dot(jax.nn.sigmoid(low).astype(BF16), g2_ref[...])

    kk = k * kk_ref[...]
    kk = kk * jnp.minimum(lax.rsqrt(_head_sum(kk * kk)), 1e12)
    k = k * (1.0 + (a - 1.0) * ka_ref[...])
    r_sc[...] = r
    k_sc[...] = k
    v_sc[...] = v
    if emit_v:
        v_out[...] = v
    kk_sc[...] = kk
    b_sc[...] = kk * a
    bonus_sc[...] = _head_sum(r * k * rk_ref[...]) * v

    _scan_tile((r_sc, lw_sc, k_sc, v_sc, kk_sc, b_sc), h_ref, y_sc, chunk)

    y = y_sc[...]
    yc = y - _head_sum(y) * (1.0 / HEAD_DIM)
    var = _head_sum(yc * yc) * (1.0 / HEAD_DIM)
    yn = yc * lax.rsqrt(var + GN_EPS) * gng_ref[...] + gnb_ref[...]
    oa_out[...] = ((yn + bonus_sc[...]) * g_sc[...]).astype(oa_out.dtype)


def _rwkv(x, gx, wz, batch, seq_len, mu_r, mu_k, mu_v, mu_l, w0, w2p, a0, a2p, g2p, k_k, k_a, r_k,
          gn_g, gn_b, vres, emit_v, rows=256, chunk=SCAN_CHUNK):
    n, d = x.shape
    aw = A_WIDTH
    nr = seq_len // rows
    cur = lambda bi, j: (bi * nr + j, 0)
    row_vec = lambda w: pl.BlockSpec((1, w), lambda bi, j: (0, 0))
    full = lambda a: pl.BlockSpec(a.shape, lambda bi, j: (0, 0), pipeline_mode=pl.Buffered(1))
    in_specs = [
        pl.BlockSpec((rows, d), cur), row_vec(d), full(wz),
        row_vec(aw), row_vec(aw), row_vec(aw), row_vec(LOW_PAD),
        row_vec(aw), full(w2p), row_vec(aw), full(a2p), full(g2p),
        row_vec(aw), row_vec(aw), row_vec(aw), row_vec(aw), row_vec(aw),
    ]
    args = [x, gx, wz, mu_r, mu_k, mu_v, mu_l, w0, w2p, a0, a2p, g2p, k_k, k_a, r_k, gn_g, gn_b]
    if vres is not None:
        v_first, v0, v1p, v2p = vres
        in_specs += [pl.BlockSpec((rows, aw), cur), row_vec(aw), full(v1p), full(v2p)]
        args += [v_first, v0, v1p, v2p]
    out_specs = [pl.BlockSpec((rows, aw), cur)]
    out_shape = [jax.ShapeDtypeStruct((n, aw), BF16)]
    if emit_v:
        out_specs.append(pl.BlockSpec((rows, aw), cur))
        out_shape.append(jax.ShapeDtypeStruct((n, aw), F32))
    state = pltpu.VMEM((A_HEADS // 2, 2 * HEAD_DIM, 2 * HEAD_DIM), F32)
    last = pltpu.VMEM((LAST_ROWS, wz.shape[1]), F32)
    return pl.pallas_call(
        functools.partial(_rwkv_kernel, rows=rows, chunk=chunk, vres=vres is not None, emit_v=emit_v),
        grid=(batch, nr),
        in_specs=in_specs,
        out_specs=out_specs,
        out_shape=out_shape,
        scratch_shapes=[state, last] + [pltpu.VMEM((rows, aw), F32)] * 9,
        compiler_params=_cparams(("parallel", "arbitrary")),
        name="rwkv",
    )(*args)


def _scan_tile(refs, h_ref, y_ref, chunk, group=16):
    c = chunk
    hd = HEAD_DIM
    assert c == hd, "the [power | inverse] lane layout below needs chunk == head width"
    pw = 2 * hd
    npair = A_HEADS // 2
    nchunk = refs[0].shape[0] // c
    pairs = [slice(p * pw, (p + 1) * pw) for p in range(npair)]
    every = [(ci, p) for ci in range(nchunk) for p in range(npair)]

    ri = lax.broadcasted_iota(jnp.int32, (c, c), 0)
    ci_ = lax.broadcasted_iota(jnp.int32, (c, c), 1)
    tri_ones = jnp.where(ri >= ci_, 1.0, 0.0).astype(BF16)
    row = lax.broadcasted_iota(jnp.int32, (c, pw), 0)
    lane = lax.broadcasted_iota(jnp.int32, (c, pw), 1)
    left = lane < hd
    col = jnp.where(left, lane, lane - hd)
    strict = row > col
    incl = row >= col
    eye_pair = jnp.where(row == col, 1.0, 0.0)
    couples = {s: (row // (2 * s) == col // (2 * s)) & (row // s != col // s) & strict
               for s in [2 ** e for e in range(int(math.log2(c)))]}

    def pair_diag(t):
        return jnp.concatenate([jnp.where(left, t, 0.0), jnp.where(left, 0.0, t)], axis=0).astype(BF16)

    left2 = lax.broadcasted_iota(jnp.int32, (2 * c, pw), 1) < hd
    kr = lax.broadcasted_iota(jnp.int32, (pw, pw), 0)
    kc = lax.broadcasted_iota(jnp.int32, (pw, pw), 1)
    same_head = (kr < hd) == (kc < hd)
    zeros_cv = jnp.zeros((c, pw), BF16)

    ar, bk, bke, vv, p_all = {}, {}, {}, {}, []
    for ci in range(nchunk):
        rs = pl.ds(ci * c, c)
        r, lw, k, v, kk, b = (ref[rs, :] for ref in refs)
        cum = _split_dot(lw, tri_ones, left=True)
        p_inv = jnp.exp(-cum)
        tail = cum[c - 1:c, :]
        p_tail = jnp.exp(tail - cum)
        p_all.append(jnp.exp(tail))
        a_t = (-kk * jnp.exp(cum - lw)).astype(BF16)
        r_t = (r * jnp.exp(cum)).astype(BF16)
        b_t = (b * p_inv).astype(BF16)
        k_t = (k * p_inv).astype(BF16)
        b_e = (b * p_tail).astype(BF16)
        k_e = (k * p_tail).astype(BF16)
        vb = v.astype(BF16)
        for p, s in enumerate(pairs):
            ar[ci, p] = jnp.concatenate([a_t[:, s], r_t[:, s]], axis=0)
            bk[ci, p] = jnp.concatenate([b_t[:, s], k_t[:, s]], axis=0)
            bke[ci, p] = jnp.concatenate([b_e[:, s], k_e[:, s]], axis=0)
            vv[ci, p] = vb[:, s]

    mb, lv, inv = {}, {}, {}
    for g in range(0, len(every), group):
        grp = every[g:g + group]
        sc = {i: _dot_nt(jnp.concatenate([jnp.where(left2, ar[i], 0), jnp.where(left2, 0, ar[i])], axis=0),
                         bk[i]) for i in grp}
        lt = {i: [jnp.where(strict, sc[i][q * 2 * c:q * 2 * c + c], 0.0) for q in range(2)] for i in grp}
        for i in grp:
            mb[i] = jnp.concatenate([jnp.where(incl, sc[i][q * 2 * c + c:(q + 1) * 2 * c], 0.0).astype(BF16)
                                     for q in range(2)], axis=0)
        lvf = {i: _dot(jnp.concatenate(lt[i], axis=0).astype(BF16),
                       jnp.concatenate([zeros_cv, vv[i]], axis=0)) for i in grp}
        for i in grp:
            lv[i] = jnp.where(left, lvf[i][:c], lvf[i][c:])
        lp = {i: jnp.where(left, lt[i][0], pltpu.roll(lt[i][1], hd, axis=1)) for i in grp}
        t = {i: jnp.where(couples[1], lp[i], eye_pair) for i in grp}
        size = 2
        while size < c:
            g = {i: _dot(t[i].astype(BF16), pair_diag(jnp.where(couples[size], lp[i], 0.0))) for i in grp}
            t = {i: t[i] + _dot(g[i].astype(BF16), pair_diag(t[i])) for i in grp}
            size *= 2
        for i in grp:
            inv[i] = t[i].astype(BF16)

    h = [h_ref[p] for p in range(npair)]
    for ci in range(nchunk):
        here = [(ci, p) for p in range(npair)]
        ah = [_dot(ar[i], h[p].astype(BF16)) for p, i in enumerate(here)]
        rhs = [ah[p][:c] + lv[i] for p, i in enumerate(here)]
        rhs = [pair_diag(t) for t in rhs]
        uv = [jnp.concatenate([_dot(inv[i], rhs[p]).astype(BF16), vv[i]], axis=0)
              for p, i in enumerate(here)]
        yy = [_dot(mb[i], uv[p]) for p, i in enumerate(here)]
        for p in range(npair):
            y_ref[pl.ds(ci * c, c), pairs[p]] = ah[p][c:] + jnp.where(left, yy[p][:c], yy[p][c:])
        hn = [_dot_tn(bke[i], uv[p]) for p, i in enumerate(here)]
        h = [jnp.transpose(jnp.broadcast_to(p_all[ci][:, pairs[p]], (pw, pw))) * h[p]
             + jnp.where(same_head, hn[p], 0.0) for p in range(npair)]
    for p in range(npair):
        h_ref[p] = h[p]


def _attn_block(sinks_ref, q, kprev, kcur, vprev, vcur, no_prev):
    blk = WINDOW
    r = lax.broadcasted_iota(jnp.int32, (blk, blk), 0)
    qi = lax.broadcasted_iota(jnp.int32, (blk, blk), 1)
    from_prev = r > qi
    neg_dist = jnp.where(from_prev, (r - qi - blk).astype(F32) + no_prev, (r - qi).astype(F32))
    scale = HEAD_DIM ** -0.5
    assert math.frexp(scale)[0] == 0.5, "a power-of-two scale commutes with the bf16 rounding of q"
    kband = jnp.concatenate([kprev, kcur], axis=0).astype(BF16)
    vband = jnp.concatenate([vprev, vcur], axis=0).astype(BF16)
    q = (q * scale).astype(BF16)
    heads = range(B_HEADS)
    head_cols = lambda t, i: t[:, i * HEAD_DIM:(i + 1) * HEAD_DIM]
    qk = [_dot_nt(head_cols(kband, h // B_GROUP), head_cols(q, h)) for h in heads]
    qk = [jnp.where(from_prev, t[:blk], t[blk:]) for t in qk]
    s = [qk[h] + 2.0 ** (-8.0 * (h + 1) / B_HEADS) * neg_dist for h in heads]
    m = [jnp.maximum(jnp.max(s[h], axis=0, keepdims=True), sinks_ref[h]) for h in heads]
    p = [jnp.exp(s[h] - m[h]) for h in heads]
    den = [jnp.sum(p[h], axis=0, keepdims=True) + jnp.exp(sinks_ref[h] - m[h]) for h in heads]
    p = [jnp.concatenate([jnp.where(from_prev, t, 0.0), jnp.where(from_prev, 0.0, t)], axis=0).astype(BF16)
         for t in p]
    pv = [_dot_tn(head_cols(vband, h // B_GROUP), p[h]) for h in heads]
    return [pv[h] / den[h] for h in heads]


def _attn_out_kernel(sinks_ref, x_ref, oa_ref, gx_ref, wq_ref, bq_ref, wa_ref, wb_ref,
                     o_ref, obt_ref, qkv_ref, kvlast_ref):
    blk = WINDOW
    qw = B_HEADS * HEAD_DIM
    kvw = B_KV_HEADS * HEAD_DIM
    tm = x_ref.shape[0]
    first = pl.program_id(1) == 0

    @pl.when(first)
    def _():
        kvlast_ref[...] = jnp.zeros_like(kvlast_ref)

    x = x_ref[...]
    qkv_ref[...] = _dot(_rms(x, gx_ref[...]).astype(BF16), wq_ref[...]) + bq_ref[...]
    acc = x + _dot(oa_ref[...], wa_ref[...])
    for j in range(tm // blk):
        rows = pl.ds(j * blk, blk)
        if j == 0:
            kprev, vprev = kvlast_ref[:, :kvw], kvlast_ref[:, kvw:]
            no_prev = jnp.where(first, -1e32, 0.0)
        else:
            before = pl.ds((j - 1) * blk, blk)
            kprev, vprev = qkv_ref[before, qw:qw + kvw], qkv_ref[before, qw + kvw:]
            no_prev = 0.0
        outs = _attn_block(sinks_ref, qkv_ref[rows, :qw], kprev, qkv_ref[rows, qw:qw + kvw],
                           vprev, qkv_ref[rows, qw + kvw:], no_prev)
        for h, o in enumerate(outs):
            obt_ref[h * HEAD_DIM:(h + 1) * HEAD_DIM, rows] = o.astype(obt_ref.dtype)
    kvlast_ref[...] = qkv_ref[tm - blk:, qw:]
    o_ref[...] = acc + _dot_tn(obt_ref[...], wb_ref[...])


def _attn_out(x, gx, wq, bq, oa, sinks, wa, wb, batch, seq_len, tm=512):
    n, d = x.shape
    aw = oa.shape[1]
    nt = seq_len // tm
    qw = B_HEADS * HEAD_DIM
    kvw = B_KV_HEADS * HEAD_DIM
    tile = lambda w: pl.BlockSpec((tm, w), lambda bi, j: (bi * nt + j, 0))
    vec = lambda w: pl.BlockSpec((1, w), lambda bi, j: (0, 0))
    resident = lambda a: pl.BlockSpec(a.shape, lambda bi, j: (0, 0), pipeline_mode=pl.Buffered(1))
    return pl.pallas_call(
        _attn_out_kernel,
        grid=(batch, nt),
        in_specs=[pl.BlockSpec(memory_space=pltpu.SMEM), tile(d), tile(aw), vec(d), resident(wq),
                  vec(qw + 2 * kvw), resident(wa), resident(wb)],
        out_specs=tile(d),
        out_shape=jax.ShapeDtypeStruct((n, d), F32),
        scratch_shapes=[pltpu.VMEM((qw, tm), BF16), pltpu.VMEM((tm, qw + 2 * kvw), F32),
                        pltpu.VMEM((WINDOW, 2 * kvw), F32)],
        compiler_params=_cparams(("parallel", "arbitrary")),
        name="attn_out",
    )(sinks, x, oa, gx, wq, bq, wa, wb)


def _gmlp_kernel(x_ref, g_ref, win_ref, vg_ref, ws_ref, bs_ref, wo_ref, o_ref, u_ref, v_ref, gate_ref,
                 *, tm, tn):
    cwid = wo_ref.shape[0]
    h = _rms(x_ref[...], g_ref[...]).astype(BF16)
    for c0 in range(0, 2 * cwid, tn):
        z = _dot(h, win_ref[:, c0:c0 + tn])
        z = 0.5 * z * (1.0 + lax.erf(z * (2.0 ** -0.5)))
        if c0 < cwid:
            u_ref[:, c0:c0 + tn] = z
        else:
            v_ref[:, c0 - cwid:c0 - cwid + tn] = z
    vn = _rms(v_ref[...], vg_ref[...]).astype(BF16)
    cw = C_CHUNK
    for ch in range(tm // C_CHUNK):
        rows = slice(ch * C_CHUNK, (ch + 1) * C_CHUNK)
        for grp in range(C_GROUPS):
            cols = slice(grp * cw, (grp + 1) * cw)
            vm = _dot(ws_ref[grp], vn[rows, cols]) + bs_ref[grp]
            gate_ref[rows, cols] = (u_ref[rows, cols] * vm).astype(BF16)
    o_ref[...] = x_ref[...] + _dot(gate_ref[...], wo_ref[...])


def _gmlp(x, g, w_in, vn_g, ws, bs, wo, tm=512, tn=GMLP_TN):
    n, d = x.shape
    cwid = wo.shape[0]
    resident = lambda a: pl.BlockSpec(a.shape, lambda i: (0,) * a.ndim, pipeline_mode=pl.Buffered(1))
    vec = lambda w: pl.BlockSpec((1, w), lambda i: (0, 0))
    return pl.pallas_call(
        functools.partial(_gmlp_kernel, tm=tm, tn=tn),
        grid=(n // tm,),
        in_specs=[pl.BlockSpec((tm, d), lambda i: (i, 0)), vec(d), resident(w_in), vec(cwid),
                  resident(ws), resident(bs), resident(wo)],
        out_specs=pl.BlockSpec((tm, d), lambda i: (i, 0)),
        out_shape=jax.ShapeDtypeStruct((n, d), F32),
        scratch_shapes=[pltpu.VMEM((tm, cwid), F32), pltpu.VMEM((tm, cwid), F32), pltpu.VMEM((tm, cwid), BF16)],
        compiler_params=_cparams(("parallel",)),
        name="gmlp",
    )(x, g.reshape(1, d), w_in, vn_g.reshape(1, cwid), ws, bs, wo)


def _pad_rows(w, lo, total):
    return jnp.pad(w, ((lo, total - lo - w.shape[0]), (0, 0)))


def _even_layer(x, batch, seq_len, g, w_in, b_qkv, mu, w0, w2, a0, a2, g2, k_k, k_a, r_k,
                gn_g, gn_b, sinks, w_out, v_first, vres):
    aw = A_WIDTH
    a_cols = 3 * aw + DECAY_RANK + ICLR_RANK + GATE_RANK
    pad_low = LOW_PAD - (a_cols - 3 * aw)
    row = lambda t: t.reshape(1, -1)
    gx = row(g)
    wz = jnp.pad(w_in[:, :a_cols], ((0, 0), (0, pad_low))).astype(BF16)
    wq = w_in[:, a_cols:].astype(BF16)
    mu_l = jnp.pad(mu[3 * aw:], (0, pad_low))
    w2p = _pad_rows(w2, 0, LOW_PAD).astype(BF16)
    a2p = _pad_rows(a2, DECAY_RANK, LOW_PAD).astype(BF16)
    g2p = _pad_rows(g2, DECAY_RANK + ICLR_RANK, LOW_PAD).astype(BF16)
    if vres is not None:
        v0, v1, v2 = vres
        rank = v1.shape[1]
        vres_args = (v_first, row(v0), jnp.pad(v1, ((0, 0), (0, VRES_PAD - rank))).astype(BF16),
                     _pad_rows(v2, 0, VRES_PAD).astype(BF16))
    else:
        vres_args = None
    outs = _rwkv(x, gx, wz, batch, seq_len, row(mu[:aw]), row(mu[aw:2 * aw]), row(mu[2 * aw:3 * aw]),
                 row(mu_l), row(w0), w2p, row(a0), a2p, g2p, row(k_k), row(k_a), row(r_k.reshape(-1)),
                 row(gn_g), row(gn_b), vres_args, emit_v=vres is None)
    oa, v = outs if vres is None else (outs[0], None)
    wo = w_out.astype(BF16)
    x = _attn_out(x, gx, wq, row(b_qkv), oa, sinks, wo[:aw], wo[aw:], batch, seq_len)
    return x, v


def _odd_layer(x, g, w_in, vn_g, w_s, b_s, w_out):
    ws = jnp.tril(w_s).astype(BF16)
    return _gmlp(x, g, w_in.astype(BF16), vn_g, ws, b_s[:, :, None], w_out.astype(BF16))


def kernel(x, norm_g, ffn_wg, ffn_wu, ffn_wd, e_w_in, e_b_qkv, e_mu, e_w0, e_w2, e_a0, e_a2, e_g2, e_k_k, e_k_a, e_r_k, e_gn_g, e_gn_b, e_sinks, e_w_out, vres_v0, vres_v1, vres_v2, o_w_in, o_vn_g, o_w_s, o_b_s, o_w_out, final_g):
    batch, seq_len, d = x.shape
    depth = norm_g.shape[0]
    x = x.reshape(batch * seq_len, d)
    n_ffn = 2 * depth
    ffn_norm = lambda k: norm_g[k // 2, 2 * (k % 2)]
    weights = tuple(w[0, 0].astype(BF16) for w in (ffn_wg, ffn_wu, ffn_wd))

    def ffn(x, k, weights):
        nxt = (ffn_wg, ffn_wu, ffn_wd, (k + 1) // 2, (k + 1) % 2) if k + 1 < n_ffn else None
        x, cast = _ffn(x, ffn_norm(k), *weights, g_out=final_g if k + 1 == n_ffn else None, cast_next=nxt)
        return x, cast

    v_first = None
    for layer in range(depth):
        x, weights = ffn(x, 2 * layer, weights)
        if layer % 2 == 0:
            i = layer // 2
            vres = None if i == 0 else (vres_v0[i - 1], vres_v1[i - 1], vres_v2[i - 1])
            x, v_a = _even_layer(x, batch, seq_len, norm_g[layer, 1], e_w_in[i], e_b_qkv[i], e_mu[i],
                                 e_w0[i], e_w2[i], e_a0[i], e_a2[i], e_g2[i], e_k_k[i], e_k_a[i],
                                 e_r_k[i], e_gn_g[i], e_gn_b[i], e_sinks[i], e_w_out[i], v_first, vres)
            if i == 0:
                v_first = v_a
        else:
            j = layer // 2
            x = _odd_layer(x, norm_g[layer, 1], o_w_in[j], o_vn_g[j], o_w_s[j], o_b_s[j], o_w_out[j])
        x, weights = ffn(x, 2 * layer + 1, weights)
    return x.reshape(batch, seq_len, d)
```

```python
import functools
import math

import jax
import jax.numpy as jnp
from jax import lax
from jax.experimental import pallas as pl
from jax.experimental.pallas import tpu as pltpu

F32 = jnp.float32
BF16 = jnp.bfloat16

NORM_EPS = 1e-6
GN_EPS = 64e-5
HEAD_DIM = 64
A_HEADS = 16
A_WIDTH = A_HEADS * HEAD_DIM
B_HEADS = 16
B_KV_HEADS = 4
B_GROUP = B_HEADS // B_KV_HEADS
WINDOW = 128
DECAY_RANK = 64
ICLR_RANK = 64
GATE_RANK = 160
LOW_PAD = 384
VRES_PAD = 128
C_GROUPS = 16
C_CHUNK = 128
SCAN_CHUNK = 64
BF16_TILE_ROWS = 16
LAST_ROWS = 8
GMLP_TN = 1024

VMEM_LIMIT = 60 * 1024 * 1024


def _cparams(sem, **extra):
    return pltpu.CompilerParams(dimension_semantics=sem, vmem_limit_bytes=VMEM_LIMIT, **extra)


def _dot(a, b):
    return jnp.dot(a, b, preferred_element_type=F32)


def _dot_nt(a, b):
    return lax.dot_general(a, b, (((1,), (1,)), ((), ())), preferred_element_type=F32)


def _dot_tn(a, b):
    return lax.dot_general(a, b, (((0,), (0,)), ((), ())), preferred_element_type=F32)


def _split_dot(x, ones_bf16, left=False):
    x1 = x.astype(BF16)
    r1 = x - x1.astype(F32)
    x2 = r1.astype(BF16)
    x3 = (r1 - x2.astype(F32)).astype(BF16)
    if left:
        return _dot(ones_bf16, x1) + _dot(ones_bf16, x2) + _dot(ones_bf16, x3)
    return _dot(x1, ones_bf16) + _dot(x2, ones_bf16) + _dot(x3, ones_bf16)


def _rms(x, g):
    return x * lax.rsqrt(jnp.mean(x * x, axis=-1, keepdims=True) + NORM_EPS) * g


def _ffn_kernel(*refs, final_norm, cast_next):
    x_ref, g_ref = refs[:2]
    refs = refs[2:]
    if final_norm:
        gout_ref = refs[0]
        refs = refs[1:]
    wg_ref, wu_ref, wd_ref = refs[:3]
    refs = refs[3:]
    if cast_next:
        next_f32 = refs[:3]
        refs = refs[3:]
    o_ref = refs[0]
    refs = refs[1:]
    if cast_next:
        next_bf16 = refs[:3]
        refs = refs[3:]
    (h_ref,) = refs
    f = pl.program_id(1)

    def half_swiglu(h):
        a = _dot(h, wg_ref[...])
        b = _dot(h, wu_ref[...])
        p = (0.5 * a * jax.nn.sigmoid(a) * b).astype(BF16)
        return _dot(p, wd_ref[...])

    def cast_slice():
        if cast_next:
            for src, dst in zip(next_f32, next_bf16):
                dst[...] = src[...].astype(BF16)

    @pl.when(f == 0)
    def _():
        cast_slice()
        half = x_ref.shape[0] // 2
        for s in range(2):
            rows = pl.ds(s * half, half)
            x = x_ref[rows, :]
            h = _rms(x, g_ref[...]).astype(BF16)
            h_ref[rows, :] = h
            o_ref[rows, :] = x + half_swiglu(h)

    @pl.when(f > 0)
    def _():
        cast_slice()
        o_ref[...] += half_swiglu(h_ref[...])

    if final_norm:
        @pl.when(f == pl.num_programs(1) - 1)
        def _():
            o_ref[...] = _rms(o_ref[...], gout_ref[...])


def _ffn(x, g, wg, wu, wd, g_out=None, cast_next=None, tm=1024, tf=512):
    n, d = x.shape
    fdim = wg.shape[1]
    ni, nf = n // tm, fdim // tf
    vec = pl.BlockSpec((1, d), lambda i, f: (0, 0))
    in_specs = [pl.BlockSpec((tm, d), lambda i, f: (i, 0)), vec]
    args = [x, g.reshape(1, d)]
    if g_out is not None:
        in_specs.append(vec)
        args.append(g_out.reshape(1, d))
    in_specs += [
        pl.BlockSpec((d, tf), lambda i, f: (0, f)),
        pl.BlockSpec((d, tf), lambda i, f: (0, f)),
        pl.BlockSpec((tf, d), lambda i, f: (f, 0)),
    ]
    args += [wg, wu, wd]
    out_specs = [pl.BlockSpec((tm, d), lambda i, f: (i, 0))]
    out_shape = [jax.ShapeDtypeStruct((n, d), F32)]
    if cast_next is not None:
        wg4, wu4, wd4, l, j = cast_next
        up_rows, down_rows = d // ni, fdim // (ni * nf)
        assert up_rows * ni == d and down_rows * ni * nf == fdim
        assert up_rows % BF16_TILE_ROWS == 0 and down_rows % BF16_TILE_ROWS == 0
        in_specs += [
            pl.BlockSpec((None, None, up_rows, tf), lambda i, f: (l, j, i, f)),
            pl.BlockSpec((None, None, up_rows, tf), lambda i, f: (l, j, i, f)),
            pl.BlockSpec((None, None, down_rows, d), lambda i, f: (l, j, i * nf + f, 0)),
        ]
        args += [wg4, wu4, wd4]
        out_specs += [
            pl.BlockSpec((up_rows, tf), lambda i, f: (i, f)),
            pl.BlockSpec((up_rows, tf), lambda i, f: (i, f)),
            pl.BlockSpec((down_rows, d), lambda i, f: (i * nf + f, 0)),
        ]
        out_shape += [jax.ShapeDtypeStruct((d, fdim), BF16), jax.ShapeDtypeStruct((d, fdim), BF16),
                      jax.ShapeDtypeStruct((fdim, d), BF16)]
    outs = pl.pallas_call(
        functools.partial(_ffn_kernel, final_norm=g_out is not None, cast_next=cast_next is not None),
        grid=(ni, nf),
        in_specs=in_specs,
        out_specs=out_specs,
        out_shape=out_shape,
        scratch_shapes=[pltpu.VMEM((tm, d), BF16)],
        compiler_params=_cparams(("parallel", "arbitrary")),
        name="ffn",
    )(*args)
    return outs[0], tuple(outs[1:])


MXU_TILE = 256


def _head_sum(x):
    rh = lax.broadcasted_iota(jnp.int32, (MXU_TILE, MXU_TILE), 0) // HEAD_DIM
    ch = lax.broadcasted_iota(jnp.int32, (MXU_TILE, MXU_TILE), 1) // HEAD_DIM
    ones = jnp.where(rh == ch, 1.0, 0.0).astype(BF16)
    hi = x.astype(BF16)
    lo = (x - hi.astype(F32)).astype(BF16)
    blocks = [slice(s, s + MXU_TILE) for s in range(0, x.shape[1], MXU_TILE)]
    return jnp.concatenate([_dot(hi[:, s], ones) + _dot(lo[:, s], ones) for s in blocks], axis=1)


def _rwkv_kernel(*refs, rows, chunk, vres, emit_v):
    (x_ref, gx_ref, wz_ref,
     mur_ref, muk_ref, muv_ref, mul_ref, w0_ref, w2_ref, a0_ref, a2_ref, g2_ref,
     kk_ref, ka_ref, rk_ref, gng_ref, gnb_ref) = refs[:17]
    refs = refs[17:]
    if vres:
        vf_ref, v0_ref, v1_ref, v2_ref = refs[:4]
        refs = refs[4:]
    oa_out = refs[0]
    refs = refs[1:]
    if emit_v:
        v_out = refs[0]
        refs = refs[1:]
    h_ref, last_sc, r_sc, lw_sc, k_sc, v_sc, kk_sc, b_sc, g_sc, bonus_sc, y_sc = refs

    first = pl.program_id(1) == 0

    @pl.when(first)
    def _():
        h_ref[...] = jnp.zeros_like(h_ref)
        last_sc[...] = jnp.zeros_like(last_sc)

    row = lax.broadcasted_iota(jnp.int32, (rows, 1), 0)
    hx = _rms(x_ref[...], gx_ref[...]).astype(BF16)
    aw = A_WIDTH

    def shifted(cols, mu):
        z = _dot(hx, wz_ref[:, cols])
        prev = jnp.where(row == 0, last_sc[LAST_ROWS - 1:LAST_ROWS, cols], pltpu.roll(z, 1, axis=0))
        last_sc[:, cols] = z[rows - LAST_ROWS:, :]
        return z + (prev - z) * mu

    low = shifted(slice(3 * aw, 3 * aw + LOW_PAD), mul_ref[...])
    low_tanh = jnp.tanh(low).astype(BF16)
    low_sig = jax.nn.sigmoid(low).astype(BF16)
    low_b = low.astype(BF16)
    blocks = [slice(s, s + MXU_TILE) for s in range(0, aw, MXU_TILE)]
    if vres:
        vs = [shifted(slice(2 * aw + cs.start, 2 * aw + cs.stop), muv_ref[:, cs]) for cs in blocks]
        mixed = sum(_dot(vb.astype(BF16), v1_ref[cs, :]) for vb, cs in zip(vs, blocks)).astype(BF16)
    for i, cs in enumerate(blocks):
        r = shifted(cs, mur_ref[:, cs])
        k = shifted(slice(aw + cs.start, aw + cs.stop), muk_ref[:, cs])
        if vres:
            v = vs[i]
            v = v + (vf_ref[:, cs] - v) * jax.nn.sigmoid(v0_ref[:, cs] + _dot(mixed, v2_ref[:, cs]))
        else:
            v = shifted(slice(2 * aw + cs.start, 2 * aw + cs.stop), muv_ref[:, cs])
        dw = w0_ref[:, cs] + _dot(low_tanh, w2_ref[:, cs])
        lw_sc[:, cs] = -math.exp(-0.5) * jax.nn.sigmoid(dw)
        a = jax.nn.sigmoid(a0_ref[:, cs] + _dot(low_b, a2_ref[:, cs]))
        g_sc[:, cs] = _dot(low_sig, g2_ref[:, cs])
        kk = k * kk_ref[:, cs]
        kk = kk * jnp.minimum(lax.rsqrt(_head_sum(kk * kk)), 1e12)
        k = k * (1.0 + (a - 1.0) * ka_ref[:, cs])
        r_sc[:, cs] = r
        k_sc[:, cs] = k
        v_sc[:, cs] = v
        if emit_v:
            v_out[:, cs] = v
        kk_sc[:, cs] = kk
        b_sc[:, cs] = kk * a
        bonus_sc[:, cs] = _head_sum(r * k * rk_ref[:, cs]) * v

    _scan_tile((r_sc, lw_sc, k_sc, v_sc, kk_sc, b_sc), h_ref, y_sc, chunk)

    y = y_sc[...]
    yc = y - _head_sum(y) * (1.0 / HEAD_DIM)
    var = _head_sum(yc * yc) * (1.0 / HEAD_DIM)
    yn = yc * lax.rsqrt(var + GN_EPS) * gng_ref[...] + gnb_ref[...]
    oa_out[...] = ((yn + bonus_sc[...]) * g_sc[...]).astype(oa_out.dtype)


def _rwkv(x, gx, wz, batch, seq_len, mu_r, mu_k, mu_v, mu_l, w0, w2p, a0, a2p, g2p, k_k, k_a, r_k,
          gn_g, gn_b, vres, emit_v, rows=256, chunk=SCAN_CHUNK):
    n, d = x.shape
    aw = A_WIDTH
    nr = seq_len // rows
    cur = lambda bi, j: (bi * nr + j, 0)
    row_vec = lambda w: pl.BlockSpec((1, w), lambda bi, j: (0, 0))
    full = lambda a: pl.BlockSpec(a.shape, lambda bi, j: (0, 0), pipeline_mode=pl.Buffered(1))
    in_specs = [
        pl.BlockSpec((rows, d), cur), row_vec(d), full(wz),
        row_vec(aw), row_vec(aw), row_vec(aw), row_vec(LOW_PAD),
        row_vec(aw), full(w2p), row_vec(aw), full(a2p), full(g2p),
        row_vec(aw), row_vec(aw), row_vec(aw), row_vec(aw), row_vec(aw),
    ]
    args = [x, gx, wz, mu_r, mu_k, mu_v, mu_l, w0, w2p, a0, a2p, g2p, k_k, k_a, r_k, gn_g, gn_b]
    if vres is not None:
        v_first, v0, v1p, v2p = vres
        in_specs += [pl.BlockSpec((rows, aw), cur), row_vec(aw), full(v1p), full(v2p)]
        args += [v_first, v0, v1p, v2p]
    out_specs = [pl.BlockSpec((rows, aw), cur)]
    out_shape = [jax.ShapeDtypeStruct((n, aw), BF16)]
    if emit_v:
        out_specs.append(pl.BlockSpec((rows, aw), cur))
        out_shape.append(jax.ShapeDtypeStruct((n, aw), F32))
    state = pltpu.VMEM((A_HEADS // 2, 2 * HEAD_DIM, 2 * HEAD_DIM), F32)
    last = pltpu.VMEM((LAST_ROWS, wz.shape[1]), F32)
    return pl.pallas_call(
        functools.partial(_rwkv_kernel, rows=rows, chunk=chunk, vres=vres is not None, emit_v=emit_v),
        grid=(batch, nr),
        in_specs=in_specs,
        out_specs=out_specs,
        out_shape=out_shape,
        scratch_shapes=[state, last] + [pltpu.VMEM((rows, aw), F32)] * 9,
        compiler_params=_cparams(("parallel", "arbitrary")),
        name="rwkv",
    )(*args)


def _scan_tile(refs, h_ref, y_ref, chunk, group=16):
    c = chunk
    hd = HEAD_DIM
    assert c == hd, "the [power | inverse] lane layout below needs chunk == head width"
    pw = 2 * hd
    npair = A_HEADS // 2
    nchunk = refs[0].shape[0] // c
    pairs = [slice(p * pw, (p + 1) * pw) for p in range(npair)]
    every = [(ci, p) for ci in range(nchunk) for p in range(npair)]

    ri = lax.broadcasted_iota(jnp.int32, (c, c), 0)
    ci_ = lax.broadcasted_iota(jnp.int32, (c, c), 1)
    tri_ones = jnp.where(ri >= ci_, 1.0, 0.0).astype(BF16)
    row = lax.broadcasted_iota(jnp.int32, (c, pw), 0)
    lane = lax.broadcasted_iota(jnp.int32, (c, pw), 1)
    left = lane < hd
    col = jnp.where(left, lane, lane - hd)
    strict = row > col
    incl = row >= col
    eye_pair = jnp.where(row == col, 1.0, 0.0)
    couples = {s: (row // (2 * s) == col // (2 * s)) & (row // s != col // s) & strict
               for s in [2 ** e for e in range(int(math.log2(c)))]}

    def pair_diag(t):
        return jnp.concatenate([jnp.where(left, t, 0.0), jnp.where(left, 0.0, t)], axis=0).astype(BF16)

    left2 = lax.broadcasted_iota(jnp.int32, (2 * c, pw), 1) < hd
    kr = lax.broadcasted_iota(jnp.int32, (pw, pw), 0)
    kc = lax.broadcasted_iota(jnp.int32, (pw, pw), 1)
    same_head = (kr < hd) == (kc < hd)
    zeros_cv = jnp.zeros((c, pw), BF16)

    ar, bk, bke, vv, p_all = {}, {}, {}, {}, []
    for ci in range(nchunk):
        rs = pl.ds(ci * c, c)
        r, lw, k, v, kk, b = (ref[rs, :] for ref in refs)
        cum = _split_dot(lw, tri_ones, left=True)
        p_inv = jnp.exp(-cum)
        tail = cum[c - 1:c, :]
        p_tail = jnp.exp(tail - cum)
        p_all.append(jnp.exp(tail))
        a_t = (-kk * jnp.exp(cum - lw)).astype(BF16)
        r_t = (r * jnp.exp(cum)).astype(BF16)
        b_t = (b * p_inv).astype(BF16)
        k_t = (k * p_inv).astype(BF16)
        b_e = (b * p_tail).astype(BF16)
        k_e = (k * p_tail).astype(BF16)
        vb = v.astype(BF16)
        for p, s in enumerate(pairs):
            ar[ci, p] = jnp.concatenate([a_t[:, s], r_t[:, s]], axis=0)
            bk[ci, p] = jnp.concatenate([b_t[:, s], k_t[:, s]], axis=0)
            bke[ci, p] = jnp.concatenate([b_e[:, s], k_e[:, s]], axis=0)
            vv[ci, p] = vb[:, s]

    mb, lv, inv = {}, {}, {}
    for g in range(0, len(every), group):
        grp = every[g:g + group]
        sc = {i: _dot_nt(jnp.concatenate([jnp.where(left2, ar[i], 0), jnp.where(left2, 0, ar[i])], axis=0),
                         bk[i]) for i in grp}
        lt = {i: [jnp.where(strict, sc[i][q * 2 * c:q * 2 * c + c], 0.0) for q in range(2)] for i in grp}
        for i in grp:
            mb[i] = jnp.concatenate([jnp.where(incl, sc[i][q * 2 * c + c:(q + 1) * 2 * c], 0.0).astype(BF16)
                                     for q in range(2)], axis=0)
        lvf = {i: _dot(jnp.concatenate(lt[i], axis=0).astype(BF16),
                       jnp.concatenate([zeros_cv, vv[i]], axis=0)) for i in grp}
        for i in grp:
            lv[i] = jnp.where(left, lvf[i][:c], lvf[i][c:])
        lp = {i: jnp.where(left, lt[i][0], pltpu.roll(lt[i][1], hd, axis=1)) for i in grp}
        t = {i: jnp.where(couples[1], lp[i], eye_pair) for i in grp}
        size = 2
        while size < c:
            g = {i: _dot(t[i].astype(BF16), pair_diag(jnp.where(couples[size], lp[i], 0.0))) for i in grp}
            t = {i: t[i] + _dot(g[i].astype(BF16), pair_diag(t[i])) for i in grp}
            size *= 2
        for i in grp:
            inv[i] = t[i].astype(BF16)

    h = [h_ref[p] for p in range(npair)]
    for ci in range(nchunk):
        here = [(ci, p) for p in range(npair)]
        ah = [_dot(ar[i], h[p].astype(BF16)) for p, i in enumerate(here)]
        rhs = [ah[p][:c] + lv[i] for p, i in enumerate(here)]
        rhs = [pair_diag(t) for t in rhs]
        uv = [jnp.concatenate([_dot(inv[i], rhs[p]).astype(BF16), vv[i]], axis=0)
              for p, i in enumerate(here)]
        yy = [_dot(mb[i], uv[p]) for p, i in enumerate(here)]
        for p in range(npair):
            y_ref[pl.ds(ci * c, c), pairs[p]] = ah[p][c:] + jnp.where(left, yy[p][:c], yy[p][c:])
        hn = [_dot_tn(bke[i], uv[p]) for p, i in enumerate(here)]
        h = [jnp.transpose(jnp.broadcast_to(p_all[ci][:, pairs[p]], (pw, pw))) * h[p]
             + jnp.where(same_head, hn[p], 0.0) for p in range(npair)]
    for p in range(npair):
        h_ref[p] = h[p]


def _attn_block(sinks_ref, q, kprev, kcur, vprev, vcur, no_prev):
    blk = WINDOW
    r = lax.broadcasted_iota(jnp.int32, (blk, blk), 0)
    qi = lax.broadcasted_iota(jnp.int32, (blk, blk), 1)
    from_prev = r > qi
    neg_dist = jnp.where(from_prev, (r - qi - blk).astype(F32) + no_prev, (r - qi).astype(F32))
    scale = HEAD_DIM ** -0.5
    assert math.frexp(scale)[0] == 0.5, "a power-of-two scale commutes with the bf16 rounding of q"
    kband = jnp.concatenate([kprev, kcur], axis=0).astype(BF16)
    vband = jnp.concatenate([vprev, vcur], axis=0).astype(BF16)
    q = (q * scale).astype(BF16)
    heads = range(B_HEADS)
    head_cols = lambda t, i: t[:, i * HEAD_DIM:(i + 1) * HEAD_DIM]
    qk = [_dot_nt(head_cols(kband, h // B_GROUP), head_cols(q, h)) for h in heads]
    qk = [jnp.where(from_prev, t[:blk], t[blk:]) for t in qk]
    s = [qk[h] + 2.0 ** (-8.0 * (h + 1) / B_HEADS) * neg_dist for h in heads]
    m = [jnp.maximum(jnp.max(s[h], axis=0, keepdims=True), sinks_ref[h]) for h in heads]
    p = [jnp.exp(s[h] - m[h]) for h in heads]
    den = [jnp.sum(p[h], axis=0, keepdims=True) + jnp.exp(sinks_ref[h] - m[h]) for h in heads]
    p = [jnp.concatenate([jnp.where(from_prev, t, 0.0), jnp.where(from_prev, 0.0, t)], axis=0).astype(BF16)
         for t in p]
    pv = [_dot_tn(head_cols(vband, h // B_GROUP), p[h]) for h in heads]
    return [pv[h] / den[h] for h in heads]


def _attn_out_kernel(sinks_ref, x_ref, oa_ref, gx_ref, wq_ref, bq_ref, wa_ref, wb_ref,
                     o_ref, obt_ref, qkv_ref, kvlast_ref):
    blk = WINDOW
    qw = B_HEADS * HEAD_DIM
    kvw = B_KV_HEADS * HEAD_DIM
    tm = x_ref.shape[0]
    first = pl.program_id(1) == 0

    @pl.when(first)
    def _():
        kvlast_ref[...] = jnp.zeros_like(kvlast_ref)

    x = x_ref[...]
    qkv_ref[...] = _dot(_rms(x, gx_ref[...]).astype(BF16), wq_ref[...]) + bq_ref[...]
    acc = x + _dot(oa_ref[...], wa_ref[...])
    for j in range(tm // blk):
        rows = pl.ds(j * blk, blk)
        if j == 0:
            kprev, vprev = kvlast_ref[:, :kvw], kvlast_ref[:, kvw:]
            no_prev = jnp.where(first, -1e32, 0.0)
        else:
            before = pl.ds((j - 1) * blk, blk)
            kprev, vprev = qkv_ref[before, qw:qw + kvw], qkv_ref[before, qw + kvw:]
            no_prev = 0.0
        outs = _attn_block(sinks_ref, qkv_ref[rows, :qw], kprev, qkv_ref[rows, qw:qw + kvw],
                           vprev, qkv_ref[rows, qw + kvw:], no_prev)
        for h, o in enumerate(outs):
            obt_ref[h * HEAD_DIM:(h + 1) * HEAD_DIM, rows] = o.astype(obt_ref.dtype)
    kvlast_ref[...] = qkv_ref[tm - blk:, qw:]
    o_ref[...] = acc + _dot_tn(obt_ref[...], wb_ref[...])


def _attn_out(x, gx, wq, bq, oa, sinks, wa, wb, batch, seq_len, tm=512):
    n, d = x.shape
    aw = oa.shape[1]
    nt = seq_len // tm
    qw = B_HEADS * HEAD_DIM
    kvw = B_KV_HEADS * HEAD_DIM
    tile = lambda w: pl.BlockSpec((tm, w), lambda bi, j: (bi * nt + j, 0))
    vec = lambda w: pl.BlockSpec((1, w), lambda bi, j: (0, 0))
    resident = lambda a: pl.BlockSpec(a.shape, lambda bi, j: (0, 0), pipeline_mode=pl.Buffered(1))
    return pl.pallas_call(
        _attn_out_kernel,
        grid=(batch, nt),
        in_specs=[pl.BlockSpec(memory_space=pltpu.SMEM), tile(d), tile(aw), vec(d), resident(wq),
                  vec(qw + 2 * kvw), resident(wa), resident(wb)],
        out_specs=tile(d),
        out_shape=jax.ShapeDtypeStruct((n, d), F32),
        scratch_shapes=[pltpu.VMEM((qw, tm), BF16), pltpu.VMEM((tm, qw + 2 * kvw), F32),
                        pltpu.VMEM((WINDOW, 2 * kvw), F32)],
        compiler_params=_cparams(("parallel", "arbitrary")),
        name="attn_out",
    )(sinks, x, oa, gx, wq, bq, wa, wb)


def _gmlp_kernel(x_ref, g_ref, win_ref, vg_ref, ws_ref, bs_ref, wo_ref, o_ref, u_ref, v_ref, gate_ref,
                 *, tm, tn):
    cwid = wo_ref.shape[0]
    h = _rms(x_ref[...], g_ref[...]).astype(BF16)
    for c0 in range(0, 2 * cwid, tn):
        z = _dot(h, win_ref[:, c0:c0 + tn])
        z = 0.5 * z * (1.0 + lax.erf(z * (2.0 ** -0.5)))
        if c0 < cwid:
            u_ref[:, c0:c0 + tn] = z
        else:
            v_ref[:, c0 - cwid:c0 - cwid + tn] = z
    vn = _rms(v_ref[...], vg_ref[...]).astype(BF16)
    cw = C_CHUNK
    for ch in range(tm // C_CHUNK):
        rows = slice(ch * C_CHUNK, (ch + 1) * C_CHUNK)
        for grp in range(C_GROUPS):
            cols = slice(grp * cw, (grp + 1) * cw)
            vm = _dot(ws_ref[grp], vn[rows, cols]) + bs_ref[grp]
            gate_ref[rows, cols] = (u_ref[rows, cols] * vm).astype(BF16)
    o_ref[...] = x_ref[...] + _dot(gate_ref[...], wo_ref[...])


def _gmlp(x, g, w_in, vn_g, ws, bs, wo, tm=512, tn=GMLP_TN):
    n, d = x.shape
    cwid = wo.shape[0]
    resident = lambda a: pl.BlockSpec(a.shape, lambda i: (0,) * a.ndim, pipeline_mode=pl.Buffered(1))
    vec = lambda w: pl.BlockSpec((1, w), lambda i: (0, 0))
    return pl.pallas_call(
        functools.partial(_gmlp_kernel, tm=tm, tn=tn),
        grid=(n // tm,),
        in_specs=[pl.BlockSpec((tm, d), lambda i: (i, 0)), vec(d), resident(w_in), vec(cwid),
                  resident(ws), resident(bs), resident(wo)],
        out_specs=pl.BlockSpec((tm, d), lambda i: (i, 0)),
        out_shape=jax.ShapeDtypeStruct((n, d), F32),
        scratch_shapes=[pltpu.VMEM((tm, cwid), F32), pltpu.VMEM((tm, cwid), F32), pltpu.VMEM((tm, cwid), BF16)],
        compiler_params=_cparams(("parallel",)),
        name="gmlp",
    )(x, g.reshape(1, d), w_in, vn_g.reshape(1, cwid), ws, bs, wo)


def _pad_rows(w, lo, total):
    return jnp.pad(w, ((lo, total - lo - w.shape[0]), (0, 0)))


def _even_layer(x, batch, seq_len, g, w_in, b_qkv, mu, w0, w2, a0, a2, g2, k_k, k_a, r_k,
                gn_g, gn_b, sinks, w_out, v_first, vres):
    aw = A_WIDTH
    a_cols = 3 * aw + DECAY_RANK + ICLR_RANK + GATE_RANK
    pad_low = LOW_PAD - (a_cols - 3 * aw)
    row = lambda t: t.reshape(1, -1)
    gx = row(g)
    wz = jnp.pad(w_in[:, :a_cols], ((0, 0), (0, pad_low))).astype(BF16)
    wq = w_in[:, a_cols:].astype(BF16)
    mu_l = jnp.pad(mu[3 * aw:], (0, pad_low))
    w2p = _pad_rows(w2, 0, LOW_PAD).astype(BF16)
    a2p = _pad_rows(a2, DECAY_RANK, LOW_PAD).astype(BF16)
    g2p = _pad_rows(g2, DECAY_RANK + ICLR_RANK, LOW_PAD).astype(BF16)
    if vres is not None:
        v0, v1, v2 = vres
        rank = v1.shape[1]
        vres_args = (v_first, row(v0), jnp.pad(v1, ((0, 0), (0, VRES_PAD - rank))).astype(BF16),
                     _pad_rows(v2, 0, VRES_PAD).astype(BF16))
    else:
        vres_args = None
    outs = _rwkv(x, gx, wz, batch, seq_len, row(mu[:aw]), row(mu[aw:2 * aw]), row(mu[2 * aw:3 * aw]),
                 row(mu_l), row(w0), w2p, row(a0), a2p, g2p, row(k_k), row(k_a), row(r_k.reshape(-1)),
                 row(gn_g), row(gn_b), vres_args, emit_v=vres is None)
    oa, v = outs if vres is None else (outs[0], None)
    wo = w_out.astype(BF16)
    x = _attn_out(x, gx, wq, row(b_qkv), oa, sinks, wo[:aw], wo[aw:], batch, seq_len)
    return x, v


def _odd_layer(x, g, w_in, vn_g, w_s, b_s, w_out):
    ws = jnp.tril(w_s).astype(BF16)
    return _gmlp(x, g, w_in.astype(BF16), vn_g, ws, b_s[:, :, None], w_out.astype(BF16))


def kernel(x, norm_g, ffn_wg, ffn_wu, ffn_wd, e_w_in, e_b_qkv, e_mu, e_w0, e_w2, e_a0, e_a2, e_g2, e_k_k, e_k_a, e_r_k, e_gn_g, e_gn_b, e_sinks, e_w_out, vres_v0, vres_v1, vres_v2, o_w_in, o_vn_g, o_w_s, o_b_s, o_w_out, final_g):
    batch, seq_len, d = x.shape
    depth = norm_g.shape[0]
    x = x.reshape(batch * seq_len, d)
    n_ffn = 2 * depth
    ffn_norm = lambda k: norm_g[k // 2, 2 * (k % 2)]
    weights = tuple(w[0, 0].astype(BF16) for w in (ffn_wg, ffn_wu, ffn_wd))

    def ffn(x, k, weights):
        nxt = (ffn_wg, ffn_wu, ffn_wd, (k + 1) // 2, (k + 1) % 2) if k + 1 < n_ffn else None
        x, cast = _ffn(x, ffn_norm(k), *weights, g_out=final_g if k + 1 == n_ffn else None, cast_next=nxt)
        return x, cast

    v_first = None
    for layer in range(depth):
        x, weights = ffn(x, 2 * layer, weights)
        if layer % 2 == 0:
            i = layer // 2
            vres = None if i == 0 else (vres_v0[i - 1], vres_v1[i - 1], vres_v2[i - 1])
            x, v_a = _even_layer(x, batch, seq_len, norm_g[layer, 1], e_w_in[i], e_b_qkv[i], e_mu[i],
                                 e_w0[i], e_w2[i], e_a0[i], e_a2[i], e_g2[i], e_k_k[i], e_k_a[i],
                                 e_r_k[i], e_gn_g[i], e_gn_b[i], e_sinks[i], e_w_out[i], v_first, vres)
            if i == 0:
                v_first = v_a
        else:
            j = layer // 2
            x = _odd_layer(x, norm_g[layer, 1], o_w_in[j], o_vn_g[j], o_w_s[j], o_b_s[j], o_w_out[j])
        x, weights = ffn(x, 2 * layer + 1, weights)
    return x.reshape(batch, seq_len, d)
```

```python
import functools
import math

import jax
import jax.numpy as jnp
from jax import lax
from jax.experimental import pallas as pl
from jax.experimental.pallas import tpu as pltpu

F32 = jnp.float32
BF16 = jnp.bfloat16

NORM_EPS = 1e-6
GN_EPS = 64e-5
HEAD_DIM = 64
A_HEADS = 16
A_WIDTH = A_HEADS * HEAD_DIM
B_HEADS = 16
B_KV_HEADS = 4
B_GROUP = B_HEADS // B_KV_HEADS
WINDOW = 128
DECAY_RANK = 64
ICLR_RANK = 64
GATE_RANK = 160
LOW_PAD = 384
VRES_PAD = 128
C_GROUPS = 16
C_CHUNK = 128
SCAN_CHUNK = 64
BF16_TILE_ROWS = 16
LAST_ROWS = 8
GMLP_TN = 1024

VMEM_LIMIT = 60 * 1024 * 1024


def _cparams(sem, **extra):
    return pltpu.CompilerParams(dimension_semantics=sem, vmem_limit_bytes=VMEM_LIMIT, **extra)


def _dot(a, b):
    return jnp.dot(a, b, preferred_element_type=F32)


def _dot_nt(a, b):
    return lax.dot_general(a, b, (((1,), (1,)), ((), ())), preferred_element_type=F32)


def _dot_tn(a, b):
    return lax.dot_general(a, b, (((0,), (0,)), ((), ())), preferred_element_type=F32)


def _split_dot(x, ones_bf16, left=False):
    x1 = x.astype(BF16)
    r1 = x - x1.astype(F32)
    x2 = r1.astype(BF16)
    x3 = (r1 - x2.astype(F32)).astype(BF16)
    if left:
        return _dot(ones_bf16, x1) + _dot(ones_bf16, x2) + _dot(ones_bf16, x3)
    return _dot(x1, ones_bf16) + _dot(x2, ones_bf16) + _dot(x3, ones_bf16)


def _rms(x, g):
    return x * lax.rsqrt(jnp.mean(x * x, axis=-1, keepdims=True) + NORM_EPS) * g


def _ffn_kernel(*refs, final_norm, cast_next):
    x_ref, g_ref = refs[:2]
    refs = refs[2:]
    if final_norm:
        gout_ref = refs[0]
        refs = refs[1:]
    wg_ref, wu_ref, wd_ref = refs[:3]
    refs = refs[3:]
    if cast_next:
        next_f32 = refs[:3]
        refs = refs[3:]
    o_ref = refs[0]
    refs = refs[1:]
    if cast_next:
        next_bf16 = refs[:3]
        refs = refs[3:]
    (h_ref,) = refs
    f = pl.program_id(1)

    def half_swiglu(h):
        a = _dot(h, wg_ref[...])
        b = _dot(h, wu_ref[...])
        p = (0.5 * a * jax.nn.sigmoid(a) * b).astype(BF16)
        return _dot(p, wd_ref[...])

    def cast_slice():
        if cast_next:
            for src, dst in zip(next_f32, next_bf16):
                dst[...] = src[...].astype(BF16)

    @pl.when(f == 0)
    def _():
        cast_slice()
        half = x_ref.shape[0] // 2
        for s in range(2):
            rows = pl.ds(s * half, half)
            x = x_ref[rows, :]
            h = _rms(x, g_ref[...]).astype(BF16)
            h_ref[rows, :] = h
            o_ref[rows, :] = x + half_swiglu(h)

    @pl.when(f > 0)
    def _():
        cast_slice()
        o_ref[...] += half_swiglu(h_ref[...])

    if final_norm:
        @pl.when(f == pl.num_programs(1) - 1)
        def _():
            o_ref[...] = _rms(o_ref[...], gout_ref[...])


def _ffn(x, g, wg, wu, wd, g_out=None, cast_next=None, tm=1024, tf=512):
    n, d = x.shape
    fdim = wg.shape[1]
    ni, nf = n // tm, fdim // tf
    vec = pl.BlockSpec((1, d), lambda i, f: (0, 0))
    in_specs = [pl.BlockSpec((tm, d), lambda i, f: (i, 0)), vec]
    args = [x, g.reshape(1, d)]
    if g_out is not None:
        in_specs.append(vec)
        args.append(g_out.reshape(1, d))
    in_specs += [
        pl.BlockSpec((d, tf), lambda i, f: (0, f)),
        pl.BlockSpec((d, tf), lambda i, f: (0, f)),
        pl.BlockSpec((tf, d), lambda i, f: (f, 0)),
    ]
    args += [wg, wu, wd]
    out_specs = [pl.BlockSpec((tm, d), lambda i, f: (i, 0))]
    out_shape = [jax.ShapeDtypeStruct((n, d), F32)]
    if cast_next is not None:
        wg4, wu4, wd4, l, j = cast_next
        up_rows, down_rows = d // ni, fdim // (ni * nf)
        assert up_rows * ni == d and down_rows * ni * nf == fdim
        assert up_rows % BF16_TILE_ROWS == 0 and down_rows % BF16_TILE_ROWS == 0
        in_specs += [
            pl.BlockSpec((None, None, up_rows, tf), lambda i, f: (l, j, i, f)),
            pl.BlockSpec((None, None, up_rows, tf), lambda i, f: (l, j, i, f)),
            pl.BlockSpec((None, None, down_rows, d), lambda i, f: (l, j, i * nf + f, 0)),
        ]
        args += [wg4, wu4, wd4]
        out_specs += [
            pl.BlockSpec((up_rows, tf), lambda i, f: (i, f)),
            pl.BlockSpec((up_rows, tf), lambda i, f: (i, f)),
            pl.BlockSpec((down_rows, d), lambda i, f: (i * nf + f, 0)),
        ]
        out_shape += [jax.ShapeDtypeStruct((d, fdim), BF16), jax.ShapeDtypeStruct((d, fdim), BF16),
                      jax.ShapeDtypeStruct((fdim, d), BF16)]
    outs = pl.pallas_call(
        functools.partial(_ffn_kernel, final_norm=g_out is not None, cast_next=cast_next is not None),
        grid=(ni, nf),
        in_specs=in_specs,
        out_specs=out_specs,
        out_shape=out_shape,
        scratch_shapes=[pltpu.VMEM((tm, d), BF16)],
        compiler_params=_cparams(("parallel", "arbitrary")),
        name="ffn",
    )(*args)
    return outs[0], tuple(outs[1:])


MXU_TILE = 256


def _head_sum(x):
    rh = lax.broadcasted_iota(jnp.int32, (MXU_TILE, MXU_TILE), 0) // HEAD_DIM
    ch = lax.broadcasted_iota(jnp.int32, (MXU_TILE, MXU_TILE), 1) // HEAD_DIM
    ones = jnp.where(rh == ch, 1.0, 0.0).astype(BF16)
    hi = x.astype(BF16)
    lo = (x - hi.astype(F32)).astype(BF16)
    blocks = [slice(s, s + MXU_TILE) for s in range(0, x.shape[1], MXU_TILE)]
    return jnp.concatenate([_dot(hi[:, s], ones) + _dot(lo[:, s], ones) for s in blocks], axis=1)


def _rwkv_kernel(*refs, rows, chunk, vres, emit_v):
    (x_ref, gx_ref, wz_ref,
     mur_ref, muk_ref, muv_ref, mul_ref, w0_ref, w2_ref, a0_ref, a2_ref, g2_ref,
     kk_ref, ka_ref, rk_ref, gng_ref, gnb_ref) = refs[:17]
    refs = refs[17:]
    if vres:
        vf_ref, v0_ref, v1_ref, v2_ref = refs[:4]
        refs = refs[4:]
    oa_out = refs[0]
    refs = refs[1:]
    if emit_v:
        v_out = refs[0]
        refs = refs[1:]
    h_ref, last_sc, r_sc, lw_sc, k_sc, v_sc, kk_sc, b_sc, g_sc, bonus_sc, y_sc = refs

    first = pl.program_id(1) == 0

    @pl.when(first)
    def _():
        h_ref[...] = jnp.zeros_like(h_ref)
        last_sc[...] = jnp.zeros_like(last_sc)

    row = lax.broadcasted_iota(jnp.int32, (rows, 1), 0)
    hx = _rms(x_ref[...], gx_ref[...]).astype(BF16)
    aw = A_WIDTH

    def shifted(cols, mu):
        z = _dot(hx, wz_ref[:, cols])
        prev = jnp.where(row == 0, last_sc[LAST_ROWS - 1:LAST_ROWS, cols], pltpu.roll(z, 1, axis=0))
        last_sc[:, cols] = z[rows - LAST_ROWS:, :]
        return z + (prev - z) * mu

    low = shifted(slice(3 * aw, 3 * aw + LOW_PAD), mul_ref[...])
    low_tanh = jnp.tanh(low).astype(BF16)
    low_sig = jax.nn.sigmoid(low).astype(BF16)
    low_b = low.astype(BF16)
    blocks = [slice(s, s + MXU_TILE) for s in range(0, aw, MXU_TILE)]
    if vres:
        vs = [shifted(slice(2 * aw + cs.start, 2 * aw + cs.stop), muv_ref[:, cs]) for cs in blocks]
        mixed = sum(_dot(vb.astype(BF16), v1_ref[cs, :]) for vb, cs in zip(vs, blocks)).astype(BF16)
    for i, cs in enumerate(blocks):
        r = shifted(cs, mur_ref[:, cs])
        k = shifted(slice(aw + cs.start, aw + cs.stop), muk_ref[:, cs])
        if vres:
            v = vs[i]
            v = v + (vf_ref[:, cs] - v) * jax.nn.sigmoid(v0_ref[:, cs] + _dot(mixed, v2_ref[:, cs]))
        else:
            v = shifted(slice(2 * aw + cs.start, 2 * aw + cs.stop), muv_ref[:, cs])
        dw = w0_ref[:, cs] + _dot(low_tanh, w2_ref[:, cs])
        lw_sc[:, cs] = -math.exp(-0.5) * jax.nn.sigmoid(dw)
        a = jax.nn.sigmoid(a0_ref[:, cs] + _dot(low_b, a2_ref[:, cs]))
        g_sc[:, cs] = _dot(low_sig, g2_ref[:, cs])
        kk = k * kk_ref[:, cs]
        kk = kk * jnp.minimum(lax.rsqrt(_head_sum(kk * kk)), 1e12)
        k = k * (1.0 + (a - 1.0) * ka_ref[:, cs])
        r_sc[:, cs] = r
        k_sc[:, cs] = k
        v_sc[:, cs] = v
        if emit_v:
            v_out[:, cs] = v
        kk_sc[:, cs] = kk
        b_sc[:, cs] = kk * a
        bonus_sc[:, cs] = _head_sum(r * k * rk_ref[:, cs]) * v

    _scan_tile((r_sc, lw_sc, k_sc, v_sc, kk_sc, b_sc), h_ref, y_sc, chunk)

    for cs in blocks:
        y = y_sc[:, cs]
        yc = y - _head_sum(y) * (1.0 / HEAD_DIM)
        var = _head_sum(yc * yc) * (1.0 / HEAD_DIM)
        yn = yc * lax.rsqrt(var + GN_EPS) * gng_ref[:, cs] + gnb_ref[:, cs]
        oa_out[:, cs] = ((yn + bonus_sc[:, cs]) * g_sc[:, cs]).astype(oa_out.dtype)


def _rwkv(x, gx, wz, batch, seq_len, mu_r, mu_k, mu_v, mu_l, w0, w2p, a0, a2p, g2p, k_k, k_a, r_k,
          gn_g, gn_b, vres, emit_v, rows=256, chunk=SCAN_CHUNK):
    n, d = x.shape
    aw = A_WIDTH
    nr = seq_len // rows
    cur = lambda bi, j: (bi * nr + j, 0)
    row_vec = lambda w: pl.BlockSpec((1, w), lambda bi, j: (0, 0))
    full = lambda a: pl.BlockSpec(a.shape, lambda bi, j: (0, 0), pipeline_mode=pl.Buffered(1))
    in_specs = [
        pl.BlockSpec((rows, d), cur), row_vec(d), full(wz),
        row_vec(aw), row_vec(aw), row_vec(aw), row_vec(LOW_PAD),
        row_vec(aw), full(w2p), row_vec(aw), full(a2p), full(g2p),
        row_vec(aw), row_vec(aw), row_vec(aw), row_vec(aw), row_vec(aw),
    ]
    args = [x, gx, wz, mu_r, mu_k, mu_v, mu_l, w0, w2p, a0, a2p, g2p, k_k, k_a, r_k, gn_g, gn_b]
    if vres is not None:
        v_first, v0, v1p, v2p = vres
        in_specs += [pl.BlockSpec((rows, aw), cur), row_vec(aw), full(v1p), full(v2p)]
        args += [v_first, v0, v1p, v2p]
    out_specs = [pl.BlockSpec((rows, aw), cur)]
    out_shape = [jax.ShapeDtypeStruct((n, aw), BF16)]
    if emit_v:
        out_specs.append(pl.BlockSpec((rows, aw), cur))
        out_shape.append(jax.ShapeDtypeStruct((n, aw), F32))
    state = pltpu.VMEM((A_HEADS // 2, 2 * HEAD_DIM, 2 * HEAD_DIM), F32)
    last = pltpu.VMEM((LAST_ROWS, wz.shape[1]), F32)
    return pl.pallas_call(
        functools.partial(_rwkv_kernel, rows=rows, chunk=chunk, vres=vres is not None, emit_v=emit_v),
        grid=(batch, nr),
        in_specs=in_specs,
        out_specs=out_specs,
        out_shape=out_shape,
        scratch_shapes=[state, last] + [pltpu.VMEM((rows, aw), F32)] * 9,
        compiler_params=_cparams(("parallel", "arbitrary")),
        name="rwkv",
    )(*args)


def _scan_tile(refs, h_ref, y_ref, chunk, group=16):
    c = chunk
    hd = HEAD_DIM
    assert c == hd, "the [power | inverse] lane layout below needs chunk == head width"
    pw = 2 * hd
    npair = A_HEADS // 2
    nchunk = refs[0].shape[0] // c
    pairs = [slice(p * pw, (p + 1) * pw) for p in range(npair)]
    every = [(ci, p) for ci in range(nchunk) for p in range(npair)]

    ri = lax.broadcasted_iota(jnp.int32, (c, c), 0)
    ci_ = lax.broadcasted_iota(jnp.int32, (c, c), 1)
    tri_ones = jnp.where(ri >= ci_, 1.0, 0.0).astype(BF16)
    row = lax.broadcasted_iota(jnp.int32, (c, pw), 0)
    lane = lax.broadcasted_iota(jnp.int32, (c, pw), 1)
    left = lane < hd
    col = jnp.where(left, lane, lane - hd)
    strict = row > col
    incl = row >= col
    eye_pair = jnp.where(row == col, 1.0, 0.0)
    couples = {s: (row // (2 * s) == col // (2 * s)) & (row // s != col // s) & strict
               for s in [2 ** e for e in range(int(math.log2(c)))]}

    def pair_diag(t):
        return jnp.concatenate([jnp.where(left, t, 0.0), jnp.where(left, 0.0, t)], axis=0).astype(BF16)

    left2 = lax.broadcasted_iota(jnp.int32, (2 * c, pw), 1) < hd
    kr = lax.broadcasted_iota(jnp.int32, (pw, pw), 0)
    kc = lax.broadcasted_iota(jnp.int32, (pw, pw), 1)
    same_head = (kr < hd) == (kc < hd)
    zeros_cv = jnp.zeros((c, pw), BF16)

    ar, bk, bke, vv, p_all = {}, {}, {}, {}, {}
    per = MXU_TILE // pw
    for ci in range(nchunk):
        rs = pl.ds(ci * c, c)
        for blk in range(npair // per):
            cs = slice(blk * MXU_TILE, (blk + 1) * MXU_TILE)
            r, lw, k, v, kk, b = (ref[rs, cs] for ref in refs)
            cum = _split_dot(lw, tri_ones, left=True)
            p_inv = jnp.exp(-cum)
            tail = cum[c - 1:c, :]
            p_tail = jnp.exp(tail - cum)
            p_end = jnp.exp(tail)
            a_t = (-kk * jnp.exp(cum - lw)).astype(BF16)
            r_t = (r * jnp.exp(cum)).astype(BF16)
            b_t = (b * p_inv).astype(BF16)
            k_t = (k * p_inv).astype(BF16)
            b_e = (b * p_tail).astype(BF16)
            k_e = (k * p_tail).astype(BF16)
            vb = v.astype(BF16)
            for q in range(per):
                p, s = blk * per + q, slice(q * pw, (q + 1) * pw)
                ar[ci, p] = jnp.concatenate([a_t[:, s], r_t[:, s]], axis=0)
                bk[ci, p] = jnp.concatenate([b_t[:, s], k_t[:, s]], axis=0)
                bke[ci, p] = jnp.concatenate([b_e[:, s], k_e[:, s]], axis=0)
                vv[ci, p] = vb[:, s]
                p_all[ci, p] = p_end[:, s]

    mb, lv, inv = {}, {}, {}
    for g in range(0, len(every), group):
        grp = every[g:g + group]
        sc = {i: _dot_nt(jnp.concatenate([jnp.where(left2, ar[i], 0), jnp.where(left2, 0, ar[i])], axis=0),
                         bk[i]) for i in grp}
        lt = {i: [jnp.where(strict, sc[i][q * 2 * c:q * 2 * c + c], 0.0) for q in range(2)] for i in grp}
        for i in grp:
            mb[i] = jnp.concatenate([jnp.where(incl, sc[i][q * 2 * c + c:(q + 1) * 2 * c], 0.0).astype(BF16)
                                     for q in range(2)], axis=0)
        lvf = {i: _dot(jnp.concatenate(lt[i], axis=0).astype(BF16),
                       jnp.concatenate([zeros_cv, vv[i]], axis=0)) for i in grp}
        for i in grp:
            lv[i] = jnp.where(left, lvf[i][:c], lvf[i][c:])
        lp = {i: jnp.where(left, lt[i][0], pltpu.roll(lt[i][1], hd, axis=1)) for i in grp}
        t = {i: jnp.where(couples[1], lp[i], eye_pair) for i in grp}
        size = 2
        while size < c:
            g = {i: _dot(t[i].astype(BF16), pair_diag(jnp.where(couples[size], lp[i], 0.0))) for i in grp}
            t = {i: t[i] + _dot(g[i].astype(BF16), pair_diag(t[i])) for i in grp}
            size *= 2
        for i in grp:
            inv[i] = t[i].astype(BF16)

    h = [h_ref[p] for p in range(npair)]
    for ci in range(nchunk):
        here = [(ci, p) for p in range(npair)]
        ah = [_dot(ar[i], h[p].astype(BF16)) for p, i in enumerate(here)]
        rhs = [ah[p][:c] + lv[i] for p, i in enumerate(here)]
        rhs = [pair_diag(t) for t in rhs]
        uv = [jnp.concatenate([_dot(inv[i], rhs[p]).astype(BF16), vv[i]], axis=0)
              for p, i in enumerate(here)]
        yy = [_dot(mb[i], uv[p]) for p, i in enumerate(here)]
        for p in range(npair):
            y_ref[pl.ds(ci * c, c), pairs[p]] = ah[p][c:] + jnp.where(left, yy[p][:c], yy[p][c:])
        hn = [_dot_tn(bke[i], uv[p]) for p, i in enumerate(here)]
        h = [jnp.transpose(jnp.broadcast_to(p_all[ci, p], (pw, pw))) * h[p]
             + jnp.where(same_head, hn[p], 0.0) for p in range(npair)]
    for p in range(npair):
        h_ref[p] = h[p]


def _attn_block(sinks_ref, q, kprev, kcur, vprev, vcur, no_prev):
    blk = WINDOW
    r = lax.broadcasted_iota(jnp.int32, (blk, blk), 0)
    qi = lax.broadcasted_iota(jnp.int32, (blk, blk), 1)
    from_prev = r > qi
    neg_dist = jnp.where(from_prev, (r - qi - blk).astype(F32) + no_prev, (r - qi).astype(F32))
    scale = HEAD_DIM ** -0.5
    assert math.frexp(scale)[0] == 0.5, "a power-of-two scale commutes with the bf16 rounding of q"
    kband = jnp.concatenate([kprev, kcur], axis=0).astype(BF16)
    vband = jnp.concatenate([vprev, vcur], axis=0).astype(BF16)
    q = (q * scale).astype(BF16)
    heads = range(B_HEADS)
    head_cols = lambda t, i: t[:, i * HEAD_DIM:(i + 1) * HEAD_DIM]
    qk = [_dot_nt(head_cols(kband, h // B_GROUP), head_cols(q, h)) for h in heads]
    qk = [jnp.where(from_prev, t[:blk], t[blk:]) for t in qk]
    s = [qk[h] + 2.0 ** (-8.0 * (h + 1) / B_HEADS) * neg_dist for h in heads]
    m = [jnp.maximum(jnp.max(s[h], axis=0, keepdims=True), sinks_ref[h]) for h in heads]
    p = [jnp.exp(s[h] - m[h]) for h in heads]
    den = [jnp.sum(p[h], axis=0, keepdims=True) + jnp.exp(sinks_ref[h] - m[h]) for h in heads]
    p = [jnp.concatenate([jnp.where(from_prev, t, 0.0), jnp.where(from_prev, 0.0, t)], axis=0).astype(BF16)
         for t in p]
    pv = [_dot_tn(head_cols(vband, h // B_GROUP), p[h]) for h in heads]
    return [pv[h] / den[h] for h in heads]


def _attn_out_kernel(sinks_ref, x_ref, oa_ref, gx_ref, wq_ref, bq_ref, wa_ref, wb_ref,
                     o_ref, obt_ref, qkv_ref, kvlast_ref):
    blk = WINDOW
    qw = B_HEADS * HEAD_DIM
    kvw = B_KV_HEADS * HEAD_DIM
    tm = x_ref.shape[0]
    first = pl.program_id(1) == 0

    @pl.when(first)
    def _():
        kvlast_ref[...] = jnp.zeros_like(kvlast_ref)

    x = x_ref[...]
    qkv_ref[...] = _dot(_rms(x, gx_ref[...]).astype(BF16), wq_ref[...]) + bq_ref[...]
    acc = x + _dot(oa_ref[...], wa_ref[...])
    for j in range(tm // blk):
        rows = pl.ds(j * blk, blk)
        if j == 0:
            kprev, vprev = kvlast_ref[:, :kvw], kvlast_ref[:, kvw:]
            no_prev = jnp.where(first, -1e32, 0.0)
        else:
            before = pl.ds((j - 1) * blk, blk)
            kprev, vprev = qkv_ref[before, qw:qw + kvw], qkv_ref[before, qw + kvw:]
            no_prev = 0.0
        outs = _attn_block(sinks_ref, qkv_ref[rows, :qw], kprev, qkv_ref[rows, qw:qw + kvw],
                           vprev, qkv_ref[rows, qw + kvw:], no_prev)
        for h, o in enumerate(outs):
            obt_ref[h * HEAD_DIM:(h + 1) * HEAD_DIM, rows] = o.astype(obt_ref.dtype)
    kvlast_ref[...] = qkv_ref[tm - blk:, qw:]
    o_ref[...] = acc + _dot_tn(obt_ref[...], wb_ref[...])


def _attn_out(x, gx, wq, bq, oa, sinks, wa, wb, batch, seq_len, tm=512):
    n, d = x.shape
    aw = oa.shape[1]
    nt = seq_len // tm
    qw = B_HEADS * HEAD_DIM
    kvw = B_KV_HEADS * HEAD_DIM
    tile = lambda w: pl.BlockSpec((tm, w), lambda bi, j: (bi * nt + j, 0))
    vec = lambda w: pl.BlockSpec((1, w), lambda bi, j: (0, 0))
    resident = lambda a: pl.BlockSpec(a.shape, lambda bi, j: (0, 0), pipeline_mode=pl.Buffered(1))
    return pl.pallas_call(
        _attn_out_kernel,
        grid=(batch, nt),
        in_specs=[pl.BlockSpec(memory_space=pltpu.SMEM), tile(d), tile(aw), vec(d), resident(wq),
                  vec(qw + 2 * kvw), resident(wa), resident(wb)],
        out_specs=tile(d),
        out_shape=jax.ShapeDtypeStruct((n, d), F32),
        scratch_shapes=[pltpu.VMEM((qw, tm), BF16), pltpu.VMEM((tm, qw + 2 * kvw), F32),
                        pltpu.VMEM((WINDOW, 2 * kvw), F32)],
        compiler_params=_cparams(("parallel", "arbitrary")),
        name="attn_out",
    )(sinks, x, oa, gx, wq, bq, wa, wb)


def _gmlp_kernel(x_ref, g_ref, win_ref, vg_ref, ws_ref, bs_ref, wo_ref, o_ref, u_ref, v_ref, gate_ref,
                 *, tm, tn):
    cwid = wo_ref.shape[0]
    h = _rms(x_ref[...], g_ref[...]).astype(BF16)
    for c0 in range(0, 2 * cwid, tn):
        z = _dot(h, win_ref[:, c0:c0 + tn])
        z = 0.5 * z * (1.0 + lax.erf(z * (2.0 ** -0.5)))
        if c0 < cwid:
            u_ref[:, c0:c0 + tn] = z
        else:
            v_ref[:, c0 - cwid:c0 - cwid + tn] = z
    vn = _rms(v_ref[...], vg_ref[...]).astype(BF16)
    cw = C_CHUNK
    for ch in range(tm // C_CHUNK):
        rows = slice(ch * C_CHUNK, (ch + 1) * C_CHUNK)
        for grp in range(C_GROUPS):
            cols = slice(grp * cw, (grp + 1) * cw)
            vm = _dot(ws_ref[grp], vn[rows, cols]) + bs_ref[grp]
            gate_ref[rows, cols] = (u_ref[rows, cols] * vm).astype(BF16)
    o_ref[...] = x_ref[...] + _dot(gate_ref[...], wo_ref[...])


def _gmlp(x, g, w_in, vn_g, ws, bs, wo, tm=512, tn=GMLP_TN):
    n, d = x.shape
    cwid = wo.shape[0]
    resident = lambda a: pl.BlockSpec(a.shape, lambda i: (0,) * a.ndim, pipeline_mode=pl.Buffered(1))
    vec = lambda w: pl.BlockSpec((1, w), lambda i: (0, 0))
    return pl.pallas_call(
        functools.partial(_gmlp_kernel, tm=tm, tn=tn),
        grid=(n // tm,),
        in_specs=[pl.BlockSpec((tm, d), lambda i: (i, 0)), vec(d), resident(w_in), vec(cwid),
                  resident(ws), resident(bs), resident(wo)],
        out_specs=pl.BlockSpec((tm, d), lambda i: (i, 0)),
        out_shape=jax.ShapeDtypeStruct((n, d), F32),
        scratch_shapes=[pltpu.VMEM((tm, cwid), F32), pltpu.VMEM((tm, cwid), F32), pltpu.VMEM((tm, cwid), BF16)],
        compiler_params=_cparams(("parallel",)),
        name="gmlp",
    )(x, g.reshape(1, d), w_in, vn_g.reshape(1, cwid), ws, bs, wo)


def _pad_rows(w, lo, total):
    return jnp.pad(w, ((lo, total - lo - w.shape[0]), (0, 0)))


def _even_layer(x, batch, seq_len, g, w_in, b_qkv, mu, w0, w2, a0, a2, g2, k_k, k_a, r_k,
                gn_g, gn_b, sinks, w_out, v_first, vres):
    aw = A_WIDTH
    a_cols = 3 * aw + DECAY_RANK + ICLR_RANK + GATE_RANK
    pad_low = LOW_PAD - (a_cols - 3 * aw)
    row = lambda t: t.reshape(1, -1)
    gx = row(g)
    wz = jnp.pad(w_in[:, :a_cols], ((0, 0), (0, pad_low))).astype(BF16)
    wq = w_in[:, a_cols:].astype(BF16)
    mu_l = jnp.pad(mu[3 * aw:], (0, pad_low))
    w2p = _pad_rows(w2, 0, LOW_PAD).astype(BF16)
    a2p = _pad_rows(a2, DECAY_RANK, LOW_PAD).astype(BF16)
    g2p = _pad_rows(g2, DECAY_RANK + ICLR_RANK, LOW_PAD).astype(BF16)
    if vres is not None:
        v0, v1, v2 = vres
        rank = v1.shape[1]
        vres_args = (v_first, row(v0), jnp.pad(v1, ((0, 0), (0, VRES_PAD - rank))).astype(BF16),
                     _pad_rows(v2, 0, VRES_PAD).astype(BF16))
    else:
        vres_args = None
    outs = _rwkv(x, gx, wz, batch, seq_len, row(mu[:aw]), row(mu[aw:2 * aw]), row(mu[2 * aw:3 * aw]),
                 row(mu_l), row(w0), w2p, row(a0), a2p, g2p, row(k_k), row(k_a), row(r_k.reshape(-1)),
                 row(gn_g), row(gn_b), vres_args, emit_v=vres is None)
    oa, v = outs if vres is None else (outs[0], None)
    wo = w_out.astype(BF16)
    x = _attn_out(x, gx, wq, row(b_qkv), oa, sinks, wo[:aw], wo[aw:], batch, seq_len)
    return x, v


def _odd_layer(x, g, w_in, vn_g, w_s, b_s, w_out):
    ws = jnp.tril(w_s).astype(BF16)
    return _gmlp(x, g, w_in.astype(BF16), vn_g, ws, b_s[:, :, None], w_out.astype(BF16))


def kernel(x, norm_g, ffn_wg, ffn_wu, ffn_wd, e_w_in, e_b_qkv, e_mu, e_w0, e_w2, e_a0, e_a2, e_g2, e_k_k, e_k_a, e_r_k, e_gn_g, e_gn_b, e_sinks, e_w_out, vres_v0, vres_v1, vres_v2, o_w_in, o_vn_g, o_w_s, o_b_s, o_w_out, final_g):
    batch, seq_len, d = x.shape
    depth = norm_g.shape[0]
    x = x.reshape(batch * seq_len, d)
    n_ffn = 2 * depth
    ffn_norm = lambda k: norm_g[k // 2, 2 * (k % 2)]
    weights = tuple(w[0, 0].astype(BF16) for w in (ffn_wg, ffn_wu, ffn_wd))

    def ffn(x, k, weights):
        nxt = (ffn_wg, ffn_wu, ffn_wd, (k + 1) // 2, (k + 1) % 2) if k + 1 < n_ffn else None
        x, cast = _ffn(x, ffn_norm(k), *weights, g_out=final_g if k + 1 == n_ffn else None, cast_next=nxt)
        return x, cast

    v_first = None
    for layer in range(depth):
        x, weights = ffn(x, 2 * layer, weights)
        if layer % 2 == 0:
            i = layer // 2
            vres = None if i == 0 else (vres_v0[i - 1], vres_v1[i - 1], vres_v2[i - 1])
            x, v_a = _even_layer(x, batch, seq_len, norm_g[layer, 1], e_w_in[i], e_b_qkv[i], e_mu[i],
                                 e_w0[i], e_w2[i], e_a0[i], e_a2[i], e_g2[i], e_k_k[i], e_k_a[i],
                                 e_r_k[i], e_gn_g[i], e_gn_b[i], e_sinks[i], e_w_out[i], v_first, vres)
            if i == 0:
                v_first = v_a
        else:
            j = layer // 2
            x = _odd_layer(x, norm_g[layer, 1], o_w_in[j], o_vn_g[j], o_w_s[j], o_b_s[j], o_w_out[j])
        x, weights = ffn(x, 2 * layer + 1, weights)
    return x.reshape(batch, seq_len, d)
```
